```python
import math
import jax, jax.numpy as jnp
from jax import lax
import numpy as np

D_MODEL = 1024
BATCH = 1
SEQ = 16384
DEPTH = 1

D_MIX = D_MODEL
LRU_WIDTH = D_MIX // 2
LRU_BLOCKS = 8
LRU_BLOCK = LRU_WIDTH // LRU_BLOCKS
CONV_LRU = 4
LRU_C = 8.0
N_DIFF_HEADS = 4
DIFF_HEAD_DIM = 64
DIFF_V_DIM = 2 * DIFF_HEAD_DIM
QK_WIDTH = N_DIFF_HEADS * 2 * DIFF_HEAD_DIM
ATTN_WIDTH = N_DIFF_HEADS * DIFF_V_DIM
D_IN = 2 * QK_WIDTH + ATTN_WIDTH + 2 * LRU_WIDTH
D_FF = 3 * D_MODEL
CONV_FFN = 3
NUM_BUCKETS = 32
MAX_EXACT = NUM_BUCKETS // 2
MAX_DISTANCE = 128
Q_BLOCK = 128
EPS = 1e-6
NEG_INF = -1e30

kernel_name = 'hybrid_rglru_diffattn_convffn_block'


def rms_norm(x, g):
    xf = x.astype(jnp.float32)
    y = xf * lax.rsqrt(jnp.mean(xf * xf, axis=-1, keepdims=True) + EPS)
    return (y * g.astype(jnp.float32)).astype(x.dtype)


def causal_dwconv(x, w, b):
    k = w.shape[0]
    s = x.shape[1]
    xp = jnp.pad(x, ((0, 0), (k - 1, 0), (0, 0)))
    y = b
    for j in range(k):
        y = y + xp[:, j:j + s] * w[j]
    return y


def block_diag_linear(x, w, b):
    xb = x.reshape(x.shape[:-1] + (LRU_BLOCKS, LRU_BLOCK))
    y = jnp.einsum('bsnc,ncd->bsnd', xb, w)
    return y.reshape(x.shape) + b


def rg_lru(x, w_a, b_a, w_x, b_x, lam):
    xf = x.astype(jnp.float32)
    r = jax.nn.sigmoid(block_diag_linear(xf, w_a.astype(jnp.float32), b_a.astype(jnp.float32)))
    i = jax.nn.sigmoid(block_diag_linear(xf, w_x.astype(jnp.float32), b_x.astype(jnp.float32)))
    log_a = -LRU_C * r * jax.nn.softplus(-lam.astype(jnp.float32))
    a = jnp.exp(log_a)
    b_in = jnp.sqrt(-jnp.expm1(2.0 * log_a)) * (i * xf)

    def combine(left, right):
        a1, b1 = left
        a2, b2 = right
        return a1 * a2, a2 * b1 + b2

    _, h = lax.associative_scan(combine, (a, b_in), axis=1)
    return h.astype(x.dtype)


def t5_bucket(rel):
    n = jnp.maximum(rel, 0)
    nf = jnp.maximum(n, 1).astype(jnp.float32)
    large = MAX_EXACT + (jnp.log(nf / MAX_EXACT) / math.log(MAX_DISTANCE / MAX_EXACT)
                         * (NUM_BUCKETS - MAX_EXACT)).astype(jnp.int32)
    large = jnp.minimum(large, NUM_BUCKETS - 1)
    return jnp.where(n < MAX_EXACT, n, large)


def diff_attention(q, k, v, rel_table, lam):
    b, s = q.shape[0], q.shape[1]
    nb = s // Q_BLOCK
    qb = q.reshape(b, nb, Q_BLOCK, N_DIFF_HEADS, 2, DIFF_HEAD_DIM).transpose(1, 0, 2, 3, 4, 5)
    kpos = jnp.arange(s, dtype=jnp.int32)
    scale = DIFF_HEAD_DIM ** -0.5

    def block(args):
        idx, qblk = args
        qpos = idx * Q_BLOCK + jnp.arange(Q_BLOCK, dtype=jnp.int32)
        rel = qpos[:, None] - kpos[None, :]
        bias = jnp.take(rel_table, t5_bucket(rel), axis=0)
        bias = jnp.transpose(bias, (2, 0, 1)).astype(jnp.float32)
        sc = jnp.einsum('bqhmd,bkhmd->bhmqk', qblk, k,
                        preferred_element_type=jnp.float32) * scale + bias[None, :, None]
        sc = jnp.where(rel >= 0, sc, NEG_INF)
        p = jax.nn.softmax(sc, axis=-1)
        wts = p[:, :, 0] - lam * p[:, :, 1]
        return jnp.einsum('bhqk,bkhd->bqhd', wts.astype(v.dtype), v)

    out = lax.map(block, (jnp.arange(nb, dtype=jnp.int32), qb))
    return out.transpose(1, 0, 2, 3, 4).reshape(b, s, N_DIFF_HEADS, DIFF_V_DIM)


def setup_inputs(seed: int = 0) -> dict:
    key = jax.random.key(seed)
    ks = jax.random.split(key, 32)
    f32 = jnp.float32
    nrm = lambda k, shape, s: jax.random.normal(k, shape, f32) * s
    gain = lambda k, shape: 1.0 + 0.02 * jax.random.normal(k, shape, f32)
    u = jax.random.uniform(ks[12], (DEPTH, LRU_WIDTH), f32, 0.9, 0.999)
    a_base = u ** (1.0 / LRU_C)
    lru_lambda = jnp.log(a_base) - jnp.log1p(-a_base)
    return {
        'x': nrm(ks[0], (BATCH, SEQ, D_MODEL), 1.0),
        'c': nrm(ks[1], (BATCH, D_MODEL), 1.0),
        'w_ada': nrm(ks[2], (DEPTH, D_MODEL, 6 * D_MODEL), 0.5 * D_MODEL ** -0.5),
        'b_ada': nrm(ks[3], (DEPTH, 6 * D_MODEL), 0.02),
        'g_norm1': gain(ks[4], (DEPTH, D_MODEL)),
        'w_in': nrm(ks[5], (DEPTH, D_MODEL, D_IN), D_MODEL ** -0.5),
        'conv_lru_w': nrm(ks[6], (DEPTH, CONV_LRU, LRU_WIDTH), 0.5),
        'conv_lru_b': nrm(ks[7], (DEPTH, LRU_WIDTH), 0.02),
        'lru_wa': nrm(ks[8], (DEPTH, LRU_BLOCKS, LRU_BLOCK, LRU_BLOCK), LRU_BLOCK ** -0.5),
        'lru_ba': nrm(ks[9], (DEPTH, LRU_WIDTH), 0.02),
        'lru_wx': nrm(ks[10], (DEPTH, LRU_BLOCKS, LRU_BLOCK, LRU_BLOCK), LRU_BLOCK ** -0.5),
        'lru_bx': nrm(ks[11], (DEPTH, LRU_WIDTH), 0.02),
        'lru_lambda': lru_lambda,
        'lam_q1': nrm(ks[13], (DEPTH, DIFF_HEAD_DIM), 0.1),
        'lam_k1': nrm(ks[14], (DEPTH, DIFF_HEAD_DIM), 0.1),
        'lam_q2': nrm(ks[15], (DEPTH, DIFF_HEAD_DIM), 0.1),
        'lam_k2': nrm(ks[16], (DEPTH, DIFF_HEAD_DIM), 0.1),
        'g_subln': gain(ks[17], (DEPTH, DIFF_V_DIM)),
        'w_out': nrm(ks[18], (DEPTH, D_MIX, D_MODEL), D_MIX ** -0.5),
        'g_norm2': gain(ks[19], (DEPTH, D_MODEL)),
        'w_up': nrm(ks[20], (DEPTH, D_MODEL, 2 * D_FF), D_MODEL ** -0.5),
        'conv_ffn_w': nrm(ks[21], (DEPTH, CONV_FFN, D_FF), 0.5),
        'conv_ffn_b': nrm(ks[22], (DEPTH, D_FF), 0.02),
        'w_down': nrm(ks[23], (DEPTH, D_FF, D_MODEL), D_FF ** -0.5),
        'rel_bias': nrm(ks[24], (NUM_BUCKETS, N_DIFF_HEADS), 0.5),
        'g_final': gain(ks[25], (D_MODEL,)),
    }


def reference(x, c, w_ada, b_ada, g_norm1, w_in, conv_lru_w, conv_lru_b, lru_wa, lru_ba,
              lru_wx, lru_bx, lru_lambda, lam_q1, lam_k1, lam_q2, lam_k2, g_subln, w_out,
              g_norm2, w_up, conv_ffn_w, conv_ffn_b, w_down, rel_bias, g_final):
    b, s, _ = x.shape
    splits = [QK_WIDTH, 2 * QK_WIDTH, 2 * QK_WIDTH + ATTN_WIDTH, 2 * QK_WIDTH + ATTN_WIDTH + LRU_WIDTH]
    cond = jax.nn.silu(c)
    for l in range(DEPTH):
        mod = cond @ w_ada[l] + b_ada[l]
        shift1, scale1, gate1, shift2, scale2, gate2 = jnp.split(mod[:, None, :], 6, axis=-1)

        h = rms_norm(x, g_norm1[l]) * (1.0 + scale1) + shift1
        proj = h @ w_in[l]
        q, k, v, xr, yg = jnp.split(proj, splits, axis=-1)
        q = q.reshape(b, s, N_DIFF_HEADS, 2, DIFF_HEAD_DIM)
        k = k.reshape(b, s, N_DIFF_HEADS, 2, DIFF_HEAD_DIM)
        v = v.reshape(b, s, N_DIFF_HEADS, DIFF_V_DIM)

        lambda_init = 0.8 - 0.6 * math.exp(-0.3 * l)
        lam = (jnp.exp(jnp.sum(lam_q1[l].astype(jnp.float32) * lam_k1[l].astype(jnp.float32)))
               - jnp.exp(jnp.sum(lam_q2[l].astype(jnp.float32) * lam_k2[l].astype(jnp.float32)))
               + lambda_init)
        attn = diff_attention(q, k, v, rel_bias, lam)
        attn = (rms_norm(attn, g_subln[l]) * (1.0 - lambda_init)).reshape(b, s, ATTN_WIDTH)

        xr = causal_dwconv(xr, conv_lru_w[l], conv_lru_b[l])
        lru = rg_lru(xr, lru_wa[l], lru_ba[l], lru_wx[l], lru_bx[l], lru_lambda[l])
        lru = lru * jax.nn.gelu(yg, approximate=True)

        mix = jnp.concatenate([lru, attn], axis=-1) @ w_out[l]
        x = x + gate1 * mix

        h = rms_norm(x, g_norm2[l]) * (1.0 + scale2) + shift2
        a, g = jnp.split(h @ w_up[l], 2, axis=-1)
        a = causal_dwconv(a, conv_ffn_w[l], conv_ffn_b[l])
        ff = (jax.nn.gelu(a, approximate=True) * g) @ w_down[l]
        x = x + gate2 * ff
    return rms_norm(x, g_final)
```

```python
import functools
import math

import jax
import jax.numpy as jnp
from jax import lax
from jax.experimental import pallas as pl
from jax.experimental.pallas import tpu as pltpu

F32 = jnp.float32
BF16 = jnp.bfloat16

D_MODEL = 1024
LRU_WIDTH = 512
LRU_BLOCKS = 8
LRU_BLOCK = LRU_WIDTH // LRU_BLOCKS
CONV_LRU = 4
LRU_C = 8.0
N_HEADS = 4
HEAD_DIM = 64
V_DIM = 2 * HEAD_DIM
QK_WIDTH = N_HEADS * 2 * HEAD_DIM
ATTN_WIDTH = N_HEADS * V_DIM
D_IN = 2 * QK_WIDTH + ATTN_WIDTH + 2 * LRU_WIDTH
D_FF = 3 * D_MODEL
CONV_FFN = 3
NUM_BUCKETS = 32
MAX_EXACT = NUM_BUCKETS // 2
MAX_DISTANCE = 128
EPS = 1e-6
NEG_INF = -1e30
LAMBDA_INIT = 0.8 - 0.6 * math.exp(-0.3 * 0)

LANES = 128
SUBLANES = 8
VMEM_LIMIT = 56 * 1024 * 1024

MOD_TN = 1536
PROJ_TM = 512
ATT_TQ = 256
ATT_TK = 512
LRU_TC = 256
FFN_TM = 256
FFN_CK = 512


def _rms_norm(x, g):
    y = x * lax.rsqrt(jnp.mean(x * x, axis=-1, keepdims=True) + EPS)
    return y * g


def _gelu_tanh(x):
    cdf = 0.5 * (1.0 + jnp.tanh(math.sqrt(2.0 / math.pi) * (x + 0.044715 * (x * x * x))))
    return x * cdf


def _shift_rows(prev8, x, s):
    ext = jnp.concatenate([prev8, x], axis=0)
    return pltpu.roll(ext, s, 0)[SUBLANES:]


def _adaln_kernel(c_ref, w_ref, b_ref, o_ref):
    c = c_ref[...]
    cond = c * jax.nn.sigmoid(c)
    o_ref[...] = jnp.sum(cond * w_ref[...], axis=0, keepdims=True) + b_ref[...]


def _adaln_mod(c_col, w_ada, b_ada):
    d, n = w_ada.shape
    return pl.pallas_call(
        _adaln_kernel,
        grid=(n // MOD_TN,),
        in_specs=[pl.BlockSpec((d, 1), lambda j: (0, 0)),
                  pl.BlockSpec((d, MOD_TN), lambda j: (0, j)),
                  pl.BlockSpec((1, MOD_TN), lambda j: (0, j))],
        out_specs=pl.BlockSpec((1, MOD_TN), lambda j: (0, j)),
        out_shape=jax.ShapeDtypeStruct((1, n), F32),
        compiler_params=pltpu.CompilerParams(vmem_limit_bytes=VMEM_LIMIT),
        name="adaln_mod",
    )(c_col, w_ada, b_ada)


def _in_proj_kernel(x_ref, mod_ref, g_ref, w_ref, q_ref, k_ref, v_ref, xr_ref, yg_ref):
    x = x_ref[...]
    shift1 = mod_ref[:, 0:D_MODEL]
    scale1 = mod_ref[:, D_MODEL:2 * D_MODEL]
    h = _rms_norm(x, g_ref[...]) * (1.0 + scale1) + shift1
    proj = jnp.dot(h.astype(BF16), w_ref[...], preferred_element_type=F32)
    for hd in range(N_HEADS):
        lo = hd * V_DIM
        q_ref[hd] = (proj[:, lo:lo + V_DIM] * (HEAD_DIM ** -0.5)).astype(BF16)
        k_ref[hd] = proj[:, QK_WIDTH + lo:QK_WIDTH + lo + V_DIM].astype(BF16)
        v_ref[hd] = proj[:, 2 * QK_WIDTH + lo:2 * QK_WIDTH + lo + V_DIM].astype(BF16)
    base = 2 * QK_WIDTH + ATTN_WIDTH
    xr_ref[...] = proj[:, base:base + LRU_WIDTH]
    yg_ref[...] = proj[:, base + LRU_WIDTH:base + 2 * LRU_WIDTH]


def _in_proj(x2d, mod, g1, w_in_bf16):
    s = x2d.shape[0]
    tm = min(PROJ_TM, s)
    head_spec = pl.BlockSpec((N_HEADS, tm, V_DIM), lambda i: (0, i, 0))
    row_spec = pl.BlockSpec((tm, LRU_WIDTH), lambda i: (i, 0))
    return pl.pallas_call(
        _in_proj_kernel,
        grid=(s // tm,),
        in_specs=[pl.BlockSpec((tm, D_MODEL), lambda i: (i, 0)),
                  pl.BlockSpec((1, 6 * D_MODEL), lambda i: (0, 0)),
                  pl.BlockSpec((1, D_MODEL), lambda i: (0, 0)),
                  pl.BlockSpec((D_MODEL, D_IN), lambda i: (0, 0))],
        out_specs=[head_spec, head_spec, head_spec, row_spec, row_spec],
        out_shape=[jax.ShapeDtypeStruct((N_HEADS, s, V_DIM), BF16)] * 3
        + [jax.ShapeDtypeStruct((s, LRU_WIDTH), F32)] * 2,
        compiler_params=pltpu.CompilerParams(dimension_semantics=("parallel",),
                                             vmem_limit_bytes=VMEM_LIMIT),
        name="in_proj",
    )(x2d, mod, g1, w_in_bf16)


def _bias_tiles_kernel(table_ref, o_ref, *, tq, tk):
    hd = pl.program_id(0)
    dd = pl.program_id(1)
    row = lax.broadcasted_iota(jnp.int32, (tq, tk), 0)
    col = lax.broadcasted_iota(jnp.int32, (tq, tk), 1)
    rel = dd * tq + row - col
    n = jnp.maximum(rel, 0)
    nf = jnp.maximum(n, 1).astype(F32)
    large = MAX_EXACT + (jnp.log(nf / MAX_EXACT) / math.log(MAX_DISTANCE / MAX_EXACT)
                         * (NUM_BUCKETS - MAX_EXACT)).astype(jnp.int32)
    large = jnp.minimum(large, NUM_BUCKETS - 1)
    bucket = jnp.where(n < MAX_EXACT, n, large)
    far = table_ref[hd * NUM_BUCKETS + NUM_BUCKETS - 1]
    val = jnp.zeros((tq, tk), F32)
    for b in range(NUM_BUCKETS):
        val = jnp.where(bucket == b, table_ref[hd * NUM_BUCKETS + b] - far, val)
    o_ref[0, 0] = jnp.where(rel >= 0, val, NEG_INF)


def _bias_tiles(table_flat, tq, tk):
    nd = tk // tq + 1
    return pl.pallas_call(
        functools.partial(_bias_tiles_kernel, tq=tq, tk=tk),
        grid=(N_HEADS, nd),
        in_specs=[pl.BlockSpec(memory_space=pltpu.SMEM)],
        out_specs=pl.BlockSpec((1, 1, tq, tk), lambda h, d: (h, d, 0, 0)),
        out_shape=jax.ShapeDtypeStruct((N_HEADS, nd, tq, tk), F32),
        name="bias_tiles",
    )(table_flat)


def _attn_kernel(q_ref, k_ref, v_ref, bias_ref, lamv_ref, gs_ref, o_ref, m_sc, l_sc, acc_sc, *, tq, tk):
    i = pl.program_id(1)
    q = q_ref[0]
    lane = lax.broadcasted_iota(jnp.int32, q.shape, 1)
    zero = jnp.zeros_like(q)
    qs = jnp.concatenate([jnp.where(lane < HEAD_DIM, q, zero),
                          jnp.where(lane >= HEAD_DIM, q, zero)], axis=0)

    m_sc[...] = jnp.full(m_sc.shape, NEG_INF, F32)
    l_sc[...] = jnp.zeros(l_sc.shape, F32)
    acc_sc[...] = jnp.zeros(acc_sc.shape, F32)

    def step(j, bias):
        start = pl.multiple_of(j * tk, tk)
        kb = k_ref[0, pl.ds(start, tk), :]
        vb = v_ref[0, pl.ds(start, tk), :]
        s = lax.dot_general(qs, kb, (((1,), (1,)), ((), ())), preferred_element_type=F32)
        if bias is not None:
            s = (s.reshape(2, tq, tk) + bias[None]).reshape(2 * tq, tk)
        m_prev = m_sc[...]
        m_new = jnp.maximum(m_prev, jnp.max(s, axis=1, keepdims=True))
        alpha = jnp.exp(m_prev - m_new)
        p = jnp.exp(s - m_new)
        l_sc[...] = alpha * l_sc[...] + jnp.sum(p, axis=1, keepdims=True)
        acc_sc[...] = alpha * acc_sc[...] + jnp.dot(p.astype(BF16), vb, preferred_element_type=F32)
        m_sc[...] = m_new

    q_start = i * tq
    n_far = jnp.maximum(q_start - (MAX_DISTANCE - 1), 0) // tk
    last = (q_start + tq - 1) // tk

    def far_body(j, carry):
        step(j, None)
        return carry

    lax.fori_loop(0, n_far, far_body, 0)

    def near_body(j, carry):
        dd = (q_start - j * tk) // tq
        step(j, bias_ref[0, dd])
        return carry

    lax.fori_loop(n_far, last + 1, near_body, 0)

    lv = lamv_ref[...]
    d1 = jnp.sum(lv[0:1] * lv[1:2], axis=1, keepdims=True)
    d2 = jnp.sum(lv[2:3] * lv[3:4], axis=1, keepdims=True)
    lam = jnp.exp(d1) - jnp.exp(d2) + LAMBDA_INIT

    out = acc_sc[...] / l_sc[...]
    diff = out[:tq] - lam * out[tq:]
    o_ref[...] = (_rms_norm(diff, gs_ref[...]) * (1.0 - LAMBDA_INIT)).astype(o_ref.dtype)


def _diff_attn(q, k, v, bias, lamv, g_subln):
    s = q.shape[1]
    tq = min(ATT_TQ, s)
    tk = min(ATT_TK, s)
    nd = bias.shape[1]
    return pl.pallas_call(
        functools.partial(_attn_kernel, tq=tq, tk=tk),
        grid=(N_HEADS, s // tq),
        in_specs=[pl.BlockSpec((1, tq, V_DIM), lambda h, i: (h, i, 0)),
                  pl.BlockSpec((1, s, V_DIM), lambda h, i: (h, 0, 0)),
                  pl.BlockSpec((1, s, V_DIM), lambda h, i: (h, 0, 0)),
                  pl.BlockSpec((1, nd, tq, tk), lambda h, i: (h, 0, 0, 0)),
                  pl.BlockSpec((4, HEAD_DIM), lambda h, i: (0, 0)),
                  pl.BlockSpec((1, V_DIM), lambda h, i: (0, 0))],
        out_specs=pl.BlockSpec((tq, V_DIM), lambda h, i: (i, h)),
        out_shape=jax.ShapeDtypeStruct((s, ATTN_WIDTH), BF16),
        scratch_shapes=[pltpu.VMEM((2 * tq, 1), F32),
                        pltpu.VMEM((2 * tq, 1), F32),
                        pltpu.VMEM((2 * tq, V_DIM), F32)],
        compiler_params=pltpu.CompilerParams(dimension_semantics=("parallel", "parallel"),
                                             vmem_limit_bytes=VMEM_LIMIT),
        name="diff_attn",
    )(q, k, v, bias, lamv, g_subln)


def _lru_kernel(xr_ref, yg_ref, cw_ref, cb_ref, wg_ref, bg_ref, lam_ref, o_ref, tail_sc, h_sc, *, tc):
    @pl.when(pl.program_id(0) == 0)
    def _():
        tail_sc[...] = jnp.zeros(tail_sc.shape, F32)
        h_sc[...] = jnp.zeros(h_sc.shape, F32)

    x = xr_ref[...]
    prev8 = tail_sc[...]
    cw = cw_ref[...]
    xc = cb_ref[...]
    for j in range(CONV_LRU):
        sh = CONV_LRU - 1 - j
        xs = x if sh == 0 else _shift_rows(prev8, x, sh)
        xc = xc + xs * cw[j:j + 1]
    tail_sc[...] = x[tc - SUBLANES:]

    gates = jnp.dot(xc.astype(BF16), wg_ref[...], preferred_element_type=F32) + bg_ref[...]
    r = jax.nn.sigmoid(gates[:, :LRU_WIDTH])
    ig = jax.nn.sigmoid(gates[:, LRU_WIDTH:])
    z = -lam_ref[...]
    softplus = jnp.maximum(z, 0.0) + jnp.log1p(jnp.exp(-jnp.abs(z)))
    log_a = -LRU_C * r * softplus
    a = jnp.exp(log_a)
    th = jnp.tanh(log_a)
    b = jnp.sqrt(-2.0 * th / (1.0 - th)) * (ig * xc)

    row = lax.broadcasted_iota(jnp.int32, a.shape, 0)
    sh = 1
    while sh < tc:
        valid = row >= sh
        a_s = jnp.where(valid, pltpu.roll(a, sh, 0), 1.0)
        b_s = jnp.where(valid, pltpu.roll(b, sh, 0), 0.0)
        b = a * b_s + b
        a = a * a_s
        sh *= 2
    h = a * h_sc[0:1] + b
    h_sc[0:1] = h[tc - 1:tc]
    o_ref[...] = (h * _gelu_tanh(yg_ref[...])).astype(o_ref.dtype)


def _rg_lru(xr, yg, conv_w, conv_b, w_gates_bf16, b_gates, lam):
    s = xr.shape[0]
    tc = min(LRU_TC, s)
    row_spec = pl.BlockSpec((tc, LRU_WIDTH), lambda i: (i, 0))
    const = lambda shape: pl.BlockSpec(shape, lambda i: (0, 0))
    return pl.pallas_call(
        functools.partial(_lru_kernel, tc=tc),
        grid=(s // tc,),
        in_specs=[row_spec, row_spec,
                  const((CONV_LRU, LRU_WIDTH)), const((1, LRU_WIDTH)),
                  const((LRU_WIDTH, 2 * LRU_WIDTH)), const((1, 2 * LRU_WIDTH)),
                  const((1, LRU_WIDTH))],
        out_specs=row_spec,
        out_shape=jax.ShapeDtypeStruct((s, LRU_WIDTH), BF16),
        scratch_shapes=[pltpu.VMEM((SUBLANES, LRU_WIDTH), F32),
                        pltpu.VMEM((SUBLANES, LRU_WIDTH), F32)],
        compiler_params=pltpu.CompilerParams(dimension_semantics=("arbitrary",),
                                             vmem_limit_bytes=VMEM_LIMIT),
        name="rg_lru",
    )(xr, yg, conv_w, conv_b, w_gates_bf16, b_gates, lam)


def _ffn_kernel(x_ref, lru_ref, attn_ref, mod_ref, wo_ref, g2_ref, wup_ref, cw_ref, cb_ref, wdn_ref, gf_ref,
                o_ref, tail_sc, *, tm):
    @pl.when(pl.program_id(0) == 0)
    def _():
        tail_sc[...] = jnp.zeros(tail_sc.shape, F32)

    gate1 = mod_ref[:, 2 * D_MODEL:3 * D_MODEL]
    shift2 = mod_ref[:, 3 * D_MODEL:4 * D_MODEL]
    scale2 = mod_ref[:, 4 * D_MODEL:5 * D_MODEL]
    gate2 = mod_ref[:, 5 * D_MODEL:6 * D_MODEL]

    mix = (jnp.dot(lru_ref[...], wo_ref[0:LRU_WIDTH, :], preferred_element_type=F32)
           + jnp.dot(attn_ref[...], wo_ref[LRU_WIDTH:, :], preferred_element_type=F32))
    x1 = x_ref[...] + gate1 * mix
    h2 = (_rms_norm(x1, g2_ref[...]) * (1.0 + scale2) + shift2).astype(BF16)

    ff = jnp.zeros((tm, D_MODEL), F32)
    for c in range(D_FF // FFN_CK):
        lo = c * FFN_CK
        a = jnp.dot(h2, wup_ref[:, lo:lo + FFN_CK], preferred_element_type=F32)
        g = jnp.dot(h2, wup_ref[:, D_FF + lo:D_FF + lo + FFN_CK], preferred_element_type=F32)
        prev8 = tail_sc[:, lo:lo + FFN_CK]
        ac = cb_ref[:, lo:lo + FFN_CK]
        for j in range(CONV_FFN):
            sh = CONV_FFN - 1 - j
            a_s = a if sh == 0 else _shift_rows(prev8, a, sh)
            ac = ac + a_s * cw_ref[j:j + 1, lo:lo + FFN_CK]
        tail_sc[:, lo:lo + FFN_CK] = a[tm - SUBLANES:]
        u = (_gelu_tanh(ac) * g).astype(BF16)
        ff = ff + jnp.dot(u, wdn_ref[lo:lo + FFN_CK, :], preferred_element_type=F32)

    x2 = x1 + gate2 * ff
    o_ref[...] = _rms_norm(x2, gf_ref[...])


def _out_ffn(x2d, lru, attn, mod, w_out, g2, w_up, conv_w, conv_b, w_down, g_final):
    s = x2d.shape[0]
    tm = min(FFN_TM, s)
    row = lambda w: pl.BlockSpec((tm, w), lambda i: (i, 0))
    const = lambda shape: pl.BlockSpec(shape, lambda i: (0, 0), pipeline_mode=pl.Buffered(1))
    return pl.pallas_call(
        functools.partial(_ffn_kernel, tm=tm),
        grid=(s // tm,),
        in_specs=[row(D_MODEL), row(LRU_WIDTH), row(ATTN_WIDTH),
                  const((1, 6 * D_MODEL)),
                  const((D_MODEL, D_MODEL)), const((1, D_MODEL)),
                  const((D_MODEL, 2 * D_FF)),
                  const((CONV_FFN, D_FF)), const((1, D_FF)),
                  const((D_FF, D_MODEL)), const((1, D_MODEL))],
        out_specs=row(D_MODEL),
        out_shape=jax.ShapeDtypeStruct((s, D_MODEL), F32),
        scratch_shapes=[pltpu.VMEM((SUBLANES, D_FF), F32)],
        compiler_params=pltpu.CompilerParams(dimension_semantics=("arbitrary",),
                                             vmem_limit_bytes=VMEM_LIMIT),
        name="out_ffn",
    )(x2d, lru, attn, mod, w_out, g2, w_up, conv_w, conv_b, w_down, g_final)


def _block_diag(w):
    nb, bs, _ = w.shape
    eye = jnp.eye(nb, dtype=w.dtype)
    return (w[:, :, None, :] * eye[:, None, :, None]).reshape(nb * bs, nb * bs)


def kernel(x, c, w_ada, b_ada, g_norm1, w_in, conv_lru_w, conv_lru_b, lru_wa, lru_ba, lru_wx, lru_bx, lru_lambda,
           lam_q1, lam_k1, lam_q2, lam_k2, g_subln, w_out, g_norm2, w_up, conv_ffn_w, conv_ffn_b, w_down, rel_bias,
           g_final):
    b, s, d = x.shape
    assert b == 1 and d == D_MODEL and w_ada.shape[0] == 1
    x2d = x.reshape(s, d)
    row = lambda a: a.reshape(1, -1)

    mod = _adaln_mod(c.reshape(d, 1), w_ada[0], row(b_ada[0]))

    q, k, v, xr, yg = _in_proj(x2d, mod, row(g_norm1[0]), w_in[0].astype(BF16))

    tq, tk = min(ATT_TQ, s), min(ATT_TK, s)
    bias = _bias_tiles(rel_bias.T.reshape(-1), tq, tk)
    lamv = jnp.stack([lam_q1[0], lam_k1[0], lam_q2[0], lam_k2[0]])
    attn = _diff_attn(q, k, v, bias, lamv, row(g_subln[0]))

    w_gates = jnp.concatenate([_block_diag(lru_wa[0]), _block_diag(lru_wx[0])], axis=1).astype(BF16)
    b_gates = jnp.concatenate([lru_ba[0], lru_bx[0]]).reshape(1, -1)
    lru = _rg_lru(xr, yg, conv_lru_w[0], row(conv_lru_b[0]), w_gates, b_gates, row(lru_lambda[0]))

    out = _out_ffn(x2d, lru, attn, mod, w_out[0].astype(BF16), row(g_norm2[0]), w_up[0].astype(BF16),
                   conv_ffn_w[0], row(conv_ffn_b[0]), w_down[0].astype(BF16), row(g_final))
    return out.reshape(b, s, d)
```

```python
import functools
import math

import jax
import jax.numpy as jnp
from jax import lax
from jax.experimental import pallas as pl
from jax.experimental.pallas import tpu as pltpu

F32 = jnp.float32
BF16 = jnp.bfloat16

D_MODEL = 1024
LRU_WIDTH = 512
LRU_BLOCKS = 8
LRU_BLOCK = LRU_WIDTH // LRU_BLOCKS
CONV_LRU = 4
LRU_C = 8.0
N_HEADS = 4
HEAD_DIM = 64
V_DIM = 2 * HEAD_DIM
QK_WIDTH = N_HEADS * 2 * HEAD_DIM
ATTN_WIDTH = N_HEADS * V_DIM
D_IN = 2 * QK_WIDTH + ATTN_WIDTH + 2 * LRU_WIDTH
D_FF = 3 * D_MODEL
CONV_FFN = 3
NUM_BUCKETS = 32
MAX_EXACT = NUM_BUCKETS // 2
MAX_DISTANCE = 128
EPS = 1e-6
NEG_INF = -1e30
LAMBDA_INIT = 0.8 - 0.6 * math.exp(-0.3 * 0)

LANES = 128
SUBLANES = 8
VMEM_LIMIT = 56 * 1024 * 1024

MOD_TN = 1536
PROJ_TM = 512
ATT_TQ = 256
ATT_TK = 512
LRU_TC = 256
FFN_TM = 256
FFN_CK = 512


def _rms_norm(x, g):
    y = x * lax.rsqrt(jnp.mean(x * x, axis=-1, keepdims=True) + EPS)
    return y * g


def _gelu_tanh(x):
    cdf = 0.5 * (1.0 + jnp.tanh(math.sqrt(2.0 / math.pi) * (x + 0.044715 * (x * x * x))))
    return x * cdf


def _shift_rows(prev8, x, s):
    ext = jnp.concatenate([prev8, x], axis=0)
    return pltpu.roll(ext, s, 0)[SUBLANES:]


def _adaln_kernel(c_ref, w_ref, b_ref, o_ref):
    c = c_ref[...]
    cond = c * jax.nn.sigmoid(c)
    o_ref[...] = jnp.sum(cond * w_ref[...], axis=0, keepdims=True) + b_ref[...]


def _adaln_mod(c_col, w_ada, b_ada):
    d, n = w_ada.shape
    return pl.pallas_call(
        _adaln_kernel,
        grid=(n // MOD_TN,),
        in_specs=[pl.BlockSpec((d, 1), lambda j: (0, 0)),
                  pl.BlockSpec((d, MOD_TN), lambda j: (0, j)),
                  pl.BlockSpec((1, MOD_TN), lambda j: (0, j))],
        out_specs=pl.BlockSpec((1, MOD_TN), lambda j: (0, j)),
        out_shape=jax.ShapeDtypeStruct((1, n), F32),
        compiler_params=pltpu.CompilerParams(vmem_limit_bytes=VMEM_LIMIT),
        name="adaln_mod",
    )(c_col, w_ada, b_ada)


def _in_proj_kernel(x_ref, mod_ref, g_ref, w_ref, q_ref, k_ref, v_ref, xr_ref, yg_ref):
    x = x_ref[...]
    shift1 = mod_ref[:, 0:D_MODEL]
    scale1 = mod_ref[:, D_MODEL:2 * D_MODEL]
    h = _rms_norm(x, g_ref[...]) * (1.0 + scale1) + shift1
    proj = jnp.dot(h.astype(BF16), w_ref[...], preferred_element_type=F32)
    lane = lax.broadcasted_iota(jnp.int32, (x.shape[0], V_DIM), 1)
    ones_col = jnp.where(lane == 0, 1.0, 0.0).astype(BF16)
    for hd in range(N_HEADS):
        lo = hd * V_DIM
        q_ref[hd] = (proj[:, lo:lo + V_DIM] * (HEAD_DIM ** -0.5)).astype(BF16)
        k_ref[hd] = proj[:, QK_WIDTH + lo:QK_WIDTH + lo + V_DIM].astype(BF16)
        v_ref[hd] = jnp.concatenate(
            [proj[:, 2 * QK_WIDTH + lo:2 * QK_WIDTH + lo + V_DIM].astype(BF16), ones_col], axis=1)
    base = 2 * QK_WIDTH + ATTN_WIDTH
    xr_ref[...] = proj[:, base:base + LRU_WIDTH]
    yg_ref[...] = proj[:, base + LRU_WIDTH:base + 2 * LRU_WIDTH]


def _in_proj(x2d, mod, g1, w_in_bf16):
    s = x2d.shape[0]
    tm = min(PROJ_TM, s)
    head_spec = pl.BlockSpec((N_HEADS, tm, V_DIM), lambda i: (0, i, 0))
    row_spec = pl.BlockSpec((tm, LRU_WIDTH), lambda i: (i, 0))
    return pl.pallas_call(
        _in_proj_kernel,
        grid=(s // tm,),
        in_specs=[pl.BlockSpec((tm, D_MODEL), lambda i: (i, 0)),
                  pl.BlockSpec((1, 6 * D_MODEL), lambda i: (0, 0)),
                  pl.BlockSpec((1, D_MODEL), lambda i: (0, 0)),
                  pl.BlockSpec((D_MODEL, D_IN), lambda i: (0, 0))],
        out_specs=[head_spec, head_spec, pl.BlockSpec((N_HEADS, tm, 2 * V_DIM), lambda i: (0, i, 0)),
                   row_spec, row_spec],
        out_shape=[jax.ShapeDtypeStruct((N_HEADS, s, V_DIM), BF16)] * 2
        + [jax.ShapeDtypeStruct((N_HEADS, s, 2 * V_DIM), BF16)]
        + [jax.ShapeDtypeStruct((s, LRU_WIDTH), F32)] * 2,
        compiler_params=pltpu.CompilerParams(dimension_semantics=("parallel",),
                                             vmem_limit_bytes=VMEM_LIMIT),
        name="in_proj",
    )(x2d, mod, g1, w_in_bf16)


def _bias_tiles_kernel(table_ref, o_ref, *, tq, tk, nd):
    hd = pl.program_id(0)
    dd = pl.program_id(1)
    row = lax.broadcasted_iota(jnp.int32, (tq, tk), 0)
    col = lax.broadcasted_iota(jnp.int32, (tq, tk), 1)
    rel = dd * tq + row - col
    n = jnp.maximum(rel, 0)
    nf = jnp.maximum(n, 1).astype(F32)
    large = MAX_EXACT + (jnp.log(nf / MAX_EXACT) / math.log(MAX_DISTANCE / MAX_EXACT)
                         * (NUM_BUCKETS - MAX_EXACT)).astype(jnp.int32)
    large = jnp.minimum(large, NUM_BUCKETS - 1)
    bucket = jnp.where(n < MAX_EXACT, n, large)
    far = table_ref[hd * NUM_BUCKETS + NUM_BUCKETS - 1]
    val = jnp.zeros((tq, tk), F32)
    for b in range(NUM_BUCKETS):
        val = jnp.where(bucket == b, table_ref[hd * NUM_BUCKETS + b] - far, val)
    tile = jnp.where(rel >= 0, val, NEG_INF)
    tile = jnp.where(dd == nd, 0.0, tile)
    o_ref[0, 0] = jnp.where(dd == nd + 1, NEG_INF, tile)


def _bias_tiles(table_flat, tq, tk):
    nd = tk // tq + 1
    return pl.pallas_call(
        functools.partial(_bias_tiles_kernel, tq=tq, tk=tk, nd=nd),
        grid=(N_HEADS, nd + 2),
        in_specs=[pl.BlockSpec(memory_space=pltpu.SMEM)],
        out_specs=pl.BlockSpec((1, 1, tq, tk), lambda h, d: (h, d, 0, 0)),
        out_shape=jax.ShapeDtypeStruct((N_HEADS, nd + 2, tq, tk), F32),
        name="bias_tiles",
    )(table_flat)


def _attn_kernel(q_ref, k_ref, v_ref, bias_ref, lamv_ref, gs_ref, o_ref, sa_sc, sb_sc, m_sc, acc_sc,
                 *, tq, tk, nd):
    i = pl.program_id(1)
    q = q_ref[0]
    lane = lax.broadcasted_iota(jnp.int32, q.shape, 1)
    zero = jnp.zeros_like(q)
    qs = jnp.concatenate([jnp.where(lane < HEAD_DIM, q, zero),
                          jnp.where(lane >= HEAD_DIM, q, zero)], axis=0)

    q_start = i * tq
    n_far = jnp.maximum(q_start - (MAX_DISTANCE - 1), 0) // tk
    last = (q_start + tq - 1) // tk
    n_pairs = (last + 2) // 2

    def scores(j):
        jc = jnp.minimum(j, last)
        kb = k_ref[0, pl.ds(pl.multiple_of(jc * tk, tk), tk), :]
        s = lax.dot_general(qs, kb, (((1,), (1,)), ((), ())), preferred_element_type=F32)
        dd = (q_start - jc * tk) // tq
        idx = jnp.where(j > last, nd + 1, jnp.where(j < n_far, nd, dd))
        return (s.reshape(2, tq, tk) + bias_ref[0, idx][None]).reshape(2 * tq, tk)

    def accumulate(s_ref, j):
        jc = jnp.minimum(j, last)
        vb = v_ref[0, pl.ds(pl.multiple_of(jc * tk, tk), tk), :]
        m_prev = m_sc[...]
        m_new = jnp.maximum(m_prev, jnp.max(s_ref[...], axis=1, keepdims=True))
        alpha = jnp.exp(m_prev - m_new)
        p = jnp.exp(s_ref[...] - pltpu.repeat(m_new, tk // LANES, axis=1))
        acc_sc[...] = (pltpu.repeat(alpha, 2, axis=1) * acc_sc[...]
                       + jnp.dot(p.astype(BF16), vb, preferred_element_type=F32))
        m_sc[...] = m_new

    m_sc[...] = jnp.full(m_sc.shape, NEG_INF, F32)
    acc_sc[...] = jnp.zeros(acc_sc.shape, F32)
    sa_sc[...] = scores(0)

    def pair_body(t, carry):
        j = 2 * t
        sb_sc[...] = scores(j + 1)
        accumulate(sa_sc, j)
        sa_sc[...] = scores(j + 2)
        accumulate(sb_sc, j + 1)
        return carry

    lax.fori_loop(0, n_pairs, pair_body, 0)

    lv = lamv_ref[...]
    d1 = jnp.sum(lv[0:1] * lv[1:2], axis=1, keepdims=True)
    d2 = jnp.sum(lv[2:3] * lv[3:4], axis=1, keepdims=True)
    lam = jnp.exp(d1) - jnp.exp(d2) + LAMBDA_INIT

    acc = acc_sc[...]
    out = acc[:, :V_DIM] / acc[:, V_DIM:V_DIM + 1]
    diff = out[:tq] - lam * out[tq:]
    o_ref[...] = (_rms_norm(diff, gs_ref[...]) * (1.0 - LAMBDA_INIT)).astype(o_ref.dtype)


def _diff_attn(q, k, v_aug, bias, lamv, g_subln):
    s = q.shape[1]
    tq = min(ATT_TQ, s)
    tk = min(ATT_TK, s)
    nd = bias.shape[1] - 2
    return pl.pallas_call(
        functools.partial(_attn_kernel, tq=tq, tk=tk, nd=nd),
        grid=(N_HEADS, s // tq),
        in_specs=[pl.BlockSpec((1, tq, V_DIM), lambda h, i: (h, i, 0)),
                  pl.BlockSpec((1, s, V_DIM), lambda h, i: (h, 0, 0)),
                  pl.BlockSpec((1, s, 2 * V_DIM), lambda h, i: (h, 0, 0)),
                  pl.BlockSpec((1, nd + 2, tq, tk), lambda h, i: (h, 0, 0, 0)),
                  pl.BlockSpec((4, HEAD_DIM), lambda h, i: (0, 0)),
                  pl.BlockSpec((1, V_DIM), lambda h, i: (0, 0))],
        out_specs=pl.BlockSpec((tq, V_DIM), lambda h, i: (i, h)),
        out_shape=jax.ShapeDtypeStruct((s, ATTN_WIDTH), BF16),
        scratch_shapes=[pltpu.VMEM((2 * tq, tk), F32),
                        pltpu.VMEM((2 * tq, tk), F32),
                        pltpu.VMEM((2 * tq, LANES), F32),
                        pltpu.VMEM((2 * tq, 2 * V_DIM), F32)],
        compiler_params=pltpu.CompilerParams(dimension_semantics=("parallel", "parallel"),
                                             vmem_limit_bytes=VMEM_LIMIT),
        name="diff_attn",
    )(q, k, v_aug, bias, lamv, g_subln)


def _lru_kernel(xr_ref, yg_ref, cw_ref, cb_ref, wg_ref, bg_ref, lam_ref, o_ref, tail_sc, h_sc, *, tc):
    @pl.when(pl.program_id(0) == 0)
    def _():
        tail_sc[...] = jnp.zeros(tail_sc.shape, F32)
        h_sc[...] = jnp.zeros(h_sc.shape, F32)

    x = xr_ref[...]
    prev8 = tail_sc[...]
    cw = cw_ref[...]
    xc = cb_ref[...]
    for j in range(CONV_LRU):
        sh = CONV_LRU - 1 - j
        xs = x if sh == 0 else _shift_rows(prev8, x, sh)
        xc = xc + xs * cw[j:j + 1]
    tail_sc[...] = x[tc - SUBLANES:]

    gates = jnp.dot(xc.astype(BF16), wg_ref[...], preferred_element_type=F32) + bg_ref[...]
    r = jax.nn.sigmoid(gates[:, :LRU_WIDTH])
    ig = jax.nn.sigmoid(gates[:, LRU_WIDTH:])
    z = -lam_ref[...]
    softplus = jnp.maximum(z, 0.0) + jnp.log1p(jnp.exp(-jnp.abs(z)))
    log_a = -LRU_C * r * softplus
    a = jnp.exp(log_a)
    th = jnp.tanh(log_a)
    b = jnp.sqrt(-2.0 * th / (1.0 - th)) * (ig * xc)

    row = lax.broadcasted_iota(jnp.int32, a.shape, 0)
    sh = 1
    while sh < tc:
        valid = row >= sh
        a_s = jnp.where(valid, pltpu.roll(a, sh, 0), 1.0)
        b_s = jnp.where(valid, pltpu.roll(b, sh, 0), 0.0)
        b = a * b_s + b
        a = a * a_s
        sh *= 2
    h = a * h_sc[0:1] + b
    h_sc[0:1] = h[tc - 1:tc]
    o_ref[...] = (h * _gelu_tanh(yg_ref[...])).astype(o_ref.dtype)


def _rg_lru(xr, yg, conv_w, conv_b, w_gates_bf16, b_gates, lam):
    s = xr.shape[0]
    tc = min(LRU_TC, s)
    row_spec = pl.BlockSpec((tc, LRU_WIDTH), lambda i: (i, 0))
    const = lambda shape: pl.BlockSpec(shape, lambda i: (0, 0))
    return pl.pallas_call(
        functools.partial(_lru_kernel, tc=tc),
        grid=(s // tc,),
        in_specs=[row_spec, row_spec,
                  const((CONV_LRU, LRU_WIDTH)), const((1, LRU_WIDTH)),
                  const((LRU_WIDTH, 2 * LRU_WIDTH)), const((1, 2 * LRU_WIDTH)),
                  const((1, LRU_WIDTH))],
        out_specs=row_spec,
        out_shape=jax.ShapeDtypeStruct((s, LRU_WIDTH), BF16),
        scratch_shapes=[pltpu.VMEM((SUBLANES, LRU_WIDTH), F32),
                        pltpu.VMEM((SUBLANES, LRU_WIDTH), F32)],
        compiler_params=pltpu.CompilerParams(dimension_semantics=("arbitrary",),
                                             vmem_limit_bytes=VMEM_LIMIT),
        name="rg_lru",
    )(xr, yg, conv_w, conv_b, w_gates_bf16, b_gates, lam)


def _ffn_kernel(x_ref, lru_ref, attn_ref, mod_ref, wo_ref, g2_ref, wup_ref, cw_ref, cb_ref, wdn_ref, gf_ref,
                o_ref, tail_sc, *, tm):
    @pl.when(pl.program_id(0) == 0)
    def _():
        tail_sc[...] = jnp.zeros(tail_sc.shape, F32)

    gate1 = mod_ref[:, 2 * D_MODEL:3 * D_MODEL]
    shift2 = mod_ref[:, 3 * D_MODEL:4 * D_MODEL]
    scale2 = mod_ref[:, 4 * D_MODEL:5 * D_MODEL]
    gate2 = mod_ref[:, 5 * D_MODEL:6 * D_MODEL]

    mix = (jnp.dot(lru_ref[...], wo_ref[0:LRU_WIDTH, :], preferred_element_type=F32)
           + jnp.dot(attn_ref[...], wo_ref[LRU_WIDTH:, :], preferred_element_type=F32))
    x1 = x_ref[...] + gate1 * mix
    h2 = (_rms_norm(x1, g2_ref[...]) * (1.0 + scale2) + shift2).astype(BF16)

    ff = jnp.zeros((tm, D_MODEL), F32)
    for c in range(D_FF // FFN_CK):
        lo = c * FFN_CK
        a = jnp.dot(h2, wup_ref[:, lo:lo + FFN_CK], preferred_element_type=F32)
        g = jnp.dot(h2, wup_ref[:, D_FF + lo:D_FF + lo + FFN_CK], preferred_element_type=F32)
        prev8 = tail_sc[:, lo:lo + FFN_CK]
        ac = cb_ref[:, lo:lo + FFN_CK]
        for j in range(CONV_FFN):
            sh = CONV_FFN - 1 - j
            a_s = a if sh == 0 else _shift_rows(prev8, a, sh)
            ac = ac + a_s * cw_ref[j:j + 1, lo:lo + FFN_CK]
        tail_sc[:, lo:lo + FFN_CK] = a[tm - SUBLANES:]
        u = (_gelu_tanh(ac) * g).astype(BF16)
        ff = ff + jnp.dot(u, wdn_ref[lo:lo + FFN_CK, :], preferred_element_type=F32)

    x2 = x1 + gate2 * ff
    o_ref[...] = _rms_norm(x2, gf_ref[...])


def _out_ffn(x2d, lru, attn, mod, w_out, g2, w_up, conv_w, conv_b, w_down, g_final):
    s = x2d.shape[0]
    tm = min(FFN_TM, s)
    row = lambda w: pl.BlockSpec((tm, w), lambda i: (i, 0))
    const = lambda shape: pl.BlockSpec(shape, lambda i: (0, 0), pipeline_mode=pl.Buffered(1))
    return pl.pallas_call(
        functools.partial(_ffn_kernel, tm=tm),
        grid=(s // tm,),
        in_specs=[row(D_MODEL), row(LRU_WIDTH), row(ATTN_WIDTH),
                  const((1, 6 * D_MODEL)),
                  const((D_MODEL, D_MODEL)), const((1, D_MODEL)),
                  const((D_MODEL, 2 * D_FF)),
                  const((CONV_FFN, D_FF)), const((1, D_FF)),
                  const((D_FF, D_MODEL)), const((1, D_MODEL))],
        out_specs=row(D_MODEL),
        out_shape=jax.ShapeDtypeStruct((s, D_MODEL), F32),
        scratch_shapes=[pltpu.VMEM((SUBLANES, D_FF), F32)],
        compiler_params=pltpu.CompilerParams(dimension_semantics=("arbitrary",),
                                             vmem_limit_bytes=VMEM_LIMIT),
        name="out_ffn",
    )(x2d, lru, attn, mod, w_out, g2, w_up, conv_w, conv_b, w_down, g_final)


def _block_diag(w):
    nb, bs, _ = w.shape
    eye = jnp.eye(nb, dtype=w.dtype)
    return (w[:, :, None, :] * eye[:, None, :, None]).reshape(nb * bs, nb * bs)


def kernel(x, c, w_ada, b_ada, g_norm1, w_in, conv_lru_w, conv_lru_b, lru_wa, lru_ba, lru_wx, lru_bx, lru_lambda,
           lam_q1, lam_k1, lam_q2, lam_k2, g_subln, w_out, g_norm2, w_up, conv_ffn_w, conv_ffn_b, w_down, rel_bias,
           g_final):
    b, s, d = x.shape
    assert b == 1 and d == D_MODEL and w_ada.shape[0] == 1
    x2d = x.reshape(s, d)
    row = lambda a: a.reshape(1, -1)

    mod = _adaln_mod(c.reshape(d, 1), w_ada[0], row(b_ada[0]))

    q, k, v, xr, yg = _in_proj(x2d, mod, row(g_norm1[0]), w_in[0].astype(BF16))

    tq, tk = min(ATT_TQ, s), min(ATT_TK, s)
    bias = _bias_tiles(rel_bias.T.reshape(-1), tq, tk)
    lamv = jnp.stack([lam_q1[0], lam_k1[0], lam_q2[0], lam_k2[0]])
    attn = _diff_attn(q, k, v, bias, lamv, row(g_subln[0]))

    w_gates = jnp.concatenate([_block_diag(lru_wa[0]), _block_diag(lru_wx[0])], axis=1).astype(BF16)
    b_gates = jnp.concatenate([lru_ba[0], lru_bx[0]]).reshape(1, -1)
    lru = _rg_lru(xr, yg, conv_lru_w[0], row(conv_lru_b[0]), w_gates, b_gates, row(lru_lambda[0]))

    out = _out_ffn(x2d, lru, attn, mod, w_out[0].astype(BF16), row(g_norm2[0]), w_up[0].astype(BF16),
                   conv_ffn_w[0], row(conv_ffn_b[0]), w_down[0].astype(BF16), row(g_final))
    return out.reshape(b, s, d)
```

```python
import functools
import math

import jax
import jax.numpy as jnp
from jax import lax
from jax.experimental import pallas as pl
from jax.experimental.pallas import tpu as pltpu

F32 = jnp.float32
BF16 = jnp.bfloat16

D_MODEL = 1024
LRU_WIDTH = 512
LRU_BLOCKS = 8
LRU_BLOCK = LRU_WIDTH // LRU_BLOCKS
CONV_LRU = 4
LRU_C = 8.0
N_HEADS = 4
HEAD_DIM = 64
V_DIM = 2 * HEAD_DIM
QK_WIDTH = N_HEADS * 2 * HEAD_DIM
ATTN_WIDTH = N_HEADS * V_DIM
D_IN = 2 * QK_WIDTH + ATTN_WIDTH + 2 * LRU_WIDTH
D_FF = 3 * D_MODEL
CONV_FFN = 3
NUM_BUCKETS = 32
MAX_EXACT = NUM_BUCKETS // 2
MAX_DISTANCE = 128
EPS = 1e-6
NEG_INF = -1e30
LAMBDA_INIT = 0.8 - 0.6 * math.exp(-0.3 * 0)

LANES = 128
SUBLANES = 8
VMEM_LIMIT = 56 * 1024 * 1024

MOD_TN = 1536
PROJ_TM = 512
ATT_TQ = 512
ATT_TK = 512
LRU_TC = 256
FFN_TM = 256
FFN_CK = 512


def _rms_norm(x, g):
    y = x * lax.rsqrt(jnp.mean(x * x, axis=-1, keepdims=True) + EPS)
    return y * g


def _gelu_tanh(x):
    cdf = 0.5 * (1.0 + jnp.tanh(math.sqrt(2.0 / math.pi) * (x + 0.044715 * (x * x * x))))
    return x * cdf


def _shift_rows(prev8, x, s):
    ext = jnp.concatenate([prev8, x], axis=0)
    return pltpu.roll(ext, s, 0)[SUBLANES:]


def _adaln_kernel(c_ref, w_ref, b_ref, o_ref):
    c = c_ref[...]
    cond = c * jax.nn.sigmoid(c)
    o_ref[...] = jnp.sum(cond * w_ref[...], axis=0, keepdims=True) + b_ref[...]


def _adaln_mod(c_col, w_ada, b_ada):
    d, n = w_ada.shape
    return pl.pallas_call(
        _adaln_kernel,
        grid=(n // MOD_TN,),
        in_specs=[pl.BlockSpec((d, 1), lambda j: (0, 0)),
                  pl.BlockSpec((d, MOD_TN), lambda j: (0, j)),
                  pl.BlockSpec((1, MOD_TN), lambda j: (0, j))],
        out_specs=pl.BlockSpec((1, MOD_TN), lambda j: (0, j)),
        out_shape=jax.ShapeDtypeStruct((1, n), F32),
        compiler_params=pltpu.CompilerParams(vmem_limit_bytes=VMEM_LIMIT),
        name="adaln_mod",
    )(c_col, w_ada, b_ada)


def _in_proj_kernel(x_ref, mod_ref, g_ref, w_ref, q_ref, k_ref, v_ref, xr_ref, yg_ref):
    x = x_ref[...]
    shift1 = mod_ref[:, 0:D_MODEL]
    scale1 = mod_ref[:, D_MODEL:2 * D_MODEL]
    h = _rms_norm(x, g_ref[...]) * (1.0 + scale1) + shift1
    proj = jnp.dot(h.astype(BF16), w_ref[...], preferred_element_type=F32)
    lane = lax.broadcasted_iota(jnp.int32, (x.shape[0], V_DIM), 1)
    ones_col = jnp.where(lane == 0, 1.0, 0.0).astype(BF16)
    for hd in range(N_HEADS):
        lo = hd * V_DIM
        q_ref[hd] = (proj[:, lo:lo + V_DIM] * (HEAD_DIM ** -0.5)).astype(BF16)
        k_ref[hd] = proj[:, QK_WIDTH + lo:QK_WIDTH + lo + V_DIM].astype(BF16)
        v_ref[hd] = jnp.concatenate(
            [proj[:, 2 * QK_WIDTH + lo:2 * QK_WIDTH + lo + V_DIM].astype(BF16), ones_col], axis=1)
    base = 2 * QK_WIDTH + ATTN_WIDTH
    xr_ref[...] = proj[:, base:base + LRU_WIDTH]
    yg_ref[...] = proj[:, base + LRU_WIDTH:base + 2 * LRU_WIDTH]


def _in_proj(x2d, mod, g1, w_in_bf16):
    s = x2d.shape[0]
    tm = min(PROJ_TM, s)
    head_spec = pl.BlockSpec((N_HEADS, tm, V_DIM), lambda i: (0, i, 0))
    row_spec = pl.BlockSpec((tm, LRU_WIDTH), lambda i: (i, 0))
    return pl.pallas_call(
        _in_proj_kernel,
        grid=(s // tm,),
        in_specs=[pl.BlockSpec((tm, D_MODEL), lambda i: (i, 0)),
                  pl.BlockSpec((1, 6 * D_MODEL), lambda i: (0, 0)),
                  pl.BlockSpec((1, D_MODEL), lambda i: (0, 0)),
                  pl.BlockSpec((D_MODEL, D_IN), lambda i: (0, 0))],
        out_specs=[head_spec, head_spec, pl.BlockSpec((N_HEADS, tm, 2 * V_DIM), lambda i: (0, i, 0)),
                   row_spec, row_spec],
        out_shape=[jax.ShapeDtypeStruct((N_HEADS, s, V_DIM), BF16)] * 2
        + [jax.ShapeDtypeStruct((N_HEADS, s, 2 * V_DIM), BF16)]
        + [jax.ShapeDtypeStruct((s, LRU_WIDTH), F32)] * 2,
        compiler_params=pltpu.CompilerParams(dimension_semantics=("parallel",),
                                             vmem_limit_bytes=VMEM_LIMIT),
        name="in_proj",
    )(x2d, mod, g1, w_in_bf16)


def _bias_tiles_kernel(table_ref, o_ref, *, tq, tk, nd):
    hd = pl.program_id(0)
    blk = MAX_DISTANCE
    row = lax.broadcasted_iota(jnp.int32, (blk, blk), 0)
    col = lax.broadcasted_iota(jnp.int32, (blk, blk), 1)
    far = table_ref[hd * NUM_BUCKETS + NUM_BUCKETS - 1]

    def band_block(offset):
        rel = offset * blk + row - col
        n = jnp.maximum(rel, 0)
        nf = jnp.maximum(n, 1).astype(F32)
        large = MAX_EXACT + (jnp.log(nf / MAX_EXACT) / math.log(MAX_DISTANCE / MAX_EXACT)
                             * (NUM_BUCKETS - MAX_EXACT)).astype(jnp.int32)
        large = jnp.minimum(large, NUM_BUCKETS - 1)
        bucket = jnp.where(n < MAX_EXACT, n, large)
        val = jnp.zeros((blk, blk), F32)
        for b in range(NUM_BUCKETS):
            val = jnp.where(bucket == b, table_ref[hd * NUM_BUCKETS + b] - far, val)
        return jnp.where(rel >= 0, val, NEG_INF)

    blocks = {0: band_block(0), 1: band_block(1)}
    zeros = jnp.zeros((blk, blk), F32)
    neg = jnp.full((blk, blk), NEG_INF, F32)
    for dd in range(nd):
        for a in range(tq // blk):
            for b in range(tk // blk):
                off = dd * (tq // blk) + a - b
                o_ref[0, dd, a * blk:(a + 1) * blk, b * blk:(b + 1) * blk] = (
                    neg if off < 0 else blocks.get(off, zeros))
    o_ref[0, nd] = jnp.zeros((tq, tk), F32)
    o_ref[0, nd + 1] = jnp.full((tq, tk), NEG_INF, F32)


def _bias_tiles(table_flat, tq, tk):
    nd = tk // tq + 1
    return pl.pallas_call(
        functools.partial(_bias_tiles_kernel, tq=tq, tk=tk, nd=nd),
        grid=(N_HEADS,),
        in_specs=[pl.BlockSpec(memory_space=pltpu.SMEM)],
        out_specs=pl.BlockSpec((1, nd + 2, tq, tk), lambda h: (h, 0, 0, 0)),
        out_shape=jax.ShapeDtypeStruct((N_HEADS, nd + 2, tq, tk), F32),
        name="bias_tiles",
    )(table_flat)


def _attn_kernel(q_ref, k_ref, v_ref, bias_ref, lamv_ref, gs_ref, o_ref, sa_sc, sb_sc, m_sc, acc_sc,
                 *, tq, tk, nd):
    i = pl.program_id(1)
    q = q_ref[0]
    lane = lax.broadcasted_iota(jnp.int32, q.shape, 1)
    zero = jnp.zeros_like(q)
    qs = jnp.concatenate([jnp.where(lane < HEAD_DIM, q, zero),
                          jnp.where(lane >= HEAD_DIM, q, zero)], axis=0)

    q_start = i * tq
    n_far = jnp.maximum(q_start - (MAX_DISTANCE - 1), 0) // tk
    last = (q_start + tq - 1) // tk
    n_pairs = (last + 2) // 2

    def scores(j):
        jc = jnp.minimum(j, last)
        kb = k_ref[0, pl.ds(pl.multiple_of(jc * tk, tk), tk), :]
        s = lax.dot_general(qs, kb, (((1,), (1,)), ((), ())), preferred_element_type=F32)
        dd = (q_start - jc * tk) // tq
        idx = jnp.where(j > last, nd + 1, jnp.where(j < n_far, nd, dd))
        return (s.reshape(2, tq, tk) + bias_ref[0, idx][None]).reshape(2 * tq, tk)

    def accumulate(s_ref, j):
        jc = jnp.minimum(j, last)
        vb = v_ref[0, pl.ds(pl.multiple_of(jc * tk, tk), tk), :]
        m_prev = m_sc[...]
        m_new = jnp.maximum(m_prev, jnp.max(s_ref[...], axis=1, keepdims=True))
        alpha = jnp.exp(m_prev - m_new)
        p = jnp.exp(s_ref[...] - jnp.concatenate([m_new] * (tk // LANES), axis=1))
        acc_sc[...] = (jnp.concatenate([alpha, alpha], axis=1) * acc_sc[...]
                       + jnp.dot(p.astype(BF16), vb, preferred_element_type=F32))
        m_sc[...] = m_new

    m_sc[...] = jnp.full(m_sc.shape, NEG_INF, F32)
    acc_sc[...] = jnp.zeros(acc_sc.shape, F32)
    sa_sc[...] = scores(0)

    def pair_body(t, carry):
        j = 2 * t
        sb_sc[...] = scores(j + 1)
        accumulate(sa_sc, j)
        sa_sc[...] = scores(j + 2)
        accumulate(sb_sc, j + 1)
        return carry

    lax.fori_loop(0, n_pairs, pair_body, 0)

    lv = lamv_ref[...]
    d1 = jnp.sum(lv[0:1] * lv[1:2], axis=1, keepdims=True)
    d2 = jnp.sum(lv[2:3] * lv[3:4], axis=1, keepdims=True)
    lam = jnp.exp(d1) - jnp.exp(d2) + LAMBDA_INIT

    acc = acc_sc[...]
    out = acc[:, :V_DIM] / acc[:, V_DIM:V_DIM + 1]
    diff = out[:tq] - lam * out[tq:]
    o_ref[...] = (_rms_norm(diff, gs_ref[...]) * (1.0 - LAMBDA_INIT)).astype(o_ref.dtype)


def _diff_attn(q, k, v_aug, bias, lamv, g_subln):
    s = q.shape[1]
    tq = min(ATT_TQ, s)
    tk = min(ATT_TK, s)
    nd = bias.shape[1] - 2
    return pl.pallas_call(
        functools.partial(_attn_kernel, tq=tq, tk=tk, nd=nd),
        grid=(N_HEADS, s // tq),
        in_specs=[pl.BlockSpec((1, tq, V_DIM), lambda h, i: (h, i, 0)),
                  pl.BlockSpec((1, s, V_DIM), lambda h, i: (h, 0, 0)),
                  pl.BlockSpec((1, s, 2 * V_DIM), lambda h, i: (h, 0, 0)),
                  pl.BlockSpec((1, nd + 2, tq, tk), lambda h, i: (h, 0, 0, 0)),
                  pl.BlockSpec((4, HEAD_DIM), lambda h, i: (0, 0)),
                  pl.BlockSpec((1, V_DIM), lambda h, i: (0, 0))],
        out_specs=pl.BlockSpec((tq, V_DIM), lambda h, i: (i, h)),
        out_shape=jax.ShapeDtypeStruct((s, ATTN_WIDTH), BF16),
        scratch_shapes=[pltpu.VMEM((2 * tq, tk), F32),
                        pltpu.VMEM((2 * tq, tk), F32),
                        pltpu.VMEM((2 * tq, LANES), F32),
                        pltpu.VMEM((2 * tq, 2 * V_DIM), F32)],
        compiler_params=pltpu.CompilerParams(dimension_semantics=("parallel", "parallel"),
                                             vmem_limit_bytes=VMEM_LIMIT),
        name="diff_attn",
    )(q, k, v_aug, bias, lamv, g_subln)


def _lru_kernel(xr_ref, yg_ref, cw_ref, cb_ref, wg_ref, bg_ref, lam_ref, o_ref, tail_sc, h_sc, *, tc):
    @pl.when(pl.program_id(0) == 0)
    def _():
        tail_sc[...] = jnp.zeros(tail_sc.shape, F32)
        h_sc[...] = jnp.zeros(h_sc.shape, F32)

    x = xr_ref[...]
    prev8 = tail_sc[...]
    cw = cw_ref[...]
    xc = cb_ref[...]
    for j in range(CONV_LRU):
        sh = CONV_LRU - 1 - j
        xs = x if sh == 0 else _shift_rows(prev8, x, sh)
        xc = xc + xs * cw[j:j + 1]
    tail_sc[...] = x[tc - SUBLANES:]

    gates = jnp.dot(xc.astype(BF16), wg_ref[...], preferred_element_type=F32) + bg_ref[...]
    r = jax.nn.sigmoid(gates[:, :LRU_WIDTH])
    ig = jax.nn.sigmoid(gates[:, LRU_WIDTH:])
    z = -lam_ref[...]
    softplus = jnp.maximum(z, 0.0) + jnp.log1p(jnp.exp(-jnp.abs(z)))
    log_a = -LRU_C * r * softplus
    a = jnp.exp(log_a)
    th = jnp.tanh(log_a)
    b = jnp.sqrt(-2.0 * th / (1.0 - th)) * (ig * xc)

    row = lax.broadcasted_iota(jnp.int32, a.shape, 0)
    sh = 1
    while sh < tc:
        valid = row >= sh
        a_s = jnp.where(valid, pltpu.roll(a, sh, 0), 1.0)
        b_s = jnp.where(valid, pltpu.roll(b, sh, 0), 0.0)
        b = a * b_s + b
        a = a * a_s
        sh *= 2
    h = a * h_sc[0:1] + b
    h_sc[0:1] = h[tc - 1:tc]
    o_ref[...] = (h * _gelu_tanh(yg_ref[...])).astype(o_ref.dtype)


def _rg_lru(xr, yg, conv_w, conv_b, w_gates_bf16, b_gates, lam):
    s = xr.shape[0]
    tc = min(LRU_TC, s)
    row_spec = pl.BlockSpec((tc, LRU_WIDTH), lambda i: (i, 0))
    const = lambda shape: pl.BlockSpec(shape, lambda i: (0, 0))
    return pl.pallas_call(
        functools.partial(_lru_kernel, tc=tc),
        grid=(s // tc,),
        in_specs=[row_spec, row_spec,
                  const((CONV_LRU, LRU_WIDTH)), const((1, LRU_WIDTH)),
                  const((LRU_WIDTH, 2 * LRU_WIDTH)), const((1, 2 * LRU_WIDTH)),
                  const((1, LRU_WIDTH))],
        out_specs=row_spec,
        out_shape=jax.ShapeDtypeStruct((s, LRU_WIDTH), BF16),
        scratch_shapes=[pltpu.VMEM((SUBLANES, LRU_WIDTH), F32),
                        pltpu.VMEM((SUBLANES, LRU_WIDTH), F32)],
        compiler_params=pltpu.CompilerParams(dimension_semantics=("arbitrary",),
                                             vmem_limit_bytes=VMEM_LIMIT),
        name="rg_lru",
    )(xr, yg, conv_w, conv_b, w_gates_bf16, b_gates, lam)


def _ffn_kernel(x_ref, lru_ref, attn_ref, mod_ref, wo_ref, g2_ref, wup_ref, cw_ref, cb_ref, wdn_ref, gf_ref,
                o_ref, tail_sc, *, tm):
    @pl.when(pl.program_id(0) == 0)
    def _():
        tail_sc[...] = jnp.zeros(tail_sc.shape, F32)

    gate1 = mod_ref[:, 2 * D_MODEL:3 * D_MODEL]
    shift2 = mod_ref[:, 3 * D_MODEL:4 * D_MODEL]
    scale2 = mod_ref[:, 4 * D_MODEL:5 * D_MODEL]
    gate2 = mod_ref[:, 5 * D_MODEL:6 * D_MODEL]

    mix = (jnp.dot(lru_ref[...], wo_ref[0:LRU_WIDTH, :], preferred_element_type=F32)
           + jnp.dot(attn_ref[...], wo_ref[LRU_WIDTH:, :], preferred_element_type=F32))
    x1 = x_ref[...] + gate1 * mix
    h2 = (_rms_norm(x1, g2_ref[...]) * (1.0 + scale2) + shift2).astype(BF16)

    ff = jnp.zeros((tm, D_MODEL), F32)
    for c in range(D_FF // FFN_CK):
        lo = c * FFN_CK
        a = jnp.dot(h2, wup_ref[:, lo:lo + FFN_CK], preferred_element_type=F32)
        g = jnp.dot(h2, wup_ref[:, D_FF + lo:D_FF + lo + FFN_CK], preferred_element_type=F32)
        prev8 = tail_sc[:, lo:lo + FFN_CK]
        ac = cb_ref[:, lo:lo + FFN_CK]
        for j in range(CONV_FFN):
            sh = CONV_FFN - 1 - j
            a_s = a if sh == 0 else _shift_rows(prev8, a, sh)
            ac = ac + a_s * cw_ref[j:j + 1, lo:lo + FFN_CK]
        tail_sc[:, lo:lo + FFN_CK] = a[tm - SUBLANES:]
        u = (_gelu_tanh(ac) * g).astype(BF16)
        ff = ff + jnp.dot(u, wdn_ref[lo:lo + FFN_CK, :], preferred_element_type=F32)

    x2 = x1 + gate2 * ff
    o_ref[...] = _rms_norm(x2, gf_ref[...])


def _out_ffn(x2d, lru, attn, mod, w_out, g2, w_up, conv_w, conv_b, w_down, g_final):
    s = x2d.shape[0]
    tm = min(FFN_TM, s)
    row = lambda w: pl.BlockSpec((tm, w), lambda i: (i, 0))
    const = lambda shape: pl.BlockSpec(shape, lambda i: (0, 0), pipeline_mode=pl.Buffered(1))
    return pl.pallas_call(
        functools.partial(_ffn_kernel, tm=tm),
        grid=(s // tm,),
        in_specs=[row(D_MODEL), row(LRU_WIDTH), row(ATTN_WIDTH),
                  const((1, 6 * D_MODEL)),
                  const((D_MODEL, D_MODEL)), const((1, D_MODEL)),
                  const((D_MODEL, 2 * D_FF)),
                  const((CONV_FFN, D_FF)), const((1, D_FF)),
                  const((D_FF, D_MODEL)), const((1, D_MODEL))],
        out_specs=row(D_MODEL),
        out_shape=jax.ShapeDtypeStruct((s, D_MODEL), F32),
        scratch_shapes=[pltpu.VMEM((SUBLANES, D_FF), F32)],
        compiler_params=pltpu.CompilerParams(dimension_semantics=("arbitrary",),
                                             vmem_limit_bytes=VMEM_LIMIT),
        name="out_ffn",
    )(x2d, lru, attn, mod, w_out, g2, w_up, conv_w, conv_b, w_down, g_final)


def _block_diag(w):
    nb, bs, _ = w.shape
    eye = jnp.eye(nb, dtype=w.dtype)
    return (w[:, :, None, :] * eye[:, None, :, None]).reshape(nb * bs, nb * bs)


def kernel(x, c, w_ada, b_ada, g_norm1, w_in, conv_lru_w, conv_lru_b, lru_wa, lru_ba, lru_wx, lru_bx, lru_lambda,
           lam_q1, lam_k1, lam_q2, lam_k2, g_subln, w_out, g_norm2, w_up, conv_ffn_w, conv_ffn_b, w_down, rel_bias,
           g_final):
    b, s, d = x.shape
    assert b == 1 and d == D_MODEL and w_ada.shape[0] == 1
    x2d = x.reshape(s, d)
    row = lambda a: a.reshape(1, -1)

    mod = _adaln_mod(c.reshape(d, 1), w_ada[0], row(b_ada[0]))

    q, k, v, xr, yg = _in_proj(x2d, mod, row(g_norm1[0]), w_in[0].astype(BF16))

    tq, tk = min(ATT_TQ, s), min(ATT_TK, s)
    bias = _bias_tiles(rel_bias.T.reshape(-1), tq, tk)
    lamv = jnp.stack([lam_q1[0], lam_k1[0], lam_q2[0], lam_k2[0]])
    attn = _diff_attn(q, k, v, bias, lamv, row(g_subln[0]))

    w_gates = jnp.concatenate([_block_diag(lru_wa[0]), _block_diag(lru_wx[0])], axis=1).astype(BF16)
    b_gates = jnp.concatenate([lru_ba[0], lru_bx[0]]).reshape(1, -1)
    lru = _rg_lru(xr, yg, conv_lru_w[0], row(conv_lru_b[0]), w_gates, b_gates, row(lru_lambda[0]))

    out = _out_ffn(x2d, lru, attn, mod, w_out[0].astype(BF16), row(g_norm2[0]), w_up[0].astype(BF16),
                   conv_ffn_w[0], row(conv_ffn_b[0]), w_down[0].astype(BF16), row(g_final))
    return out.reshape(b, s, d)
```

```python
import functools
import math

import jax
import jax.numpy as jnp
from jax import lax
from jax.experimental import pallas as pl
from jax.experimental.pallas import tpu as pltpu

F32 = jnp.float32
BF16 = jnp.bfloat16

D_MODEL = 1024
LRU_WIDTH = 512
LRU_BLOCKS = 8
LRU_BLOCK = LRU_WIDTH // LRU_BLOCKS
CONV_LRU = 4
LRU_C = 8.0
N_HEADS = 4
HEAD_DIM = 64
V_DIM = 2 * HEAD_DIM
QK_WIDTH = N_HEADS * 2 * HEAD_DIM
ATTN_WIDTH = N_HEADS * V_DIM
D_IN = 2 * QK_WIDTH + ATTN_WIDTH + 2 * LRU_WIDTH
D_FF = 3 * D_MODEL
CONV_FFN = 3
NUM_BUCKETS = 32
MAX_EXACT = NUM_BUCKETS // 2
MAX_DISTANCE = 128
EPS = 1e-6
NEG_INF = -1e30
LAMBDA_INIT = 0.8 - 0.6 * math.exp(-0.3 * 0)

LANES = 128
SUBLANES = 8
VMEM_LIMIT = 56 * 1024 * 1024

MOD_TN = 1536
PROJ_TM = 512
ATT_TQ = 512
ATT_TK = 512
LRU_TC = 256
FFN_TM = 512
FFN_CK = 1536


def _rms_norm(x, g):
    y = x * lax.rsqrt(jnp.mean(x * x, axis=-1, keepdims=True) + EPS)
    return y * g


def _gelu_tanh(x):
    cdf = 0.5 * (1.0 + jnp.tanh(math.sqrt(2.0 / math.pi) * (x + 0.044715 * (x * x * x))))
    return x * cdf


def _shift_rows(prev8, x, s):
    ext = jnp.concatenate([prev8, x], axis=0)
    return pltpu.roll(ext, s, 0)[SUBLANES:]


def _adaln_kernel(c_ref, w_ref, b_ref, o_ref):
    c = c_ref[...]
    cond = c * jax.nn.sigmoid(c)
    o_ref[...] = jnp.sum(cond * w_ref[...], axis=0, keepdims=True) + b_ref[...]


def _adaln_mod(c_col, w_ada, b_ada):
    d, n = w_ada.shape
    return pl.pallas_call(
        _adaln_kernel,
        grid=(n // MOD_TN,),
        in_specs=[pl.BlockSpec((d, 1), lambda j: (0, 0)),
                  pl.BlockSpec((d, MOD_TN), lambda j: (0, j)),
                  pl.BlockSpec((1, MOD_TN), lambda j: (0, j))],
        out_specs=pl.BlockSpec((1, MOD_TN), lambda j: (0, j)),
        out_shape=jax.ShapeDtypeStruct((1, n), F32),
        compiler_params=pltpu.CompilerParams(vmem_limit_bytes=VMEM_LIMIT),
        name="adaln_mod",
    )(c_col, w_ada, b_ada)


def _in_proj_kernel(x_ref, mod_ref, g_ref, w_ref, q_ref, k_ref, v_ref, xr_ref, yg_ref):
    x = x_ref[...]
    shift1 = mod_ref[:, 0:D_MODEL]
    scale1 = mod_ref[:, D_MODEL:2 * D_MODEL]
    h = _rms_norm(x, g_ref[...]) * (1.0 + scale1) + shift1
    proj = jnp.dot(h.astype(BF16), w_ref[...], preferred_element_type=F32)
    lane = lax.broadcasted_iota(jnp.int32, (x.shape[0], V_DIM), 1)
    ones_col = jnp.where(lane == 0, 1.0, 0.0).astype(BF16)
    for hd in range(N_HEADS):
        lo = hd * V_DIM
        q_ref[hd] = (proj[:, lo:lo + V_DIM] * (HEAD_DIM ** -0.5)).astype(BF16)
        k_ref[hd] = proj[:, QK_WIDTH + lo:QK_WIDTH + lo + V_DIM].astype(BF16)
        v_ref[hd] = jnp.concatenate(
            [proj[:, 2 * QK_WIDTH + lo:2 * QK_WIDTH + lo + V_DIM].astype(BF16), ones_col], axis=1)
    base = 2 * QK_WIDTH + ATTN_WIDTH
    xr_ref[...] = proj[:, base:base + LRU_WIDTH]
    yg_ref[...] = proj[:, base + LRU_WIDTH:base + 2 * LRU_WIDTH]


def _in_proj(x2d, mod, g1, w_in_bf16):
    s = x2d.shape[0]
    tm = min(PROJ_TM, s)
    head_spec = pl.BlockSpec((N_HEADS, tm, V_DIM), lambda i: (0, i, 0))
    row_spec = pl.BlockSpec((tm, LRU_WIDTH), lambda i: (i, 0))
    return pl.pallas_call(
        _in_proj_kernel,
        grid=(s // tm,),
        in_specs=[pl.BlockSpec((tm, D_MODEL), lambda i: (i, 0)),
                  pl.BlockSpec((1, 6 * D_MODEL), lambda i: (0, 0)),
                  pl.BlockSpec((1, D_MODEL), lambda i: (0, 0)),
                  pl.BlockSpec((D_MODEL, D_IN), lambda i: (0, 0))],
        out_specs=[head_spec, head_spec, pl.BlockSpec((N_HEADS, tm, 2 * V_DIM), lambda i: (0, i, 0)),
                   row_spec, row_spec],
        out_shape=[jax.ShapeDtypeStruct((N_HEADS, s, V_DIM), BF16)] * 2
        + [jax.ShapeDtypeStruct((N_HEADS, s, 2 * V_DIM), BF16)]
        + [jax.ShapeDtypeStruct((s, LRU_WIDTH), F32)] * 2,
        compiler_params=pltpu.CompilerParams(dimension_semantics=("parallel",),
                                             vmem_limit_bytes=VMEM_LIMIT),
        name="in_proj",
    )(x2d, mod, g1, w_in_bf16)


def _bias_tiles_kernel(table_ref, o_ref, *, tq, tk, nd):
    hd = pl.program_id(0)
    blk = MAX_DISTANCE
    row = lax.broadcasted_iota(jnp.int32, (blk, blk), 0)
    col = lax.broadcasted_iota(jnp.int32, (blk, blk), 1)
    far = table_ref[hd * NUM_BUCKETS + NUM_BUCKETS - 1]

    def band_block(offset):
        rel = offset * blk + row - col
        n = jnp.maximum(rel, 0)
        nf = jnp.maximum(n, 1).astype(F32)
        large = MAX_EXACT + (jnp.log(nf / MAX_EXACT) / math.log(MAX_DISTANCE / MAX_EXACT)
                             * (NUM_BUCKETS - MAX_EXACT)).astype(jnp.int32)
        large = jnp.minimum(large, NUM_BUCKETS - 1)
        bucket = jnp.where(n < MAX_EXACT, n, large)
        val = jnp.zeros((blk, blk), F32)
        for b in range(NUM_BUCKETS):
            val = jnp.where(bucket == b, table_ref[hd * NUM_BUCKETS + b] - far, val)
        return jnp.where(rel >= 0, val, NEG_INF)

    blocks = {0: band_block(0), 1: band_block(1)}
    zeros = jnp.zeros((blk, blk), F32)
    neg = jnp.full((blk, blk), NEG_INF, F32)
    for dd in range(nd):
        for a in range(tq // blk):
            for b in range(tk // blk):
                off = ((dd + 1) * tk - tq) // blk + a - b
                o_ref[0, dd, a * blk:(a + 1) * blk, b * blk:(b + 1) * blk] = (
                    neg if off < 0 else blocks.get(off, zeros))
    o_ref[0, nd] = jnp.zeros((tq, tk), F32)
    o_ref[0, nd + 1] = jnp.full((tq, tk), NEG_INF, F32)


def _bias_tiles(table_flat, tq, tk):
    assert tq % tk == 0 and tk % MAX_DISTANCE == 0
    nd = tq // tk + 1
    return pl.pallas_call(
        functools.partial(_bias_tiles_kernel, tq=tq, tk=tk, nd=nd),
        grid=(N_HEADS,),
        in_specs=[pl.BlockSpec(memory_space=pltpu.SMEM)],
        out_specs=pl.BlockSpec((1, nd + 2, tq, tk), lambda h: (h, 0, 0, 0)),
        out_shape=jax.ShapeDtypeStruct((N_HEADS, nd + 2, tq, tk), F32),
        name="bias_tiles",
    )(table_flat)


def _attn_kernel(q_ref, k_ref, v_ref, bias_ref, lamv_ref, gs_ref, o_ref, sa_sc, sb_sc, m_sc, acc_sc,
                 *, tq, tk, nd):
    i = pl.program_id(1)
    q = q_ref[0]
    lane = lax.broadcasted_iota(jnp.int32, q.shape, 1)
    zero = jnp.zeros_like(q)
    qs = jnp.concatenate([jnp.where(lane < HEAD_DIM, q, zero),
                          jnp.where(lane >= HEAD_DIM, q, zero)], axis=0)

    q_start = i * tq
    n_far = jnp.maximum(q_start - (MAX_DISTANCE - 1), 0) // tk
    last = (q_start + tq - 1) // tk
    n_pairs = (last + 2) // 2

    def scores(j):
        jc = jnp.minimum(j, last)
        kb = k_ref[0, pl.ds(pl.multiple_of(jc * tk, tk), tk), :]
        s = lax.dot_general(qs, kb, (((1,), (1,)), ((), ())), preferred_element_type=F32)
        dd = (q_start + tq - (jc + 1) * tk) // tk
        idx = jnp.where(j > last, nd + 1, jnp.where(j < n_far, nd, dd))
        return (s.reshape(2, tq, tk) + bias_ref[0, idx][None]).reshape(2 * tq, tk)

    def accumulate(s_ref, j):
        jc = jnp.minimum(j, last)
        vb = v_ref[0, pl.ds(pl.multiple_of(jc * tk, tk), tk), :]
        m_prev = m_sc[...]
        m_new = jnp.maximum(m_prev, jnp.max(s_ref[...], axis=1, keepdims=True))
        alpha = jnp.exp(m_prev - m_new)
        p = jnp.exp(s_ref[...] - jnp.concatenate([m_new] * (tk // LANES), axis=1))
        acc_sc[...] = (jnp.concatenate([alpha, alpha], axis=1) * acc_sc[...]
                       + jnp.dot(p.astype(BF16), vb, preferred_element_type=F32))
        m_sc[...] = m_new

    m_sc[...] = jnp.full(m_sc.shape, NEG_INF, F32)
    acc_sc[...] = jnp.zeros(acc_sc.shape, F32)
    sa_sc[...] = scores(0)

    def pair_body(t, carry):
        j = 2 * t
        sb_sc[...] = scores(j + 1)
        accumulate(sa_sc, j)
        sa_sc[...] = scores(j + 2)
        accumulate(sb_sc, j + 1)
        return carry

    lax.fori_loop(0, n_pairs, pair_body, 0)

    lv = lamv_ref[...]
    d1 = jnp.sum(lv[0:1] * lv[1:2], axis=1, keepdims=True)
    d2 = jnp.sum(lv[2:3] * lv[3:4], axis=1, keepdims=True)
    lam = jnp.exp(d1) - jnp.exp(d2) + LAMBDA_INIT

    acc = acc_sc[...]
    out = acc[:, :V_DIM] / acc[:, V_DIM:V_DIM + 1]
    diff = out[:tq] - lam * out[tq:]
    o_ref[...] = (_rms_norm(diff, gs_ref[...]) * (1.0 - LAMBDA_INIT)).astype(o_ref.dtype)


def _diff_attn(q, k, v_aug, bias, lamv, g_subln):
    s = q.shape[1]
    tq = min(ATT_TQ, s)
    tk = min(ATT_TK, s)
    nd = bias.shape[1] - 2
    return pl.pallas_call(
        functools.partial(_attn_kernel, tq=tq, tk=tk, nd=nd),
        grid=(N_HEADS, s // tq),
        in_specs=[pl.BlockSpec((1, tq, V_DIM), lambda h, i: (h, i, 0)),
                  pl.BlockSpec((1, s, V_DIM), lambda h, i: (h, 0, 0)),
                  pl.BlockSpec((1, s, 2 * V_DIM), lambda h, i: (h, 0, 0)),
                  pl.BlockSpec((1, nd + 2, tq, tk), lambda h, i: (h, 0, 0, 0), pipeline_mode=pl.Buffered(1)),
                  pl.BlockSpec((4, HEAD_DIM), lambda h, i: (0, 0)),
                  pl.BlockSpec((1, V_DIM), lambda h, i: (0, 0))],
        out_specs=pl.BlockSpec((tq, V_DIM), lambda h, i: (i, h)),
        out_shape=jax.ShapeDtypeStruct((s, ATTN_WIDTH), BF16),
        scratch_shapes=[pltpu.VMEM((2 * tq, tk), F32),
                        pltpu.VMEM((2 * tq, tk), F32),
                        pltpu.VMEM((2 * tq, LANES), F32),
                        pltpu.VMEM((2 * tq, 2 * V_DIM), F32)],
        compiler_params=pltpu.CompilerParams(dimension_semantics=("parallel", "parallel"),
                                             vmem_limit_bytes=VMEM_LIMIT),
        name="diff_attn",
    )(q, k, v_aug, bias, lamv, g_subln)


def _lru_kernel(xr_ref, yg_ref, cw_ref, cb_ref, wg_ref, bg_ref, lam_ref, o_ref, tail_sc, h_sc, *, tc):
    @pl.when(pl.program_id(0) == 0)
    def _():
        tail_sc[...] = jnp.zeros(tail_sc.shape, F32)
        h_sc[...] = jnp.zeros(h_sc.shape, F32)

    x = xr_ref[...]
    prev8 = tail_sc[...]
    cw = cw_ref[...]
    xc = cb_ref[...]
    for j in range(CONV_LRU):
        sh = CONV_LRU - 1 - j
        xs = x if sh == 0 else _shift_rows(prev8, x, sh)
        xc = xc + xs * cw[j:j + 1]
    tail_sc[...] = x[tc - SUBLANES:]

    gates = jnp.dot(xc.astype(BF16), wg_ref[...], preferred_element_type=F32) + bg_ref[...]
    sig = 0.5 * jnp.tanh(0.5 * gates) + 0.5
    r = sig[:, :LRU_WIDTH]
    ig = sig[:, LRU_WIDTH:]
    z = -lam_ref[...]
    softplus = jnp.maximum(z, 0.0) + jnp.log1p(jnp.exp(-jnp.abs(z)))
    log_a = -LRU_C * r * softplus
    a = jnp.exp(log_a)
    th = jnp.tanh(log_a)
    b = jnp.sqrt(-2.0 * th / (1.0 - th)) * (ig * xc)

    ng = tc // SUBLANES
    a = a.reshape(ng, SUBLANES, LRU_WIDTH)
    b = b.reshape(ng, SUBLANES, LRU_WIDTH)
    row = lax.broadcasted_iota(jnp.int32, a.shape, 1)
    sh = 1
    while sh < SUBLANES:
        valid = row >= sh
        a_s = jnp.where(valid, pltpu.roll(a, sh, 1), 1.0)
        b_s = jnp.where(valid, pltpu.roll(b, sh, 1), 0.0)
        b = a * b_s + b
        a = a * a_s
        sh *= 2
    carry = h_sc[0:1]
    hs = []
    for g in range(ng):
        hg = a[g] * carry + b[g]
        hs.append(hg)
        carry = hg[SUBLANES - 1:SUBLANES]
    h = jnp.concatenate(hs, axis=0)
    h_sc[0:1] = carry
    o_ref[...] = (h * _gelu_tanh(yg_ref[...])).astype(o_ref.dtype)


def _rg_lru(xr, yg, conv_w, conv_b, w_gates_bf16, b_gates, lam):
    s = xr.shape[0]
    tc = min(LRU_TC, s)
    row_spec = pl.BlockSpec((tc, LRU_WIDTH), lambda i: (i, 0))
    const = lambda shape: pl.BlockSpec(shape, lambda i: (0, 0))
    return pl.pallas_call(
        functools.partial(_lru_kernel, tc=tc),
        grid=(s // tc,),
        in_specs=[row_spec, row_spec,
                  const((CONV_LRU, LRU_WIDTH)), const((1, LRU_WIDTH)),
                  const((LRU_WIDTH, 2 * LRU_WIDTH)), const((1, 2 * LRU_WIDTH)),
                  const((1, LRU_WIDTH))],
        out_specs=row_spec,
        out_shape=jax.ShapeDtypeStruct((s, LRU_WIDTH), BF16),
        scratch_shapes=[pltpu.VMEM((SUBLANES, LRU_WIDTH), F32),
                        pltpu.VMEM((SUBLANES, LRU_WIDTH), F32)],
        compiler_params=pltpu.CompilerParams(dimension_semantics=("arbitrary",),
                                             vmem_limit_bytes=VMEM_LIMIT),
        name="rg_lru",
    )(xr, yg, conv_w, conv_b, w_gates_bf16, b_gates, lam)


def _ffn_kernel(x_ref, lru_ref, attn_ref, mod_ref, wo_ref, g2_ref, wup_ref, cw_ref, cb_ref, wdn_ref, gf_ref,
                o_ref, tail_sc, *, tm):
    @pl.when(pl.program_id(0) == 0)
    def _():
        tail_sc[...] = jnp.zeros(tail_sc.shape, F32)

    gate1 = mod_ref[:, 2 * D_MODEL:3 * D_MODEL]
    shift2 = mod_ref[:, 3 * D_MODEL:4 * D_MODEL]
    scale2 = mod_ref[:, 4 * D_MODEL:5 * D_MODEL]
    gate2 = mod_ref[:, 5 * D_MODEL:6 * D_MODEL]

    mix = (jnp.dot(lru_ref[...], wo_ref[0:LRU_WIDTH, :], preferred_element_type=F32)
           + jnp.dot(attn_ref[...], wo_ref[LRU_WIDTH:, :], preferred_element_type=F32))
    x1 = x_ref[...] + gate1 * mix
    h2 = (_rms_norm(x1, g2_ref[...]) * (1.0 + scale2) + shift2).astype(BF16)

    ff = jnp.zeros((tm, D_MODEL), F32)
    for c in range(D_FF // FFN_CK):
        lo = c * FFN_CK
        a = jnp.dot(h2, wup_ref[:, lo:lo + FFN_CK], preferred_element_type=F32)
        g = jnp.dot(h2, wup_ref[:, D_FF + lo:D_FF + lo + FFN_CK], preferred_element_type=F32)
        prev8 = tail_sc[:, lo:lo + FFN_CK]
        ac = cb_ref[:, lo:lo + FFN_CK]
        for j in range(CONV_FFN):
            sh = CONV_FFN - 1 - j
            a_s = a if sh == 0 else _shift_rows(prev8, a, sh)
            ac = ac + a_s * cw_ref[j:j + 1, lo:lo + FFN_CK]
        tail_sc[:, lo:lo + FFN_CK] = a[tm - SUBLANES:]
        u = (_gelu_tanh(ac) * g).astype(BF16)
        ff = ff + jnp.dot(u, wdn_ref[lo:lo + FFN_CK, :], preferred_element_type=F32)

    x2 = x1 + gate2 * ff
    o_ref[...] = _rms_norm(x2, gf_ref[...])


def _out_ffn(x2d, lru, attn, mod, w_out, g2, w_up, conv_w, conv_b, w_down, g_final):
    s = x2d.shape[0]
    tm = min(FFN_TM, s)
    row = lambda w: pl.BlockSpec((tm, w), lambda i: (i, 0))
    const = lambda shape: pl.BlockSpec(shape, lambda i: (0, 0), pipeline_mode=pl.Buffered(1))
    return pl.pallas_call(
        functools.partial(_ffn_kernel, tm=tm),
        grid=(s // tm,),
        in_specs=[row(D_MODEL), row(LRU_WIDTH), row(ATTN_WIDTH),
                  const((1, 6 * D_MODEL)),
                  const((D_MODEL, D_MODEL)), const((1, D_MODEL)),
                  const((D_MODEL, 2 * D_FF)),
                  const((CONV_FFN, D_FF)), const((1, D_FF)),
                  const((D_FF, D_MODEL)), const((1, D_MODEL))],
        out_specs=row(D_MODEL),
        out_shape=jax.ShapeDtypeStruct((s, D_MODEL), F32),
        scratch_shapes=[pltpu.VMEM((SUBLANES, D_FF), F32)],
        compiler_params=pltpu.CompilerParams(dimension_semantics=("arbitrary",),
                                             vmem_limit_bytes=VMEM_LIMIT),
        name="out_ffn",
    )(x2d, lru, attn, mod, w_out, g2, w_up, conv_w, conv_b, w_down, g_final)


def _block_diag(w):
    nb, bs, _ = w.shape
    eye = jnp.eye(nb, dtype=w.dtype)
    return (w[:, :, None, :] * eye[:, None, :, None]).reshape(nb * bs, nb * bs)


def kernel(x, c, w_ada, b_ada, g_norm1, w_in, conv_lru_w, conv_lru_b, lru_wa, lru_ba, lru_wx, lru_bx, lru_lambda,
           lam_q1, lam_k1, lam_q2, lam_k2, g_subln, w_out, g_norm2, w_up, conv_ffn_w, conv_ffn_b, w_down, rel_bias,
           g_final):
    b, s, d = x.shape
    assert b == 1 and d == D_MODEL and w_ada.shape[0] == 1
    x2d = x.reshape(s, d)
    row = lambda a: a.reshape(1, -1)

    mod = _adaln_mod(c.reshape(d, 1), w_ada[0], row(b_ada[0]))

    q, k, v, xr, yg = _in_proj(x2d, mod, row(g_norm1[0]), w_in[0].astype(BF16))

    tq, tk = min(ATT_TQ, s), min(ATT_TK, s)
    bias = _bias_tiles(rel_bias.T.reshape(-1), tq, tk)
    lamv = jnp.stack([lam_q1[0], lam_k1[0], lam_q2[0], lam_k2[0]])
    attn = _diff_attn(q, k, v, bias, lamv, row(g_subln[0]))

    w_gates = jnp.concatenate([_block_diag(lru_wa[0]), _block_diag(lru_wx[0])], axis=1).astype(BF16)
    b_gates = jnp.concatenate([lru_ba[0], lru_bx[0]]).reshape(1, -1)
    lru = _rg_lru(xr, yg, conv_lru_w[0], row(conv_lru_b[0]), w_gates, b_gates, row(lru_lambda[0]))

    out = _out_ffn(x2d, lru, attn, mod, w_out[0].astype(BF16), row(g_norm2[0]), w_up[0].astype(BF16),
                   conv_ffn_w[0], row(conv_ffn_b[0]), w_down[0].astype(BF16), row(g_final))
    return out.reshape(b, s, d)
```

```python
import functools
import math

import jax
import jax.numpy as jnp
from jax import lax
from jax.experimental import pallas as pl
from jax.experimental.pallas import tpu as pltpu

F32 = jnp.float32
BF16 = jnp.bfloat16

D_MODEL = 1024
LRU_WIDTH = 512
LRU_BLOCKS = 8
LRU_BLOCK = LRU_WIDTH // LRU_BLOCKS
CONV_LRU = 4
LRU_C = 8.0
N_HEADS = 4
HEAD_DIM = 64
V_DIM = 2 * HEAD_DIM
QK_WIDTH = N_HEADS * 2 * HEAD_DIM
ATTN_WIDTH = N_HEADS * V_DIM
D_IN = 2 * QK_WIDTH + ATTN_WIDTH + 2 * LRU_WIDTH
D_FF = 3 * D_MODEL
CONV_FFN = 3
NUM_BUCKETS = 32
MAX_EXACT = NUM_BUCKETS // 2
MAX_DISTANCE = 128
EPS = 1e-6
NEG_INF = -1e30
LAMBDA_INIT = 0.8 - 0.6 * math.exp(-0.3 * 0)
LOG2E = math.log2(math.e)

LANES = 128
SUBLANES = 8
VMEM_LIMIT = 56 * 1024 * 1024

MOD_TN = 1536
PROJ_TM = 512
ATT_TQ = 512
ATT_TK = 512
LRU_TC = 256
FFN_TM = 512
FFN_CK = 1536


def _rms_norm(x, g):
    y = x * lax.rsqrt(jnp.mean(x * x, axis=-1, keepdims=True) + EPS)
    return y * g


def _gelu_tanh(x):
    cdf = 0.5 * (1.0 + jnp.tanh(math.sqrt(2.0 / math.pi) * (x + 0.044715 * (x * x * x))))
    return x * cdf


def _shift_rows(prev8, x, s):
    ext = jnp.concatenate([prev8, x], axis=0)
    return pltpu.roll(ext, s, 0)[SUBLANES:]


def _adaln_kernel(c_ref, w_ref, b_ref, o_ref):
    c = c_ref[...]
    cond = c * jax.nn.sigmoid(c)
    o_ref[...] = jnp.sum(cond * w_ref[...], axis=0, keepdims=True) + b_ref[...]


def _adaln_mod(c_col, w_ada, b_ada):
    d, n = w_ada.shape
    return pl.pallas_call(
        _adaln_kernel,
        grid=(n // MOD_TN,),
        in_specs=[pl.BlockSpec((d, 1), lambda j: (0, 0)),
                  pl.BlockSpec((d, MOD_TN), lambda j: (0, j)),
                  pl.BlockSpec((1, MOD_TN), lambda j: (0, j))],
        out_specs=pl.BlockSpec((1, MOD_TN), lambda j: (0, j)),
        out_shape=jax.ShapeDtypeStruct((1, n), F32),
        compiler_params=pltpu.CompilerParams(vmem_limit_bytes=VMEM_LIMIT),
        name="adaln_mod",
    )(c_col, w_ada, b_ada)


def _in_proj_kernel(x_ref, mod_ref, g_ref, w_ref, q_ref, k_ref, v_ref, xr_ref, yg_ref):
    x = x_ref[...]
    shift1 = mod_ref[:, 0:D_MODEL]
    scale1 = mod_ref[:, D_MODEL:2 * D_MODEL]
    h = _rms_norm(x, g_ref[...]) * (1.0 + scale1) + shift1
    proj = jnp.dot(h.astype(BF16), w_ref[...], preferred_element_type=F32)
    lane = lax.broadcasted_iota(jnp.int32, (x.shape[0], V_DIM), 1)
    ones_col = jnp.where(lane == 0, 1.0, 0.0).astype(BF16)
    for hd in range(N_HEADS):
        lo = hd * V_DIM
        q_ref[hd] = (proj[:, lo:lo + V_DIM] * (HEAD_DIM ** -0.5 * LOG2E)).astype(BF16)
        k_ref[hd] = proj[:, QK_WIDTH + lo:QK_WIDTH + lo + V_DIM].astype(BF16)
        v_ref[hd] = jnp.concatenate(
            [proj[:, 2 * QK_WIDTH + lo:2 * QK_WIDTH + lo + V_DIM].astype(BF16), ones_col], axis=1)
    base = 2 * QK_WIDTH + ATTN_WIDTH
    xr_ref[...] = proj[:, base:base + LRU_WIDTH]
    yg_ref[...] = proj[:, base + LRU_WIDTH:base + 2 * LRU_WIDTH]


def _in_proj(x2d, mod, g1, w_in_bf16):
    s = x2d.shape[0]
    tm = min(PROJ_TM, s)
    head_spec = pl.BlockSpec((N_HEADS, tm, V_DIM), lambda i: (0, i, 0))
    row_spec = pl.BlockSpec((tm, LRU_WIDTH), lambda i: (i, 0))
    return pl.pallas_call(
        _in_proj_kernel,
        grid=(s // tm,),
        in_specs=[pl.BlockSpec((tm, D_MODEL), lambda i: (i, 0)),
                  pl.BlockSpec((1, 6 * D_MODEL), lambda i: (0, 0)),
                  pl.BlockSpec((1, D_MODEL), lambda i: (0, 0)),
                  pl.BlockSpec((D_MODEL, D_IN), lambda i: (0, 0))],
        out_specs=[head_spec, head_spec, pl.BlockSpec((N_HEADS, tm, 2 * V_DIM), lambda i: (0, i, 0)),
                   row_spec, row_spec],
        out_shape=[jax.ShapeDtypeStruct((N_HEADS, s, V_DIM), BF16)] * 2
        + [jax.ShapeDtypeStruct((N_HEADS, s, 2 * V_DIM), BF16)]
        + [jax.ShapeDtypeStruct((s, LRU_WIDTH), F32)] * 2,
        compiler_params=pltpu.CompilerParams(dimension_semantics=("parallel",),
                                             vmem_limit_bytes=VMEM_LIMIT),
        name="in_proj",
    )(x2d, mod, g1, w_in_bf16)


def _bias_tiles_kernel(table_ref, o_ref, *, tq, tk, nd):
    hd = pl.program_id(0)
    blk = MAX_DISTANCE
    row = lax.broadcasted_iota(jnp.int32, (blk, blk), 0)
    col = lax.broadcasted_iota(jnp.int32, (blk, blk), 1)
    far = table_ref[hd * NUM_BUCKETS + NUM_BUCKETS - 1]

    def band_block(offset):
        rel = offset * blk + row - col
        n = jnp.maximum(rel, 0)
        nf = jnp.maximum(n, 1).astype(F32)
        large = MAX_EXACT + (jnp.log(nf / MAX_EXACT) / math.log(MAX_DISTANCE / MAX_EXACT)
                             * (NUM_BUCKETS - MAX_EXACT)).astype(jnp.int32)
        large = jnp.minimum(large, NUM_BUCKETS - 1)
        bucket = jnp.where(n < MAX_EXACT, n, large)
        val = jnp.zeros((blk, blk), F32)
        for b in range(NUM_BUCKETS):
            val = jnp.where(bucket == b, (table_ref[hd * NUM_BUCKETS + b] - far) * LOG2E, val)
        return jnp.where(rel >= 0, val, NEG_INF)

    blocks = {0: band_block(0), 1: band_block(1)}
    zeros = jnp.zeros((blk, blk), F32)
    neg = jnp.full((blk, blk), NEG_INF, F32)
    for dd in range(nd):
        for a in range(tq // blk):
            for b in range(tk // blk):
                off = ((dd + 1) * tk - tq) // blk + a - b
                o_ref[0, dd, a * blk:(a + 1) * blk, b * blk:(b + 1) * blk] = (
                    neg if off < 0 else blocks.get(off, zeros))
    o_ref[0, nd] = jnp.zeros((tq, tk), F32)
    o_ref[0, nd + 1] = jnp.full((tq, tk), NEG_INF, F32)


def _bias_tiles(table_flat, tq, tk):
    assert tq % tk == 0 and tk % MAX_DISTANCE == 0
    nd = tq // tk + 1
    return pl.pallas_call(
        functools.partial(_bias_tiles_kernel, tq=tq, tk=tk, nd=nd),
        grid=(N_HEADS,),
        in_specs=[pl.BlockSpec(memory_space=pltpu.SMEM)],
        out_specs=pl.BlockSpec((1, nd + 2, tq, tk), lambda h: (h, 0, 0, 0)),
        out_shape=jax.ShapeDtypeStruct((N_HEADS, nd + 2, tq, tk), F32),
        name="bias_tiles",
    )(table_flat)


def _attn_kernel(q_ref, k_ref, v_ref, bias_ref, lamv_ref, gs_ref, o_ref, sa_sc, sb_sc, pa_sc, pb_sc, m_sc, acc_sc,
                 *, tq, tk, nd):
    i = pl.program_id(1)
    q = q_ref[0]
    lane = lax.broadcasted_iota(jnp.int32, q.shape, 1)
    zero = jnp.zeros_like(q)
    qs = jnp.concatenate([jnp.where(lane < HEAD_DIM, q, zero),
                          jnp.where(lane >= HEAD_DIM, q, zero)], axis=0)

    q_start = i * tq
    n_far = jnp.maximum(q_start - (MAX_DISTANCE - 1), 0) // tk
    last = (q_start + tq - 1) // tk
    n_pairs = (last + 2) // 2

    def scores(j, s_ref, part_ref):
        jc = jnp.minimum(j, last)
        kb = k_ref[0, pl.ds(pl.multiple_of(jc * tk, tk), tk), :]
        s = lax.dot_general(qs, kb, (((1,), (1,)), ((), ())), preferred_element_type=F32)
        dd = (q_start + tq - (jc + 1) * tk) // tk
        idx = jnp.where(j > last, nd + 1, jnp.where(j < n_far, nd, dd))
        s = (s.reshape(2, tq, tk) + bias_ref[0, idx][None]).reshape(2 * tq, tk)
        s_ref[...] = s
        part_ref[...] = functools.reduce(
            jnp.maximum, [s[:, c * LANES:(c + 1) * LANES] for c in range(tk // LANES)])

    def accumulate(s_ref, part_ref, j):
        jc = jnp.minimum(j, last)
        vb = v_ref[0, pl.ds(pl.multiple_of(jc * tk, tk), tk), :]
        m_prev = m_sc[...]
        m_new = jnp.maximum(m_prev, jnp.max(part_ref[...], axis=1, keepdims=True))
        alpha = jnp.exp2(m_prev - m_new)
        p = jnp.exp2(s_ref[...] - jnp.concatenate([m_new] * (tk // LANES), axis=1))
        acc_sc[...] = (jnp.concatenate([alpha, alpha], axis=1) * acc_sc[...]
                       + jnp.dot(p.astype(BF16), vb, preferred_element_type=F32))
        m_sc[...] = m_new

    m_sc[...] = jnp.full(m_sc.shape, NEG_INF, F32)
    acc_sc[...] = jnp.zeros(acc_sc.shape, F32)
    scores(0, sa_sc, pa_sc)

    def pair_body(t, carry):
        j = 2 * t
        scores(j + 1, sb_sc, pb_sc)
        accumulate(sa_sc, pa_sc, j)
        scores(j + 2, sa_sc, pa_sc)
        accumulate(sb_sc, pb_sc, j + 1)
        return carry

    lax.fori_loop(0, n_pairs, pair_body, 0)

    lv = lamv_ref[...]
    d1 = jnp.sum(lv[0:1] * lv[1:2], axis=1, keepdims=True)
    d2 = jnp.sum(lv[2:3] * lv[3:4], axis=1, keepdims=True)
    lam = jnp.exp(d1) - jnp.exp(d2) + LAMBDA_INIT

    acc = acc_sc[...]
    out = acc[:, :V_DIM] / acc[:, V_DIM:V_DIM + 1]
    diff = out[:tq] - lam * out[tq:]
    o_ref[...] = (_rms_norm(diff, gs_ref[...]) * (1.0 - LAMBDA_INIT)).astype(o_ref.dtype)


def _diff_attn(q, k, v_aug, bias, lamv, g_subln):
    s = q.shape[1]
    tq = min(ATT_TQ, s)
    tk = min(ATT_TK, s)
    nd = bias.shape[1] - 2
    return pl.pallas_call(
        functools.partial(_attn_kernel, tq=tq, tk=tk, nd=nd),
        grid=(N_HEADS, s // tq),
        in_specs=[pl.BlockSpec((1, tq, V_DIM), lambda h, i: (h, i, 0)),
                  pl.BlockSpec((1, s, V_DIM), lambda h, i: (h, 0, 0)),
                  pl.BlockSpec((1, s, 2 * V_DIM), lambda h, i: (h, 0, 0)),
                  pl.BlockSpec((1, nd + 2, tq, tk), lambda h, i: (h, 0, 0, 0), pipeline_mode=pl.Buffered(1)),
                  pl.BlockSpec((4, HEAD_DIM), lambda h, i: (0, 0)),
                  pl.BlockSpec((1, V_DIM), lambda h, i: (0, 0))],
        out_specs=pl.BlockSpec((tq, V_DIM), lambda h, i: (i, h)),
        out_shape=jax.ShapeDtypeStruct((s, ATTN_WIDTH), BF16),
        scratch_shapes=[pltpu.VMEM((2 * tq, tk), F32),
                        pltpu.VMEM((2 * tq, tk), F32),
                        pltpu.VMEM((2 * tq, LANES), F32),
                        pltpu.VMEM((2 * tq, LANES), F32),
                        pltpu.VMEM((2 * tq, LANES), F32),
                        pltpu.VMEM((2 * tq, 2 * V_DIM), F32)],
        compiler_params=pltpu.CompilerParams(dimension_semantics=("parallel", "parallel"),
                                             vmem_limit_bytes=VMEM_LIMIT),
        name="diff_attn",
    )(q, k, v_aug, bias, lamv, g_subln)


def _lru_kernel(xr_ref, yg_ref, cw_ref, cb_ref, wg_ref, bg_ref, lam_ref, o_ref, tail_sc, h_sc, *, tc):
    @pl.when(pl.program_id(0) == 0)
    def _():
        tail_sc[...] = jnp.zeros(tail_sc.shape, F32)
        h_sc[...] = jnp.zeros(h_sc.shape, F32)

    x = xr_ref[...]
    prev8 = tail_sc[...]
    cw = cw_ref[...]
    xc = cb_ref[...]
    for j in range(CONV_LRU):
        sh = CONV_LRU - 1 - j
        xs = x if sh == 0 else _shift_rows(prev8, x, sh)
        xc = xc + xs * cw[j:j + 1]
    tail_sc[...] = x[tc - SUBLANES:]

    gates = jnp.dot(xc.astype(BF16), wg_ref[...], preferred_element_type=F32) + bg_ref[...]
    sig = 0.5 * jnp.tanh(0.5 * gates) + 0.5
    r = sig[:, :LRU_WIDTH]
    ig = sig[:, LRU_WIDTH:]
    z = -lam_ref[...]
    softplus = jnp.maximum(z, 0.0) + jnp.log1p(jnp.exp(-jnp.abs(z)))
    log_a = -LRU_C * r * softplus
    a = jnp.exp(log_a)
    th = jnp.tanh(log_a)
    b = jnp.sqrt(-2.0 * th / (1.0 - th)) * (ig * xc)

    ng = tc // SUBLANES
    a = a.reshape(ng, SUBLANES, LRU_WIDTH)
    b = b.reshape(ng, SUBLANES, LRU_WIDTH)
    row = lax.broadcasted_iota(jnp.int32, a.shape, 1)
    sh = 1
    while sh < SUBLANES:
        valid = row >= sh
        a_s = jnp.where(valid, pltpu.roll(a, sh, 1), 1.0)
        b_s = jnp.where(valid, pltpu.roll(b, sh, 1), 0.0)
        b = a * b_s + b
        a = a * a_s
        sh *= 2
    carry = h_sc[0:1]
    hs = []
    for g in range(ng):
        hg = a[g] * carry + b[g]
        hs.append(hg)
        carry = hg[SUBLANES - 1:SUBLANES]
    h = jnp.concatenate(hs, axis=0)
    h_sc[0:1] = carry
    o_ref[...] = (h * _gelu_tanh(yg_ref[...])).astype(o_ref.dtype)


def _rg_lru(xr, yg, conv_w, conv_b, w_gates_bf16, b_gates, lam):
    s = xr.shape[0]
    tc = min(LRU_TC, s)
    row_spec = pl.BlockSpec((tc, LRU_WIDTH), lambda i: (i, 0))
    const = lambda shape: pl.BlockSpec(shape, lambda i: (0, 0))
    return pl.pallas_call(
        functools.partial(_lru_kernel, tc=tc),
        grid=(s // tc,),
        in_specs=[row_spec, row_spec,
                  const((CONV_LRU, LRU_WIDTH)), const((1, LRU_WIDTH)),
                  const((LRU_WIDTH, 2 * LRU_WIDTH)), const((1, 2 * LRU_WIDTH)),
                  const((1, LRU_WIDTH))],
        out_specs=row_spec,
        out_shape=jax.ShapeDtypeStruct((s, LRU_WIDTH), BF16),
        scratch_shapes=[pltpu.VMEM((SUBLANES, LRU_WIDTH), F32),
                        pltpu.VMEM((SUBLANES, LRU_WIDTH), F32)],
        compiler_params=pltpu.CompilerParams(dimension_semantics=("arbitrary",),
                                             vmem_limit_bytes=VMEM_LIMIT),
        name="rg_lru",
    )(xr, yg, conv_w, conv_b, w_gates_bf16, b_gates, lam)


def _ffn_kernel(x_ref, lru_ref, attn_ref, mod_ref, wo_ref, g2_ref, wup_ref, cw_ref, cb_ref, wdn_ref, gf_ref,
                o_ref, tail_sc, *, tm):
    @pl.when(pl.program_id(0) == 0)
    def _():
        tail_sc[...] = jnp.zeros(tail_sc.shape, F32)

    gate1 = mod_ref[:, 2 * D_MODEL:3 * D_MODEL]
    shift2 = mod_ref[:, 3 * D_MODEL:4 * D_MODEL]
    scale2 = mod_ref[:, 4 * D_MODEL:5 * D_MODEL]
    gate2 = mod_ref[:, 5 * D_MODEL:6 * D_MODEL]

    mix = (jnp.dot(lru_ref[...], wo_ref[0:LRU_WIDTH, :], preferred_element_type=F32)
           + jnp.dot(attn_ref[...], wo_ref[LRU_WIDTH:, :], preferred_element_type=F32))
    x1 = x_ref[...] + gate1 * mix
    h2 = (_rms_norm(x1, g2_ref[...]) * (1.0 + scale2) + shift2).astype(BF16)

    ff = jnp.zeros((tm, D_MODEL), F32)
    for c in range(D_FF // FFN_CK):
        lo = c * FFN_CK
        a = jnp.dot(h2, wup_ref[:, lo:lo + FFN_CK], preferred_element_type=F32)
        g = jnp.dot(h2, wup_ref[:, D_FF + lo:D_FF + lo + FFN_CK], preferred_element_type=F32)
        prev8 = tail_sc[:, lo:lo + FFN_CK]
        ac = cb_ref[:, lo:lo + FFN_CK]
        for j in range(CONV_FFN):
            sh = CONV_FFN - 1 - j
            a_s = a if sh == 0 else _shift_rows(prev8, a, sh)
            ac = ac + a_s * cw_ref[j:j + 1, lo:lo + FFN_CK]
        tail_sc[:, lo:lo + FFN_CK] = a[tm - SUBLANES:]
        u = (_gelu_tanh(ac) * g).astype(BF16)
        ff = ff + jnp.dot(u, wdn_ref[lo:lo + FFN_CK, :], preferred_element_type=F32)

    x2 = x1 + gate2 * ff
    o_ref[...] = _rms_norm(x2, gf_ref[...])


def _out_ffn(x2d, lru, attn, mod, w_out, g2, w_up, conv_w, conv_b, w_down, g_final):
    s = x2d.shape[0]
    tm = min(FFN_TM, s)
    row = lambda w: pl.BlockSpec((tm, w), lambda i: (i, 0))
    const = lambda shape: pl.BlockSpec(shape, lambda i: (0, 0), pipeline_mode=pl.Buffered(1))
    return pl.pallas_call(
        functools.partial(_ffn_kernel, tm=tm),
        grid=(s // tm,),
        in_specs=[row(D_MODEL), row(LRU_WIDTH), row(ATTN_WIDTH),
                  const((1, 6 * D_MODEL)),
                  const((D_MODEL, D_MODEL)), const((1, D_MODEL)),
                  const((D_MODEL, 2 * D_FF)),
                  const((CONV_FFN, D_FF)), const((1, D_FF)),
                  const((D_FF, D_MODEL)), const((1, D_MODEL))],
        out_specs=row(D_MODEL),
        out_shape=jax.ShapeDtypeStruct((s, D_MODEL), F32),
        scratch_shapes=[pltpu.VMEM((SUBLANES, D_FF), F32)],
        compiler_params=pltpu.CompilerParams(dimension_semantics=("arbitrary",),
                                             vmem_limit_bytes=VMEM_LIMIT),
        name="out_ffn",
    )(x2d, lru, attn, mod, w_out, g2, w_up, conv_w, conv_b, w_down, g_final)


def _block_diag(w):
    nb, bs, _ = w.shape
    eye = jnp.eye(nb, dtype=w.dtype)
    return (w[:, :, None, :] * eye[:, None, :, None]).reshape(nb * bs, nb * bs)


def kernel(x, c, w_ada, b_ada, g_norm1, w_in, conv_lru_w, conv_lru_b, lru_wa, lru_ba, lru_wx, lru_bx, lru_lambda,
           lam_q1, lam_k1, lam_q2, lam_k2, g_subln, w_out, g_norm2, w_up, conv_ffn_w, conv_ffn_b, w_down, rel_bias,
           g_final):
    b, s, d = x.shape
    assert b == 1 and d == D_MODEL and w_ada.shape[0] == 1
    x2d = x.reshape(s, d)
    row = lambda a: a.reshape(1, -1)

    mod = _adaln_mod(c.reshape(d, 1), w_ada[0], row(b_ada[0]))

    q, k, v, xr, yg = _in_proj(x2d, mod, row(g_norm1[0]), w_in[0].astype(BF16))

    tq, tk = min(ATT_TQ, s), min(ATT_TK, s)
    bias = _bias_tiles(rel_bias.T.reshape(-1), tq, tk)
    lamv = jnp.stack([lam_q1[0], lam_k1[0], lam_q2[0], lam_k2[0]])
    attn = _diff_attn(q, k, v, bias, lamv, row(g_subln[0]))

    w_gates = jnp.concatenate([_block_diag(lru_wa[0]), _block_diag(lru_wx[0])], axis=1).astype(BF16)
    b_gates = jnp.concatenate([lru_ba[0], lru_bx[0]]).reshape(1, -1)
    lru = _rg_lru(xr, yg, conv_lru_w[0], row(conv_lru_b[0]), w_gates, b_gates, row(lru_lambda[0]))

    out = _out_ffn(x2d, lru, attn, mod, w_out[0].astype(BF16), row(g_norm2[0]), w_up[0].astype(BF16),
                   conv_ffn_w[0], row(conv_ffn_b[0]), w_down[0].astype(BF16), row(g_final))
    return out.reshape(b, s, d)
```

```python
import functools
import math

import jax
import jax.numpy as jnp
from jax import lax
from jax.experimental import pallas as pl
from jax.experimental.pallas import tpu as pltpu

F32 = jnp.float32
BF16 = jnp.bfloat16

D_MODEL = 1024
LRU_WIDTH = 512
LRU_BLOCKS = 8
LRU_BLOCK = LRU_WIDTH // LRU_BLOCKS
CONV_LRU = 4
LRU_C = 8.0
N_HEADS = 4
HEAD_DIM = 64
V_DIM = 2 * HEAD_DIM
QK_WIDTH = N_HEADS * 2 * HEAD_DIM
ATTN_WIDTH = N_HEADS * V_DIM
D_IN = 2 * QK_WIDTH + ATTN_WIDTH + 2 * LRU_WIDTH
D_FF = 3 * D_MODEL
CONV_FFN = 3
NUM_BUCKETS = 32
MAX_EXACT = NUM_BUCKETS // 2
MAX_DISTANCE = 128
EPS = 1e-6
NEG_INF = -1e30
LAMBDA_INIT = 0.8 - 0.6 * math.exp(-0.3 * 0)
LOG2E = math.log2(math.e)

LANES = 128
SUBLANES = 8
BF16_SUBLANES = 16
VMEM_LIMIT = 56 * 1024 * 1024

V_ROWS = V_DIM + BF16_SUBLANES

MOD_TN = 1536
ATT_TQ = 512
ATT_TK = 512
LRU_TC = 256
FFN_TM = 512
FFN_CK = 1536


def _rms_norm(x, g):
    y = x * lax.rsqrt(jnp.mean(x * x, axis=-1, keepdims=True) + EPS)
    return y * g


def _gelu_tanh(x):
    cdf = 0.5 * (1.0 + jnp.tanh(math.sqrt(2.0 / math.pi) * (x + 0.044715 * (x * x * x))))
    return x * cdf


def _shift_rows(prev8, x, s):
    ext = jnp.concatenate([prev8, x], axis=0)
    return pltpu.roll(ext, s, 0)[SUBLANES:]


def _adaln_kernel(c_ref, w_ref, b_ref, o_ref):
    c = c_ref[...]
    cond = c * jax.nn.sigmoid(c)
    o_ref[...] = jnp.sum(cond * w_ref[...], axis=0, keepdims=True) + b_ref[...]


def _adaln_mod(c_col, w_ada, b_ada):
    d, n = w_ada.shape
    return pl.pallas_call(
        _adaln_kernel,
        grid=(n // MOD_TN,),
        in_specs=[pl.BlockSpec((d, 1), lambda j: (0, 0)),
                  pl.BlockSpec((d, MOD_TN), lambda j: (0, j)),
                  pl.BlockSpec((1, MOD_TN), lambda j: (0, j))],
        out_specs=pl.BlockSpec((1, MOD_TN), lambda j: (0, j)),
        out_shape=jax.ShapeDtypeStruct((1, n), F32),
        compiler_params=pltpu.CompilerParams(vmem_limit_bytes=VMEM_LIMIT),
        name="adaln_mod",
    )(c_col, w_ada, b_ada)


def _in_proj_kernel(x_ref, mod_ref, g_ref, w_ref, q_ref, k_ref, v_ref, xr_ref, yg_ref):
    x = x_ref[...]
    tm = x.shape[0]
    shift1 = mod_ref[:, 0:D_MODEL]
    scale1 = mod_ref[:, D_MODEL:2 * D_MODEL]
    h = _rms_norm(x, g_ref[...]) * (1.0 + scale1) + shift1
    proj = jnp.dot(h.astype(BF16), w_ref[...], preferred_element_type=F32)
    row = lax.broadcasted_iota(jnp.int32, (V_ROWS - V_DIM, tm), 0)
    ones_rows = jnp.where(row == 0, 1.0, 0.0).astype(BF16)
    for hd in range(N_HEADS):
        lo = hd * V_DIM
        q_ref[hd] = (proj[:, lo:lo + V_DIM].T * (HEAD_DIM ** -0.5 * LOG2E)).astype(BF16)
        k_ref[hd] = proj[:, QK_WIDTH + lo:QK_WIDTH + lo + V_DIM].astype(BF16)
        v_ref[hd, 0] = jnp.concatenate(
            [proj[:, 2 * QK_WIDTH + lo:2 * QK_WIDTH + lo + V_DIM].T.astype(BF16), ones_rows], axis=0)
    base = 2 * QK_WIDTH + ATTN_WIDTH
    xr_ref[...] = proj[:, base:base + LRU_WIDTH]
    yg_ref[...] = proj[:, base + LRU_WIDTH:base + 2 * LRU_WIDTH]


def _in_proj(x2d, mod, g1, w_in_bf16, tm):
    s = x2d.shape[0]
    row_spec = pl.BlockSpec((tm, LRU_WIDTH), lambda i: (i, 0))
    return pl.pallas_call(
        _in_proj_kernel,
        grid=(s // tm,),
        in_specs=[pl.BlockSpec((tm, D_MODEL), lambda i: (i, 0)),
                  pl.BlockSpec((1, 6 * D_MODEL), lambda i: (0, 0)),
                  pl.BlockSpec((1, D_MODEL), lambda i: (0, 0)),
                  pl.BlockSpec((D_MODEL, D_IN), lambda i: (0, 0))],
        out_specs=[pl.BlockSpec((N_HEADS, V_DIM, tm), lambda i: (0, 0, i)),
                   pl.BlockSpec((N_HEADS, tm, V_DIM), lambda i: (0, i, 0)),
                   pl.BlockSpec((N_HEADS, 1, V_ROWS, tm), lambda i: (0, i, 0, 0)),
                   row_spec, row_spec],
        out_shape=[jax.ShapeDtypeStruct((N_HEADS, V_DIM, s), BF16),
                   jax.ShapeDtypeStruct((N_HEADS, s, V_DIM), BF16),
                   jax.ShapeDtypeStruct((N_HEADS, s // tm, V_ROWS, tm), BF16)]
        + [jax.ShapeDtypeStruct((s, LRU_WIDTH), F32)] * 2,
        compiler_params=pltpu.CompilerParams(dimension_semantics=("parallel",),
                                             vmem_limit_bytes=VMEM_LIMIT),
        name="in_proj",
    )(x2d, mod, g1, w_in_bf16)


def _bias_tiles_kernel(table_ref, o_ref, *, tq, tk, nd):
    hd = pl.program_id(0)
    blk = MAX_DISTANCE
    kpos = lax.broadcasted_iota(jnp.int32, (blk, blk), 0)
    qpos = lax.broadcasted_iota(jnp.int32, (blk, blk), 1)
    far = table_ref[hd * NUM_BUCKETS + NUM_BUCKETS - 1]

    def band_block(offset):
        rel = offset * blk + qpos - kpos
        n = jnp.maximum(rel, 0)
        nf = jnp.maximum(n, 1).astype(F32)
        large = MAX_EXACT + (jnp.log(nf / MAX_EXACT) / math.log(MAX_DISTANCE / MAX_EXACT)
                             * (NUM_BUCKETS - MAX_EXACT)).astype(jnp.int32)
        large = jnp.minimum(large, NUM_BUCKETS - 1)
        bucket = jnp.where(n < MAX_EXACT, n, large)
        val = jnp.zeros((blk, blk), F32)
        for b in range(NUM_BUCKETS):
            val = jnp.where(bucket == b, (table_ref[hd * NUM_BUCKETS + b] - far) * LOG2E, val)
        return jnp.where(rel >= 0, val, NEG_INF)

    blocks = {0: band_block(0), 1: band_block(1)}
    zeros = jnp.zeros((blk, blk), F32)
    neg = jnp.full((blk, blk), NEG_INF, F32)
    for dd in range(nd):
        for a in range(tq // blk):
            for b in range(tk // blk):
                off = ((dd + 1) * tk - tq) // blk + a - b
                o_ref[0, dd, b * blk:(b + 1) * blk, a * blk:(a + 1) * blk] = (
                    neg if off < 0 else blocks.get(off, zeros))
    o_ref[0, nd] = jnp.zeros((tk, tq), F32)
    o_ref[0, nd + 1] = jnp.full((tk, tq), NEG_INF, F32)


def _bias_tiles(table_flat, tq, tk):
    assert tq % tk == 0 and tk % MAX_DISTANCE == 0
    nd = tq // tk + 1
    return pl.pallas_call(
        functools.partial(_bias_tiles_kernel, tq=tq, tk=tk, nd=nd),
        grid=(N_HEADS,),
        in_specs=[pl.BlockSpec(memory_space=pltpu.SMEM)],
        out_specs=pl.BlockSpec((1, nd + 2, tk, tq), lambda h: (h, 0, 0, 0)),
        out_shape=jax.ShapeDtypeStruct((N_HEADS, nd + 2, tk, tq), F32),
        name="bias_tiles",
    )(table_flat)


def _attn_kernel(q_ref, k_ref, v_ref, bias_ref, lamv_ref, gs_ref, o_ref, sa_sc, sb_sc, pa_sc, pb_sc, m_sc, acc_sc,
                 *, tq, tk, nd):
    i = pl.program_id(1)
    qt = q_ref[0]
    row = lax.broadcasted_iota(jnp.int32, qt.shape, 0)
    zero = jnp.zeros_like(qt)
    qs = jnp.concatenate([jnp.where(row < HEAD_DIM, qt, zero),
                          jnp.where(row >= HEAD_DIM, qt, zero)], axis=1)

    q_start = i * tq
    n_far = jnp.maximum(q_start - (MAX_DISTANCE - 1), 0) // tk
    last = (q_start + tq - 1) // tk
    n_pairs = (last + 2) // 2

    def scores(j, s_ref, part_ref):
        jc = jnp.minimum(j, last)
        kb = k_ref[0, pl.ds(pl.multiple_of(jc * tk, tk), tk), :]
        s = jnp.dot(kb, qs, preferred_element_type=F32)
        dd = (q_start + tq - (jc + 1) * tk) // tk
        idx = jnp.where(j > last, nd + 1, jnp.where(j < n_far, nd, dd))
        bt = bias_ref[0, idx]
        s = jnp.concatenate([s[:, :tq] + bt, s[:, tq:] + bt], axis=1)
        s_ref[...] = s
        part_ref[...] = jnp.max(s.reshape(tk // SUBLANES, SUBLANES, 2 * tq), axis=0)

    def accumulate(s_ref, part_ref, j):
        jc = jnp.minimum(j, last)
        vb = v_ref[0, jc]
        m_prev = m_sc[...]
        m_new = jnp.maximum(m_prev, jnp.max(part_ref[...], axis=0, keepdims=True))
        alpha = jnp.exp2(m_prev - m_new)
        p = jnp.exp2(s_ref[...] - m_new)
        acc_sc[...] = alpha * acc_sc[...] + jnp.dot(vb, p.astype(BF16), preferred_element_type=F32)
        m_sc[...] = m_new

    m_sc[...] = jnp.full(m_sc.shape, NEG_INF, F32)
    acc_sc[...] = jnp.zeros(acc_sc.shape, F32)
    scores(0, sa_sc, pa_sc)

    def pair_body(t, carry):
        j = 2 * t
        scores(j + 1, sb_sc, pb_sc)
        accumulate(sa_sc, pa_sc, j)
        scores(j + 2, sa_sc, pa_sc)
        accumulate(sb_sc, pb_sc, j + 1)
        return carry

    lax.fori_loop(0, n_pairs, pair_body, 0)

    lv = lamv_ref[...]
    d1 = jnp.sum(lv[0:1] * lv[1:2], axis=1, keepdims=True)
    d2 = jnp.sum(lv[2:3] * lv[3:4], axis=1, keepdims=True)
    lam = jnp.exp(d1) - jnp.exp(d2) + LAMBDA_INIT

    acc = acc_sc[...]
    out = acc[:V_DIM] / acc[V_DIM:V_DIM + 1]
    diff = out[:, :tq] - lam * out[:, tq:]
    y = diff * lax.rsqrt(jnp.mean(diff * diff, axis=0, keepdims=True) + EPS)
    y = y * gs_ref[...] * (1.0 - LAMBDA_INIT)
    o_ref[...] = y.T.astype(o_ref.dtype)


def _diff_attn(qt, k, vt_aug, bias, lamv, g_subln_col, tq, tk):
    s = k.shape[1]
    nd = bias.shape[1] - 2
    return pl.pallas_call(
        functools.partial(_attn_kernel, tq=tq, tk=tk, nd=nd),
        grid=(N_HEADS, s // tq),
        in_specs=[pl.BlockSpec((1, V_DIM, tq), lambda h, i: (h, 0, i)),
                  pl.BlockSpec((1, s, V_DIM), lambda h, i: (h, 0, 0)),
                  pl.BlockSpec((1, s // tk, V_ROWS, tk), lambda h, i: (h, 0, 0, 0)),
                  pl.BlockSpec((1, nd + 2, tk, tq), lambda h, i: (h, 0, 0, 0), pipeline_mode=pl.Buffered(1)),
                  pl.BlockSpec((4, HEAD_DIM), lambda h, i: (0, 0)),
                  pl.BlockSpec((V_DIM, 1), lambda h, i: (0, 0))],
        out_specs=pl.BlockSpec((tq, V_DIM), lambda h, i: (i, h)),
        out_shape=jax.ShapeDtypeStruct((s, ATTN_WIDTH), BF16),
        scratch_shapes=[pltpu.VMEM((tk, 2 * tq), F32),
                        pltpu.VMEM((tk, 2 * tq), F32),
                        pltpu.VMEM((SUBLANES, 2 * tq), F32),
                        pltpu.VMEM((SUBLANES, 2 * tq), F32),
                        pltpu.VMEM((1, 2 * tq), F32),
                        pltpu.VMEM((V_ROWS, 2 * tq), F32)],
        compiler_params=pltpu.CompilerParams(dimension_semantics=("parallel", "parallel"),
                                             vmem_limit_bytes=VMEM_LIMIT),
        name="diff_attn",
    )(qt, k, vt_aug, bias, lamv, g_subln_col)


def _lru_kernel(xr_ref, yg_ref, cw_ref, cb_ref, wg_ref, bg_ref, lam_ref, o_ref, tail_sc, h_sc, *, tc):
    @pl.when(pl.program_id(0) == 0)
    def _():
        tail_sc[...] = jnp.zeros(tail_sc.shape, F32)
        h_sc[...] = jnp.zeros(h_sc.shape, F32)

    x = xr_ref[...]
    prev8 = tail_sc[...]
    cw = cw_ref[...]
    xc = cb_ref[...]
    for j in range(CONV_LRU):
        sh = CONV_LRU - 1 - j
        xs = x if sh == 0 else _shift_rows(prev8, x, sh)
        xc = xc + xs * cw[j:j + 1]
    tail_sc[...] = x[tc - SUBLANES:]

    gates = jnp.dot(xc.astype(BF16), wg_ref[...], preferred_element_type=F32) + bg_ref[...]
    sig = 0.5 * jnp.tanh(0.5 * gates) + 0.5
    r = sig[:, :LRU_WIDTH]
    ig = sig[:, LRU_WIDTH:]
    z = -lam_ref[...]
    softplus = jnp.maximum(z, 0.0) + jnp.log1p(jnp.exp(-jnp.abs(z)))
    log_a = -LRU_C * r * softplus
    a = jnp.exp(log_a)
    th = jnp.tanh(log_a)
    b = jnp.sqrt(-2.0 * th / (1.0 - th)) * (ig * xc)

    ng = tc // SUBLANES
    a = a.reshape(ng, SUBLANES, LRU_WIDTH)
    b = b.reshape(ng, SUBLANES, LRU_WIDTH)
    row = lax.broadcasted_iota(jnp.int32, a.shape, 1)
    sh = 1
    while sh < SUBLANES:
        valid = row >= sh
        a_s = jnp.where(valid, pltpu.roll(a, sh, 1), 1.0)
        b_s = jnp.where(valid, pltpu.roll(b, sh, 1), 0.0)
        b = a * b_s + b
        a = a * a_s
        sh *= 2
    carry = h_sc[0:1]
    hs = []
    for g in range(ng):
        hg = a[g] * carry + b[g]
        hs.append(hg)
        carry = hg[SUBLANES - 1:SUBLANES]
    h = jnp.concatenate(hs, axis=0)
    h_sc[0:1] = carry
    o_ref[...] = (h * _gelu_tanh(yg_ref[...])).astype(o_ref.dtype)


def _rg_lru(xr, yg, conv_w, conv_b, w_gates_bf16, b_gates, lam):
    s = xr.shape[0]
    tc = min(LRU_TC, s)
    row_spec = pl.BlockSpec((tc, LRU_WIDTH), lambda i: (i, 0))
    const = lambda shape: pl.BlockSpec(shape, lambda i: (0, 0))
    return pl.pallas_call(
        functools.partial(_lru_kernel, tc=tc),
        grid=(s // tc,),
        in_specs=[row_spec, row_spec,
                  const((CONV_LRU, LRU_WIDTH)), const((1, LRU_WIDTH)),
                  const((LRU_WIDTH, 2 * LRU_WIDTH)), const((1, 2 * LRU_WIDTH)),
                  const((1, LRU_WIDTH))],
        out_specs=row_spec,
        out_shape=jax.ShapeDtypeStruct((s, LRU_WIDTH), BF16),
        scratch_shapes=[pltpu.VMEM((SUBLANES, LRU_WIDTH), F32),
                        pltpu.VMEM((SUBLANES, LRU_WIDTH), F32)],
        compiler_params=pltpu.CompilerParams(dimension_semantics=("arbitrary",),
                                             vmem_limit_bytes=VMEM_LIMIT),
        name="rg_lru",
    )(xr, yg, conv_w, conv_b, w_gates_bf16, b_gates, lam)


def _ffn_kernel(x_ref, lru_ref, attn_ref, mod_ref, wo_ref, g2_ref, wup_ref, cw_ref, cb_ref, wdn_ref, gf_ref,
                o_ref, tail_sc, *, tm):
    @pl.when(pl.program_id(0) == 0)
    def _():
        tail_sc[...] = jnp.zeros(tail_sc.shape, F32)

    gate1 = mod_ref[:, 2 * D_MODEL:3 * D_MODEL]
    shift2 = mod_ref[:, 3 * D_MODEL:4 * D_MODEL]
    scale2 = mod_ref[:, 4 * D_MODEL:5 * D_MODEL]
    gate2 = mod_ref[:, 5 * D_MODEL:6 * D_MODEL]

    mix = (jnp.dot(lru_ref[...], wo_ref[0:LRU_WIDTH, :], preferred_element_type=F32)
           + jnp.dot(attn_ref[...], wo_ref[LRU_WIDTH:, :], preferred_element_type=F32))
    x1 = x_ref[...] + gate1 * mix
    h2 = (_rms_norm(x1, g2_ref[...]) * (1.0 + scale2) + shift2).astype(BF16)

    ff = jnp.zeros((tm, D_MODEL), F32)
    for c in range(D_FF // FFN_CK):
        lo = c * FFN_CK
        a = jnp.dot(h2, wup_ref[:, lo:lo + FFN_CK], preferred_element_type=F32)
        g = jnp.dot(h2, wup_ref[:, D_FF + lo:D_FF + lo + FFN_CK], preferred_element_type=F32)
        prev8 = tail_sc[:, lo:lo + FFN_CK]
        ac = cb_ref[:, lo:lo + FFN_CK]
        for j in range(CONV_FFN):
            sh = CONV_FFN - 1 - j
            a_s = a if sh == 0 else _shift_rows(prev8, a, sh)
            ac = ac + a_s * cw_ref[j:j + 1, lo:lo + FFN_CK]
        tail_sc[:, lo:lo + FFN_CK] = a[tm - SUBLANES:]
        u = (_gelu_tanh(ac) * g).astype(BF16)
        ff = ff + jnp.dot(u, wdn_ref[lo:lo + FFN_CK, :], preferred_element_type=F32)

    x2 = x1 + gate2 * ff
    o_ref[...] = _rms_norm(x2, gf_ref[...])


def _out_ffn(x2d, lru, attn, mod, w_out, g2, w_up, conv_w, conv_b, w_down, g_final):
    s = x2d.shape[0]
    tm = min(FFN_TM, s)
    row = lambda w: pl.BlockSpec((tm, w), lambda i: (i, 0))
    const = lambda shape: pl.BlockSpec(shape, lambda i: (0, 0), pipeline_mode=pl.Buffered(1))
    return pl.pallas_call(
        functools.partial(_ffn_kernel, tm=tm),
        grid=(s // tm,),
        in_specs=[row(D_MODEL), row(LRU_WIDTH), row(ATTN_WIDTH),
                  const((1, 6 * D_MODEL)),
                  const((D_MODEL, D_MODEL)), const((1, D_MODEL)),
                  const((D_MODEL, 2 * D_FF)),
                  const((CONV_FFN, D_FF)), const((1, D_FF)),
                  const((D_FF, D_MODEL)), const((1, D_MODEL))],
        out_specs=row(D_MODEL),
        out_shape=jax.ShapeDtypeStruct((s, D_MODEL), F32),
        scratch_shapes=[pltpu.VMEM((SUBLANES, D_FF), F32)],
        compiler_params=pltpu.CompilerParams(dimension_semantics=("arbitrary",),
                                             vmem_limit_bytes=VMEM_LIMIT),
        name="out_ffn",
    )(x2d, lru, attn, mod, w_out, g2, w_up, conv_w, conv_b, w_down, g_final)


def _block_diag(w):
    nb, bs, _ = w.shape
    eye = jnp.eye(nb, dtype=w.dtype)
    return (w[:, :, None, :] * eye[:, None, :, None]).reshape(nb * bs, nb * bs)


def kernel(x, c, w_ada, b_ada, g_norm1, w_in, conv_lru_w, conv_lru_b, lru_wa, lru_ba, lru_wx, lru_bx, lru_lambda,
           lam_q1, lam_k1, lam_q2, lam_k2, g_subln, w_out, g_norm2, w_up, conv_ffn_w, conv_ffn_b, w_down, rel_bias,
           g_final):
    b, s, d = x.shape
    assert b == 1 and d == D_MODEL and w_ada.shape[0] == 1
    x2d = x.reshape(s, d)
    row = lambda a: a.reshape(1, -1)

    mod = _adaln_mod(c.reshape(d, 1), w_ada[0], row(b_ada[0]))

    tq, tk = min(ATT_TQ, s), min(ATT_TK, s)
    qt, k, vt, xr, yg = _in_proj(x2d, mod, row(g_norm1[0]), w_in[0].astype(BF16), tk)

    bias = _bias_tiles(rel_bias.T.reshape(-1), tq, tk)
    lamv = jnp.stack([lam_q1[0], lam_k1[0], lam_q2[0], lam_k2[0]])
    attn = _diff_attn(qt, k, vt, bias, lamv, g_subln[0].reshape(-1, 1), tq, tk)

    w_gates = jnp.concatenate([_block_diag(lru_wa[0]), _block_diag(lru_wx[0])], axis=1).astype(BF16)
    b_gates = jnp.concatenate([lru_ba[0], lru_bx[0]]).reshape(1, -1)
    lru = _rg_lru(xr, yg, conv_lru_w[0], row(conv_lru_b[0]), w_gates, b_gates, row(lru_lambda[0]))

    out = _out_ffn(x2d, lru, attn, mod, w_out[0].astype(BF16), row(g_norm2[0]), w_up[0].astype(BF16),
                   conv_ffn_w[0], row(conv_ffn_b[0]), w_down[0].astype(BF16), row(g_final))
    return out.reshape(b, s, d)
```

```python
import functools
import math

import jax
import jax.numpy as jnp
from jax import lax
from jax.experimental import pallas as pl
from jax.experimental.pallas import tpu as pltpu

F32 = jnp.float32
BF16 = jnp.bfloat16

D_MODEL = 1024
LRU_WIDTH = 512
LRU_BLOCKS = 8
LRU_BLOCK = LRU_WIDTH // LRU_BLOCKS
CONV_LRU = 4
LRU_C = 8.0
N_HEADS = 4
HEAD_DIM = 64
V_DIM = 2 * HEAD_DIM
QK_WIDTH = N_HEADS * 2 * HEAD_DIM
ATTN_WIDTH = N_HEADS * V_DIM
D_IN = 2 * QK_WIDTH + ATTN_WIDTH + 2 * LRU_WIDTH
D_FF = 3 * D_MODEL
CONV_FFN = 3
NUM_BUCKETS = 32
MAX_EXACT = NUM_BUCKETS // 2
MAX_DISTANCE = 128
EPS = 1e-6
NEG_INF = -1e30
LAMBDA_INIT = 0.8 - 0.6 * math.exp(-0.3 * 0)
LOG2E = math.log2(math.e)

LANES = 128
SUBLANES = 8
VMEM_LIMIT = 56 * 1024 * 1024

MOD_TN = 1536
ATT_TQ = 512
ATT_TK = 512
LRU_TC = 256
FFN_TM = 512
FFN_CK = 1536


def _rms_norm(x, g):
    y = x * lax.rsqrt(jnp.mean(x * x, axis=-1, keepdims=True) + EPS)
    return y * g


def _gelu_tanh(x):
    cdf = 0.5 * (1.0 + jnp.tanh(math.sqrt(2.0 / math.pi) * (x + 0.044715 * (x * x * x))))
    return x * cdf


def _shift_rows(prev8, x, s):
    ext = jnp.concatenate([prev8, x], axis=0)
    return pltpu.roll(ext, s, 0)[SUBLANES:]


def _adaln_kernel(c_ref, w_ref, b_ref, o_ref):
    c = c_ref[...]
    cond = c * jax.nn.sigmoid(c)
    o_ref[...] = jnp.sum(cond * w_ref[...], axis=0, keepdims=True) + b_ref[...]


def _adaln_mod(c_col, w_ada, b_ada):
    d, n = w_ada.shape
    return pl.pallas_call(
        _adaln_kernel,
        grid=(n // MOD_TN,),
        in_specs=[pl.BlockSpec((d, 1), lambda j: (0, 0)),
                  pl.BlockSpec((d, MOD_TN), lambda j: (0, j)),
                  pl.BlockSpec((1, MOD_TN), lambda j: (0, j))],
        out_specs=pl.BlockSpec((1, MOD_TN), lambda j: (0, j)),
        out_shape=jax.ShapeDtypeStruct((1, n), F32),
        compiler_params=pltpu.CompilerParams(vmem_limit_bytes=VMEM_LIMIT),
        name="adaln_mod",
    )(c_col, w_ada, b_ada)


def _in_proj_kernel(x_ref, mod_ref, g_ref, w_ref, q_ref, k_ref, v_ref, xr_ref, yg_ref):
    x = x_ref[...]
    shift1 = mod_ref[:, 0:D_MODEL]
    scale1 = mod_ref[:, D_MODEL:2 * D_MODEL]
    h = _rms_norm(x, g_ref[...]) * (1.0 + scale1) + shift1
    proj = jnp.dot(h.astype(BF16), w_ref[...], preferred_element_type=F32)
    lane = lax.broadcasted_iota(jnp.int32, (x.shape[0], V_DIM), 1)
    ones_col = jnp.where(lane == 0, 1.0, 0.0).astype(BF16)
    for hd in range(N_HEADS):
        lo = hd * V_DIM
        q_ref[hd] = (proj[:, lo:lo + V_DIM] * (HEAD_DIM ** -0.5 * LOG2E)).astype(BF16)
        k_ref[hd, 0] = proj[:, QK_WIDTH + lo:QK_WIDTH + lo + V_DIM].T.astype(BF16)
        v_ref[hd] = jnp.concatenate(
            [proj[:, 2 * QK_WIDTH + lo:2 * QK_WIDTH + lo + V_DIM].astype(BF16), ones_col], axis=1)
    base = 2 * QK_WIDTH + ATTN_WIDTH
    xr_ref[...] = proj[:, base:base + LRU_WIDTH]
    yg_ref[...] = proj[:, base + LRU_WIDTH:base + 2 * LRU_WIDTH]


def _in_proj(x2d, mod, g1, w_in_bf16, tm):
    s = x2d.shape[0]
    head_spec = pl.BlockSpec((N_HEADS, tm, V_DIM), lambda i: (0, i, 0))
    row_spec = pl.BlockSpec((tm, LRU_WIDTH), lambda i: (i, 0))
    return pl.pallas_call(
        _in_proj_kernel,
        grid=(s // tm,),
        in_specs=[pl.BlockSpec((tm, D_MODEL), lambda i: (i, 0)),
                  pl.BlockSpec((1, 6 * D_MODEL), lambda i: (0, 0)),
                  pl.BlockSpec((1, D_MODEL), lambda i: (0, 0)),
                  pl.BlockSpec((D_MODEL, D_IN), lambda i: (0, 0))],
        out_specs=[head_spec, pl.BlockSpec((N_HEADS, 1, V_DIM, tm), lambda i: (0, i, 0, 0)),
                   pl.BlockSpec((N_HEADS, tm, 2 * V_DIM), lambda i: (0, i, 0)), row_spec, row_spec],
        out_shape=[jax.ShapeDtypeStruct((N_HEADS, s, V_DIM), BF16),
                   jax.ShapeDtypeStruct((N_HEADS, s // tm, V_DIM, tm), BF16),
                   jax.ShapeDtypeStruct((N_HEADS, s, 2 * V_DIM), BF16)]
        + [jax.ShapeDtypeStruct((s, LRU_WIDTH), F32)] * 2,
        compiler_params=pltpu.CompilerParams(dimension_semantics=("parallel",),
                                             vmem_limit_bytes=VMEM_LIMIT),
        name="in_proj",
    )(x2d, mod, g1, w_in_bf16)


def _bias_tiles_kernel(table_ref, o_ref, *, tq, tk, nd):
    hd = pl.program_id(0)
    blk = MAX_DISTANCE
    qpos = lax.broadcasted_iota(jnp.int32, (blk, blk), 0)
    kpos = lax.broadcasted_iota(jnp.int32, (blk, blk), 1)
    far = table_ref[hd * NUM_BUCKETS + NUM_BUCKETS - 1]

    def band_block(offset):
        rel = offset * blk + qpos - kpos
        n = jnp.maximum(rel, 0)
        nf = jnp.maximum(n, 1).astype(F32)
        large = MAX_EXACT + (jnp.log(nf / MAX_EXACT) / math.log(MAX_DISTANCE / MAX_EXACT)
                             * (NUM_BUCKETS - MAX_EXACT)).astype(jnp.int32)
        large = jnp.minimum(large, NUM_BUCKETS - 1)
        bucket = jnp.where(n < MAX_EXACT, n, large)
        val = jnp.zeros((blk, blk), F32)
        for b in range(NUM_BUCKETS):
            val = jnp.where(bucket == b, (table_ref[hd * NUM_BUCKETS + b] - far) * LOG2E, val)
        return jnp.where(rel >= 0, val, NEG_INF)

    blocks = {0: band_block(0), 1: band_block(1)}
    zeros = jnp.zeros((blk, blk), F32)
    neg = jnp.full((blk, blk), NEG_INF, F32)
    for dd in range(nd):
        for a in range(tq // blk):
            for b in range(tk // blk):
                off = ((dd + 1) * tk - tq) // blk + a - b
                o_ref[0, dd, a * blk:(a + 1) * blk, b * blk:(b + 1) * blk] = (
                    neg if off < 0 else blocks.get(off, zeros))
    o_ref[0, nd] = jnp.zeros((tq, tk), F32)
    o_ref[0, nd + 1] = jnp.full((tq, tk), NEG_INF, F32)


def _bias_tiles(table_flat, tq, tk):
    assert tq % tk == 0 and tk % MAX_DISTANCE == 0
    nd = tq // tk + 1
    return pl.pallas_call(
        functools.partial(_bias_tiles_kernel, tq=tq, tk=tk, nd=nd),
        grid=(N_HEADS,),
        in_specs=[pl.BlockSpec(memory_space=pltpu.SMEM)],
        out_specs=pl.BlockSpec((1, nd + 2, tq, tk), lambda h: (h, 0, 0, 0)),
        out_shape=jax.ShapeDtypeStruct((N_HEADS, nd + 2, tq, tk), F32),
        name="bias_tiles",
    )(table_flat)


def _attn_kernel(q_ref, k_ref, v_ref, bias_ref, lamv_ref, gs_ref, o_ref, sa_sc, sb_sc, pa_sc, pb_sc, m_sc, acc_sc,
                 *, tq, tk, nd):
    i = pl.program_id(1)
    q = q_ref[0]
    lane = lax.broadcasted_iota(jnp.int32, q.shape, 1)
    zero = jnp.zeros_like(q)
    qs = jnp.concatenate([jnp.where(lane < HEAD_DIM, q, zero),
                          jnp.where(lane >= HEAD_DIM, q, zero)], axis=0)

    q_start = i * tq
    n_far = jnp.maximum(q_start - (MAX_DISTANCE - 1), 0) // tk
    last = (q_start + tq - 1) // tk
    n_pairs = (last + 2) // 2

    def scores(j, s_ref, part_ref):
        jc = jnp.minimum(j, last)
        s = jnp.dot(qs, k_ref[0, jc], preferred_element_type=F32)
        dd = (q_start + tq - (jc + 1) * tk) // tk
        idx = jnp.where(j > last, nd + 1, jnp.where(j < n_far, nd, dd))
        s = (s.reshape(2, tq, tk) + bias_ref[0, idx][None]).reshape(2 * tq, tk)
        s_ref[...] = s
        part_ref[...] = functools.reduce(
            jnp.maximum, [s[:, c * LANES:(c + 1) * LANES] for c in range(tk // LANES)])

    def accumulate(s_ref, part_ref, j):
        jc = jnp.minimum(j, last)
        vb = v_ref[0, pl.ds(pl.multiple_of(jc * tk, tk), tk), :]
        m_prev = m_sc[...]
        m_new = jnp.maximum(m_prev, jnp.max(part_ref[...], axis=1, keepdims=True))
        alpha = jnp.exp2(m_prev - m_new)
        p = jnp.exp2(s_ref[...] - jnp.concatenate([m_new] * (tk // LANES), axis=1))
        acc_sc[...] = (jnp.concatenate([alpha, alpha], axis=1) * acc_sc[...]
                       + jnp.dot(p.astype(BF16), vb, preferred_element_type=F32))
        m_sc[...] = m_new

    m_sc[...] = jnp.full(m_sc.shape, NEG_INF, F32)
    acc_sc[...] = jnp.zeros(acc_sc.shape, F32)
    scores(0, sa_sc, pa_sc)

    def pair_body(t, carry):
        j = 2 * t
        scores(j + 1, sb_sc, pb_sc)
        accumulate(sa_sc, pa_sc, j)
        scores(j + 2, sa_sc, pa_sc)
        accumulate(sb_sc, pb_sc, j + 1)
        return carry

    lax.fori_loop(0, n_pairs, pair_body, 0)

    lv = lamv_ref[...]
    d1 = jnp.sum(lv[0:1] * lv[1:2], axis=1, keepdims=True)
    d2 = jnp.sum(lv[2:3] * lv[3:4], axis=1, keepdims=True)
    lam = jnp.exp(d1) - jnp.exp(d2) + LAMBDA_INIT

    acc = acc_sc[...]
    out = acc[:, :V_DIM] / acc[:, V_DIM:V_DIM + 1]
    diff = out[:tq] - lam * out[tq:]
    o_ref[...] = (_rms_norm(diff, gs_ref[...]) * (1.0 - LAMBDA_INIT)).astype(o_ref.dtype)


def _diff_attn(q, kt, v_aug, bias, lamv, g_subln, tq, tk):
    s = q.shape[1]
    nd = bias.shape[1] - 2
    return pl.pallas_call(
        functools.partial(_attn_kernel, tq=tq, tk=tk, nd=nd),
        grid=(N_HEADS, s // tq),
        in_specs=[pl.BlockSpec((1, tq, V_DIM), lambda h, i: (h, i, 0)),
                  pl.BlockSpec((1, s // tk, V_DIM, tk), lambda h, i: (h, 0, 0, 0)),
                  pl.BlockSpec((1, s, 2 * V_DIM), lambda h, i: (h, 0, 0)),
                  pl.BlockSpec((1, nd + 2, tq, tk), lambda h, i: (h, 0, 0, 0), pipeline_mode=pl.Buffered(1)),
                  pl.BlockSpec((4, HEAD_DIM), lambda h, i: (0, 0)),
                  pl.BlockSpec((1, V_DIM), lambda h, i: (0, 0))],
        out_specs=pl.BlockSpec((tq, V_DIM), lambda h, i: (i, h)),
        out_shape=jax.ShapeDtypeStruct((s, ATTN_WIDTH), BF16),
        scratch_shapes=[pltpu.VMEM((2 * tq, tk), F32),
                        pltpu.VMEM((2 * tq, tk), F32),
                        pltpu.VMEM((2 * tq, LANES), F32),
                        pltpu.VMEM((2 * tq, LANES), F32),
                        pltpu.VMEM((2 * tq, LANES), F32),
                        pltpu.VMEM((2 * tq, 2 * V_DIM), F32)],
        compiler_params=pltpu.CompilerParams(dimension_semantics=("parallel", "parallel"),
                                             vmem_limit_bytes=VMEM_LIMIT),
        name="diff_attn",
    )(q, kt, v_aug, bias, lamv, g_subln)


def _lru_kernel(xr_ref, yg_ref, cw_ref, cb_ref, wg_ref, bg_ref, lam_ref, o_ref, tail_sc, h_sc, *, tc):
    @pl.when(pl.program_id(0) == 0)
    def _():
        tail_sc[...] = jnp.zeros(tail_sc.shape, F32)
        h_sc[...] = jnp.zeros(h_sc.shape, F32)

    x = xr_ref[...]
    prev8 = tail_sc[...]
    cw = cw_ref[...]
    xc = cb_ref[...]
    for j in range(CONV_LRU):
        sh = CONV_LRU - 1 - j
        xs = x if sh == 0 else _shift_rows(prev8, x, sh)
        xc = xc + xs * cw[j:j + 1]
    tail_sc[...] = x[tc - SUBLANES:]

    gates = jnp.dot(xc.astype(BF16), wg_ref[...], preferred_element_type=F32) + bg_ref[...]
    sig = 0.5 * jnp.tanh(0.5 * gates) + 0.5
    r = sig[:, :LRU_WIDTH]
    ig = sig[:, LRU_WIDTH:]
    z = -lam_ref[...]
    softplus = jnp.maximum(z, 0.0) + jnp.log1p(jnp.exp(-jnp.abs(z)))
    log_a = -LRU_C * r * softplus
    a = jnp.exp(log_a)
    th = jnp.tanh(log_a)
    b = jnp.sqrt(-2.0 * th / (1.0 - th)) * (ig * xc)

    ng = tc // SUBLANES
    a = a.reshape(ng, SUBLANES, LRU_WIDTH)
    b = b.reshape(ng, SUBLANES, LRU_WIDTH)
    row = lax.broadcasted_iota(jnp.int32, a.shape, 1)
    sh = 1
    while sh < SUBLANES:
        valid = row >= sh
        a_s = jnp.where(valid, pltpu.roll(a, sh, 1), 1.0)
        b_s = jnp.where(valid, pltpu.roll(b, sh, 1), 0.0)
        b = a * b_s + b
        a = a * a_s
        sh *= 2
    carry = h_sc[0:1]
    hs = []
    for g in range(ng):
        hg = a[g] * carry + b[g]
        hs.append(hg)
        carry = hg[SUBLANES - 1:SUBLANES]
    h = jnp.concatenate(hs, axis=0)
    h_sc[0:1] = carry
    o_ref[...] = (h * _gelu_tanh(yg_ref[...])).astype(o_ref.dtype)


def _rg_lru(xr, yg, conv_w, conv_b, w_gates_bf16, b_gates, lam):
    s = xr.shape[0]
    tc = min(LRU_TC, s)
    row_spec = pl.BlockSpec((tc, LRU_WIDTH), lambda i: (i, 0))
    const = lambda shape: pl.BlockSpec(shape, lambda i: (0, 0))
    return pl.pallas_call(
        functools.partial(_lru_kernel, tc=tc),
        grid=(s // tc,),
        in_specs=[row_spec, row_spec,
                  const((CONV_LRU, LRU_WIDTH)), const((1, LRU_WIDTH)),
                  const((LRU_WIDTH, 2 * LRU_WIDTH)), const((1, 2 * LRU_WIDTH)),
                  const((1, LRU_WIDTH))],
        out_specs=row_spec,
        out_shape=jax.ShapeDtypeStruct((s, LRU_WIDTH), BF16),
        scratch_shapes=[pltpu.VMEM((SUBLANES, LRU_WIDTH), F32),
                        pltpu.VMEM((SUBLANES, LRU_WIDTH), F32)],
        compiler_params=pltpu.CompilerParams(dimension_semantics=("arbitrary",),
                                             vmem_limit_bytes=VMEM_LIMIT),
        name="rg_lru",
    )(xr, yg, conv_w, conv_b, w_gates_bf16, b_gates, lam)


def _ffn_kernel(x_ref, lru_ref, attn_ref, mod_ref, wo_ref, g2_ref, wup_ref, cw_ref, cb_ref, wdn_ref, gf_ref,
                o_ref, tail_sc, *, tm):
    @pl.when(pl.program_id(0) == 0)
    def _():
        tail_sc[...] = jnp.zeros(tail_sc.shape, F32)

    gate1 = mod_ref[:, 2 * D_MODEL:3 * D_MODEL]
    shift2 = mod_ref[:, 3 * D_MODEL:4 * D_MODEL]
    scale2 = mod_ref[:, 4 * D_MODEL:5 * D_MODEL]
    gate2 = mod_ref[:, 5 * D_MODEL:6 * D_MODEL]

    mix = (jnp.dot(lru_ref[...], wo_ref[0:LRU_WIDTH, :], preferred_element_type=F32)
           + jnp.dot(attn_ref[...], wo_ref[LRU_WIDTH:, :], preferred_element_type=F32))
    x1 = x_ref[...] + gate1 * mix
    h2 = (_rms_norm(x1, g2_ref[...]) * (1.0 + scale2) + shift2).astype(BF16)

    ff = jnp.zeros((tm, D_MODEL), F32)
    for c in range(D_FF // FFN_CK):
        lo = c * FFN_CK
        a = jnp.dot(h2, wup_ref[:, lo:lo + FFN_CK], preferred_element_type=F32)
        g = jnp.dot(h2, wup_ref[:, D_FF + lo:D_FF + lo + FFN_CK], preferred_element_type=F32)
        prev8 = tail_sc[:, lo:lo + FFN_CK]
        ac = cb_ref[:, lo:lo + FFN_CK]
        for j in range(CONV_FFN):
            sh = CONV_FFN - 1 - j
            a_s = a if sh == 0 else _shift_rows(prev8, a, sh)
            ac = ac + a_s * cw_ref[j:j + 1, lo:lo + FFN_CK]
        tail_sc[:, lo:lo + FFN_CK] = a[tm - SUBLANES:]
        u = (_gelu_tanh(ac) * g).astype(BF16)
        ff = ff + jnp.dot(u, wdn_ref[lo:lo + FFN_CK, :], preferred_element_type=F32)

    x2 = x1 + gate2 * ff
    o_ref[...] = _rms_norm(x2, gf_ref[...])


def _out_ffn(x2d, lru, attn, mod, w_out, g2, w_up, conv_w, conv_b, w_down, g_final):
    s = x2d.shape[0]
    tm = min(FFN_TM, s)
    row = lambda w: pl.BlockSpec((tm, w), lambda i: (i, 0))
    const = lambda shape: pl.BlockSpec(shape, lambda i: (0, 0), pipeline_mode=pl.Buffered(1))
    return pl.pallas_call(
        functools.partial(_ffn_kernel, tm=tm),
        grid=(s // tm,),
        in_specs=[row(D_MODEL), row(LRU_WIDTH), row(ATTN_WIDTH),
                  const((1, 6 * D_MODEL)),
                  const((D_MODEL, D_MODEL)), const((1, D_MODEL)),
                  const((D_MODEL, 2 * D_FF)),
                  const((CONV_FFN, D_FF)), const((1, D_FF)),
                  const((D_FF, D_MODEL)), const((1, D_MODEL))],
        out_specs=row(D_MODEL),
        out_shape=jax.ShapeDtypeStruct((s, D_MODEL), F32),
        scratch_shapes=[pltpu.VMEM((SUBLANES, D_FF), F32)],
        compiler_params=pltpu.CompilerParams(dimension_semantics=("arbitrary",),
                                             vmem_limit_bytes=VMEM_LIMIT),
        name="out_ffn",
    )(x2d, lru, attn, mod, w_out, g2, w_up, conv_w, conv_b, w_down, g_final)


def _block_diag(w):
    nb, bs, _ = w.shape
    eye = jnp.eye(nb, dtype=w.dtype)
    return (w[:, :, None, :] * eye[:, None, :, None]).reshape(nb * bs, nb * bs)


def kernel(x, c, w_ada, b_ada, g_norm1, w_in, conv_lru_w, conv_lru_b, lru_wa, lru_ba, lru_wx, lru_bx, lru_lambda,
           lam_q1, lam_k1, lam_q2, lam_k2, g_subln, w_out, g_norm2, w_up, conv_ffn_w, conv_ffn_b, w_down, rel_bias,
           g_final):
    b, s, d = x.shape
    assert b == 1 and d == D_MODEL and w_ada.shape[0] == 1
    x2d = x.reshape(s, d)
    row = lambda a: a.reshape(1, -1)

    mod = _adaln_mod(c.reshape(d, 1), w_ada[0], row(b_ada[0]))

    tq, tk = min(ATT_TQ, s), min(ATT_TK, s)
    q, kt, v, xr, yg = _in_proj(x2d, mod, row(g_norm1[0]), w_in[0].astype(BF16), tk)

    bias = _bias_tiles(rel_bias.T.reshape(-1), tq, tk)
    lamv = jnp.stack([lam_q1[0], lam_k1[0], lam_q2[0], lam_k2[0]])
    attn = _diff_attn(q, kt, v, bias, lamv, row(g_subln[0]), tq, tk)

    w_gates = jnp.concatenate([_block_diag(lru_wa[0]), _block_diag(lru_wx[0])], axis=1).astype(BF16)
    b_gates = jnp.concatenate([lru_ba[0], lru_bx[0]]).reshape(1, -1)
    lru = _rg_lru(xr, yg, conv_lru_w[0], row(conv_lru_b[0]), w_gates, b_gates, row(lru_lambda[0]))

    out = _out_ffn(x2d, lru, attn, mod, w_out[0].astype(BF16), row(g_norm2[0]), w_up[0].astype(BF16),
                   conv_ffn_w[0], row(conv_ffn_b[0]), w_down[0].astype(BF16), row(g_final))
    return out.reshape(b, s, d)
```

```python
import functools
import math

import jax
import jax.numpy as jnp
from jax import lax
from jax.experimental import pallas as pl
from jax.experimental.pallas import tpu as pltpu

F32 = jnp.float32
BF16 = jnp.bfloat16

D_MODEL = 1024
LRU_WIDTH = 512
LRU_BLOCKS = 8
LRU_BLOCK = LRU_WIDTH // LRU_BLOCKS
CONV_LRU = 4
LRU_C = 8.0
N_HEADS = 4
HEAD_DIM = 64
V_DIM = 2 * HEAD_DIM
QK_WIDTH = N_HEADS * 2 * HEAD_DIM
ATTN_WIDTH = N_HEADS * V_DIM
D_IN = 2 * QK_WIDTH + ATTN_WIDTH + 2 * LRU_WIDTH
D_FF = 3 * D_MODEL
CONV_FFN = 3
NUM_BUCKETS = 32
MAX_EXACT = NUM_BUCKETS // 2
MAX_DISTANCE = 128
EPS = 1e-6
NEG_INF = -1e30
LAMBDA_INIT = 0.8 - 0.6 * math.exp(-0.3 * 0)
LOG2E = math.log2(math.e)

LANES = 128
SUBLANES = 8
VMEM_LIMIT = 56 * 1024 * 1024

MOD_TN = 1536
ATT_TQ = 1024
ATT_TK = 512
LRU_TC = 256
FFN_TM = 512
FFN_CK = 1536


def _rms_norm(x, g):
    y = x * lax.rsqrt(jnp.mean(x * x, axis=-1, keepdims=True) + EPS)
    return y * g


def _gelu_tanh(x):
    cdf = 0.5 * (1.0 + jnp.tanh(math.sqrt(2.0 / math.pi) * (x + 0.044715 * (x * x * x))))
    return x * cdf


def _shift_rows(prev8, x, s):
    ext = jnp.concatenate([prev8, x], axis=0)
    return pltpu.roll(ext, s, 0)[SUBLANES:]


def _adaln_kernel(c_ref, w_ref, b_ref, o_ref):
    c = c_ref[...]
    cond = c * jax.nn.sigmoid(c)
    o_ref[...] = jnp.sum(cond * w_ref[...], axis=0, keepdims=True) + b_ref[...]


def _adaln_mod(c_col, w_ada, b_ada):
    d, n = w_ada.shape
    return pl.pallas_call(
        _adaln_kernel,
        grid=(n // MOD_TN,),
        in_specs=[pl.BlockSpec((d, 1), lambda j: (0, 0)),
                  pl.BlockSpec((d, MOD_TN), lambda j: (0, j)),
                  pl.BlockSpec((1, MOD_TN), lambda j: (0, j))],
        out_specs=pl.BlockSpec((1, MOD_TN), lambda j: (0, j)),
        out_shape=jax.ShapeDtypeStruct((1, n), F32),
        compiler_params=pltpu.CompilerParams(vmem_limit_bytes=VMEM_LIMIT),
        name="adaln_mod",
    )(c_col, w_ada, b_ada)


def _in_proj_kernel(x_ref, mod_ref, g_ref, w_ref, q_ref, k_ref, v_ref, xr_ref, yg_ref):
    x = x_ref[...]
    shift1 = mod_ref[:, 0:D_MODEL]
    scale1 = mod_ref[:, D_MODEL:2 * D_MODEL]
    h = _rms_norm(x, g_ref[...]) * (1.0 + scale1) + shift1
    proj = jnp.dot(h.astype(BF16), w_ref[...], preferred_element_type=F32)
    lane = lax.broadcasted_iota(jnp.int32, (x.shape[0], V_DIM), 1)
    ones_col = jnp.where(lane == 0, 1.0, 0.0).astype(BF16)
    for hd in range(N_HEADS):
        lo = hd * V_DIM
        q_ref[hd] = (proj[:, lo:lo + V_DIM] * (HEAD_DIM ** -0.5 * LOG2E)).astype(BF16)
        k_ref[hd, 0] = proj[:, QK_WIDTH + lo:QK_WIDTH + lo + V_DIM].T.astype(BF16)
        v_ref[hd] = jnp.concatenate(
            [proj[:, 2 * QK_WIDTH + lo:2 * QK_WIDTH + lo + V_DIM].astype(BF16), ones_col], axis=1)
    base = 2 * QK_WIDTH + ATTN_WIDTH
    xr_ref[...] = proj[:, base:base + LRU_WIDTH]
    yg_ref[...] = proj[:, base + LRU_WIDTH:base + 2 * LRU_WIDTH]


def _in_proj(x2d, mod, g1, w_in_bf16, tm):
    s = x2d.shape[0]
    head_spec = pl.BlockSpec((N_HEADS, tm, V_DIM), lambda i: (0, i, 0))
    row_spec = pl.BlockSpec((tm, LRU_WIDTH), lambda i: (i, 0))
    return pl.pallas_call(
        _in_proj_kernel,
        grid=(s // tm,),
        in_specs=[pl.BlockSpec((tm, D_MODEL), lambda i: (i, 0)),
                  pl.BlockSpec((1, 6 * D_MODEL), lambda i: (0, 0)),
                  pl.BlockSpec((1, D_MODEL), lambda i: (0, 0)),
                  pl.BlockSpec((D_MODEL, D_IN), lambda i: (0, 0))],
        out_specs=[head_spec, pl.BlockSpec((N_HEADS, 1, V_DIM, tm), lambda i: (0, i, 0, 0)),
                   pl.BlockSpec((N_HEADS, tm, 2 * V_DIM), lambda i: (0, i, 0)), row_spec, row_spec],
        out_shape=[jax.ShapeDtypeStruct((N_HEADS, s, V_DIM), BF16),
                   jax.ShapeDtypeStruct((N_HEADS, s // tm, V_DIM, tm), BF16),
                   jax.ShapeDtypeStruct((N_HEADS, s, 2 * V_DIM), BF16)]
        + [jax.ShapeDtypeStruct((s, LRU_WIDTH), F32)] * 2,
        compiler_params=pltpu.CompilerParams(dimension_semantics=("parallel",),
                                             vmem_limit_bytes=VMEM_LIMIT),
        name="in_proj",
    )(x2d, mod, g1, w_in_bf16)


def _bias_tiles_kernel(table_ref, o_ref, *, tq, tk, nd):
    hd = pl.program_id(0)
    blk = MAX_DISTANCE
    qpos = lax.broadcasted_iota(jnp.int32, (blk, blk), 0)
    kpos = lax.broadcasted_iota(jnp.int32, (blk, blk), 1)
    far = table_ref[hd * NUM_BUCKETS + NUM_BUCKETS - 1]

    def band_block(offset):
        rel = offset * blk + qpos - kpos
        n = jnp.maximum(rel, 0)
        nf = jnp.maximum(n, 1).astype(F32)
        large = MAX_EXACT + (jnp.log(nf / MAX_EXACT) / math.log(MAX_DISTANCE / MAX_EXACT)
                             * (NUM_BUCKETS - MAX_EXACT)).astype(jnp.int32)
        large = jnp.minimum(large, NUM_BUCKETS - 1)
        bucket = jnp.where(n < MAX_EXACT, n, large)
        val = jnp.zeros((blk, blk), F32)
        for b in range(NUM_BUCKETS):
            val = jnp.where(bucket == b, (table_ref[hd * NUM_BUCKETS + b] - far) * LOG2E, val)
        return jnp.where(rel >= 0, val, NEG_INF)

    blocks = {0: band_block(0), 1: band_block(1)}
    zeros = jnp.zeros((blk, blk), F32)
    neg = jnp.full((blk, blk), NEG_INF, F32)
    for dd in range(nd):
        for a in range(tq // blk):
            for b in range(tk // blk):
                off = ((dd + 1) * tk - tq) // blk + a - b
                o_ref[0, dd, a * blk:(a + 1) * blk, b * blk:(b + 1) * blk] = (
                    neg if off < 0 else blocks.get(off, zeros))
    o_ref[0, nd] = jnp.zeros((tq, tk), F32)
    o_ref[0, nd + 1] = jnp.full((tq, tk), NEG_INF, F32)


def _bias_tiles(table_flat, tq, tk):
    assert tq % tk == 0 and tk % MAX_DISTANCE == 0
    nd = tq // tk + 1
    return pl.pallas_call(
        functools.partial(_bias_tiles_kernel, tq=tq, tk=tk, nd=nd),
        grid=(N_HEADS,),
        in_specs=[pl.BlockSpec(memory_space=pltpu.SMEM)],
        out_specs=pl.BlockSpec((1, nd + 2, tq, tk), lambda h: (h, 0, 0, 0)),
        out_shape=jax.ShapeDtypeStruct((N_HEADS, nd + 2, tq, tk), F32),
        name="bias_tiles",
    )(table_flat)


def _attn_kernel(q_ref, k_ref, v_ref, bias_ref, lamv_ref, gs_ref, o_ref, sa_sc, sb_sc, pa_sc, pb_sc, m_sc, acc_sc,
                 *, tq, tk, nd):
    i = pl.program_id(1)
    q = q_ref[0]
    lane = lax.broadcasted_iota(jnp.int32, q.shape, 1)
    zero = jnp.zeros_like(q)
    qs = jnp.concatenate([jnp.where(lane < HEAD_DIM, q, zero),
                          jnp.where(lane >= HEAD_DIM, q, zero)], axis=0)

    q_start = i * tq
    n_far = jnp.maximum(q_start - (MAX_DISTANCE - 1), 0) // tk
    last = (q_start + tq - 1) // tk
    n_pairs = (last + 2) // 2

    def scores(j, s_ref, part_ref):
        jc = jnp.minimum(j, last)
        s = jnp.dot(qs, k_ref[0, jc], preferred_element_type=F32)
        dd = (q_start + tq - (jc + 1) * tk) // tk
        idx = jnp.where(j > last, nd + 1, jnp.where(j < n_far, nd, dd))
        s = (s.reshape(2, tq, tk) + bias_ref[0, idx][None]).reshape(2 * tq, tk)
        s_ref[...] = s
        part_ref[...] = functools.reduce(
            jnp.maximum, [s[:, c * LANES:(c + 1) * LANES] for c in range(tk // LANES)])

    def accumulate(s_ref, part_ref, j):
        jc = jnp.minimum(j, last)
        vb = v_ref[0, pl.ds(pl.multiple_of(jc * tk, tk), tk), :]
        m_prev = m_sc[...]
        m_new = jnp.maximum(m_prev, jnp.max(part_ref[...], axis=1, keepdims=True))
        alpha = jnp.exp2(m_prev - m_new)
        p = jnp.exp2(s_ref[...] - jnp.concatenate([m_new] * (tk // LANES), axis=1))
        acc_sc[...] = (jnp.concatenate([alpha, alpha], axis=1) * acc_sc[...]
                       + jnp.dot(p.astype(BF16), vb, preferred_element_type=F32))
        m_sc[...] = m_new

    m_sc[...] = jnp.full(m_sc.shape, NEG_INF, F32)
    acc_sc[...] = jnp.zeros(acc_sc.shape, F32)
    scores(0, sa_sc, pa_sc)

    def pair_body(t, carry):
        j = 2 * t
        scores(j + 1, sb_sc, pb_sc)
        accumulate(sa_sc, pa_sc, j)
        scores(j + 2, sa_sc, pa_sc)
        accumulate(sb_sc, pb_sc, j + 1)
        return carry

    lax.fori_loop(0, n_pairs, pair_body, 0)

    lv = lamv_ref[...]
    d1 = jnp.sum(lv[0:1] * lv[1:2], axis=1, keepdims=True)
    d2 = jnp.sum(lv[2:3] * lv[3:4], axis=1, keepdims=True)
    lam = jnp.exp(d1) - jnp.exp(d2) + LAMBDA_INIT

    acc = acc_sc[...]
    out = acc[:, :V_DIM] / acc[:, V_DIM:V_DIM + 1]
    diff = out[:tq] - lam * out[tq:]
    o_ref[...] = (_rms_norm(diff, gs_ref[...]) * (1.0 - LAMBDA_INIT)).astype(o_ref.dtype)


def _diff_attn(q, kt, v_aug, bias, lamv, g_subln, tq, tk):
    s = q.shape[1]
    nd = bias.shape[1] - 2
    return pl.pallas_call(
        functools.partial(_attn_kernel, tq=tq, tk=tk, nd=nd),
        grid=(N_HEADS, s // tq),
        in_specs=[pl.BlockSpec((1, tq, V_DIM), lambda h, i: (h, i, 0)),
                  pl.BlockSpec((1, s // tk, V_DIM, tk), lambda h, i: (h, 0, 0, 0)),
                  pl.BlockSpec((1, s, 2 * V_DIM), lambda h, i: (h, 0, 0)),
                  pl.BlockSpec((1, nd + 2, tq, tk), lambda h, i: (h, 0, 0, 0), pipeline_mode=pl.Buffered(1)),
                  pl.BlockSpec((4, HEAD_DIM), lambda h, i: (0, 0)),
                  pl.BlockSpec((1, V_DIM), lambda h, i: (0, 0))],
        out_specs=pl.BlockSpec((tq, V_DIM), lambda h, i: (i, h)),
        out_shape=jax.ShapeDtypeStruct((s, ATTN_WIDTH), BF16),
        scratch_shapes=[pltpu.VMEM((2 * tq, tk), F32),
                        pltpu.VMEM((2 * tq, tk), F32),
                        pltpu.VMEM((2 * tq, LANES), F32),
                        pltpu.VMEM((2 * tq, LANES), F32),
                        pltpu.VMEM((2 * tq, LANES), F32),
                        pltpu.VMEM((2 * tq, 2 * V_DIM), F32)],
        compiler_params=pltpu.CompilerParams(dimension_semantics=("parallel", "parallel"),
                                             vmem_limit_bytes=VMEM_LIMIT),
        name="diff_attn",
    )(q, kt, v_aug, bias, lamv, g_subln)


def _lru_kernel(xr_ref, yg_ref, cw_ref, cb_ref, wg_ref, bg_ref, lam_ref, o_ref, tail_sc, h_sc, *, tc):
    @pl.when(pl.program_id(0) == 0)
    def _():
        tail_sc[...] = jnp.zeros(tail_sc.shape, F32)
        h_sc[...] = jnp.zeros(h_sc.shape, F32)

    x = xr_ref[...]
    prev8 = tail_sc[...]
    cw = cw_ref[...]
    xc = cb_ref[...]
    for j in range(CONV_LRU):
        sh = CONV_LRU - 1 - j
        xs = x if sh == 0 else _shift_rows(prev8, x, sh)
        xc = xc + xs * cw[j:j + 1]
    tail_sc[...] = x[tc - SUBLANES:]

    gates = jnp.dot(xc.astype(BF16), wg_ref[...], preferred_element_type=F32) + bg_ref[...]
    sig = 0.5 * jnp.tanh(0.5 * gates) + 0.5
    r = sig[:, :LRU_WIDTH]
    ig = sig[:, LRU_WIDTH:]
    z = -lam_ref[...]
    softplus = jnp.maximum(z, 0.0) + jnp.log1p(jnp.exp(-jnp.abs(z)))
    log_a = -LRU_C * r * softplus
    a = jnp.exp(log_a)
    th = jnp.tanh(log_a)
    b = jnp.sqrt(-2.0 * th / (1.0 - th)) * (ig * xc)

    ng = tc // SUBLANES
    a = a.reshape(ng, SUBLANES, LRU_WIDTH)
    b = b.reshape(ng, SUBLANES, LRU_WIDTH)
    row = lax.broadcasted_iota(jnp.int32, a.shape, 1)
    sh = 1
    while sh < SUBLANES:
        valid = row >= sh
        a_s = jnp.where(valid, pltpu.roll(a, sh, 1), 1.0)
        b_s = jnp.where(valid, pltpu.roll(b, sh, 1), 0.0)
        b = a * b_s + b
        a = a * a_s
        sh *= 2
    carry = h_sc[0:1]
    hs = []
    for g in range(ng):
        hg = a[g] * carry + b[g]
        hs.append(hg)
        carry = hg[SUBLANES - 1:SUBLANES]
    h = jnp.concatenate(hs, axis=0)
    h_sc[0:1] = carry
    o_ref[...] = (h * _gelu_tanh(yg_ref[...])).astype(o_ref.dtype)


def _rg_lru(xr, yg, conv_w, conv_b, w_gates_bf16, b_gates, lam):
    s = xr.shape[0]
    tc = min(LRU_TC, s)
    row_spec = pl.BlockSpec((tc, LRU_WIDTH), lambda i: (i, 0))
    const = lambda shape: pl.BlockSpec(shape, lambda i: (0, 0))
    return pl.pallas_call(
        functools.partial(_lru_kernel, tc=tc),
        grid=(s // tc,),
        in_specs=[row_spec, row_spec,
                  const((CONV_LRU, LRU_WIDTH)), const((1, LRU_WIDTH)),
                  const((LRU_WIDTH, 2 * LRU_WIDTH)), const((1, 2 * LRU_WIDTH)),
                  const((1, LRU_WIDTH))],
        out_specs=row_spec,
        out_shape=jax.ShapeDtypeStruct((s, LRU_WIDTH), BF16),
        scratch_shapes=[pltpu.VMEM((SUBLANES, LRU_WIDTH), F32),
                        pltpu.VMEM((SUBLANES, LRU_WIDTH), F32)],
        compiler_params=pltpu.CompilerParams(dimension_semantics=("arbitrary",),
                                             vmem_limit_bytes=VMEM_LIMIT),
        name="rg_lru",
    )(xr, yg, conv_w, conv_b, w_gates_bf16, b_gates, lam)


def _ffn_kernel(x_ref, lru_ref, attn_ref, mod_ref, wo_ref, g2_ref, wup_ref, cw_ref, cb_ref, wdn_ref, gf_ref,
                o_ref, tail_sc, *, tm):
    @pl.when(pl.program_id(0) == 0)
    def _():
        tail_sc[...] = jnp.zeros(tail_sc.shape, F32)

    gate1 = mod_ref[:, 2 * D_MODEL:3 * D_MODEL]
    shift2 = mod_ref[:, 3 * D_MODEL:4 * D_MODEL]
    scale2 = mod_ref[:, 4 * D_MODEL:5 * D_MODEL]
    gate2 = mod_ref[:, 5 * D_MODEL:6 * D_MODEL]

    mix = (jnp.dot(lru_ref[...], wo_ref[0:LRU_WIDTH, :], preferred_element_type=F32)
           + jnp.dot(attn_ref[...], wo_ref[LRU_WIDTH:, :], preferred_element_type=F32))
    x1 = x_ref[...] + gate1 * mix
    h2 = (_rms_norm(x1, g2_ref[...]) * (1.0 + scale2) + shift2).astype(BF16)

    ff = jnp.zeros((tm, D_MODEL), F32)
    for c in range(D_FF // FFN_CK):
        lo = c * FFN_CK
        a = jnp.dot(h2, wup_ref[:, lo:lo + FFN_CK], preferred_element_type=F32)
        g = jnp.dot(h2, wup_ref[:, D_FF + lo:D_FF + lo + FFN_CK], preferred_element_type=F32)
        prev8 = tail_sc[:, lo:lo + FFN_CK]
        ac = cb_ref[:, lo:lo + FFN_CK]
        for j in range(CONV_FFN):
            sh = CONV_FFN - 1 - j
            a_s = a if sh == 0 else _shift_rows(prev8, a, sh)
            ac = ac + a_s * cw_ref[j:j + 1, lo:lo + FFN_CK]
        tail_sc[:, lo:lo + FFN_CK] = a[tm - SUBLANES:]
        u = (_gelu_tanh(ac) * g).astype(BF16)
        ff = ff + jnp.dot(u, wdn_ref[lo:lo + FFN_CK, :], preferred_element_type=F32)

    x2 = x1 + gate2 * ff
    o_ref[...] = _rms_norm(x2, gf_ref[...])


def _out_ffn(x2d, lru, attn, mod, w_out, g2, w_up, conv_w, conv_b, w_down, g_final):
    s = x2d.shape[0]
    tm = min(FFN_TM, s)
    row = lambda w: pl.BlockSpec((tm, w), lambda i: (i, 0))
    const = lambda shape: pl.BlockSpec(shape, lambda i: (0, 0), pipeline_mode=pl.Buffered(1))
    return pl.pallas_call(
        functools.partial(_ffn_kernel, tm=tm),
        grid=(s // tm,),
        in_specs=[row(D_MODEL), row(LRU_WIDTH), row(ATTN_WIDTH),
                  const((1, 6 * D_MODEL)),
                  const((D_MODEL, D_MODEL)), const((1, D_MODEL)),
                  const((D_MODEL, 2 * D_FF)),
                  const((CONV_FFN, D_FF)), const((1, D_FF)),
                  const((D_FF, D_MODEL)), const((1, D_MODEL))],
        out_specs=row(D_MODEL),
        out_shape=jax.ShapeDtypeStruct((s, D_MODEL), F32),
        scratch_shapes=[pltpu.VMEM((SUBLANES, D_FF), F32)],
        compiler_params=pltpu.CompilerParams(dimension_semantics=("arbitrary",),
                                             vmem_limit_bytes=VMEM_LIMIT),
        name="out_ffn",
    )(x2d, lru, attn, mod, w_out, g2, w_up, conv_w, conv_b, w_down, g_final)


def _block_diag(w):
    nb, bs, _ = w.shape
    eye = jnp.eye(nb, dtype=w.dtype)
    return (w[:, :, None, :] * eye[:, None, :, None]).reshape(nb * bs, nb * bs)


def kernel(x, c, w_ada, b_ada, g_norm1, w_in, conv_lru_w, conv_lru_b, lru_wa, lru_ba, lru_wx, lru_bx, lru_lambda,
           lam_q1, lam_k1, lam_q2, lam_k2, g_subln, w_out, g_norm2, w_up, conv_ffn_w, conv_ffn_b, w_down, rel_bias,
           g_final):
    b, s, d = x.shape
    assert b == 1 and d == D_MODEL and w_ada.shape[0] == 1
    x2d = x.reshape(s, d)
    row = lambda a: a.reshape(1, -1)

    mod = _adaln_mod(c.reshape(d, 1), w_ada[0], row(b_ada[0]))

    tq, tk = min(ATT_TQ, s), min(ATT_TK, s)
    q, kt, v, xr, yg = _in_proj(x2d, mod, row(g_norm1[0]), w_in[0].astype(BF16), tk)

    bias = _bias_tiles(rel_bias.T.reshape(-1), tq, tk)
    lamv = jnp.stack([lam_q1[0], lam_k1[0], lam_q2[0], lam_k2[0]])
    attn = _diff_attn(q, kt, v, bias, lamv, row(g_subln[0]), tq, tk)

    w_gates = jnp.concatenate([_block_diag(lru_wa[0]), _block_diag(lru_wx[0])], axis=1).astype(BF16)
    b_gates = jnp.concatenate([lru_ba[0], lru_bx[0]]).reshape(1, -1)
    lru = _rg_lru(xr, yg, conv_lru_w[0], row(conv_lru_b[0]), w_gates, b_gates, row(lru_lambda[0]))

    out = _out_ffn(x2d, lru, attn, mod, w_out[0].astype(BF16), row(g_norm2[0]), w_up[0].astype(BF16),
                   conv_ffn_w[0], row(conv_ffn_b[0]), w_down[0].astype(BF16), row(g_final))
    return out.reshape(b, s, d)
```

```python
import functools
import math

import jax
import jax.numpy as jnp
from jax import lax
from jax.experimental import pallas as pl
from jax.experimental.pallas import tpu as pltpu

F32 = jnp.float32
BF16 = jnp.bfloat16

D_MODEL = 1024
LRU_WIDTH = 512
LRU_BLOCKS = 8
LRU_BLOCK = LRU_WIDTH // LRU_BLOCKS
CONV_LRU = 4
LRU_C = 8.0
N_HEADS = 4
HEAD_DIM = 64
V_DIM = 2 * HEAD_DIM
QK_WIDTH = N_HEADS * 2 * HEAD_DIM
ATTN_WIDTH = N_HEADS * V_DIM
D_IN = 2 * QK_WIDTH + ATTN_WIDTH + 2 * LRU_WIDTH
D_FF = 3 * D_MODEL
CONV_FFN = 3
NUM_BUCKETS = 32
MAX_EXACT = NUM_BUCKETS // 2
MAX_DISTANCE = 128
EPS = 1e-6
NEG_INF = -1e30
LAMBDA_INIT = 0.8 - 0.6 * math.exp(-0.3 * 0)
LOG2E = math.log2(math.e)

LANES = 128
SUBLANES = 8
VMEM_LIMIT = 56 * 1024 * 1024

MOD_TN = 1536
ATT_TQ = 512
ATT_TK = 512
LRU_TC = 256
FFN_TM = 512
FFN_CK = 1536


def _rms_norm(x, g):
    y = x * lax.rsqrt(jnp.mean(x * x, axis=-1, keepdims=True) + EPS)
    return y * g


def _gelu_tanh(x):
    cdf = 0.5 * (1.0 + jnp.tanh(math.sqrt(2.0 / math.pi) * (x + 0.044715 * (x * x * x))))
    return x * cdf


def _shift_rows(prev8, x, s):
    ext = jnp.concatenate([prev8, x], axis=0)
    return pltpu.roll(ext, s, 0)[SUBLANES:]


def _adaln_kernel(c_ref, w_ref, b_ref, o_ref):
    c = c_ref[...]
    cond = c * jax.nn.sigmoid(c)
    o_ref[...] = jnp.sum(cond * w_ref[...], axis=0, keepdims=True) + b_ref[...]


def _adaln_mod(c_col, w_ada, b_ada):
    d, n = w_ada.shape
    return pl.pallas_call(
        _adaln_kernel,
        grid=(n // MOD_TN,),
        in_specs=[pl.BlockSpec((d, 1), lambda j: (0, 0)),
                  pl.BlockSpec((d, MOD_TN), lambda j: (0, j)),
                  pl.BlockSpec((1, MOD_TN), lambda j: (0, j))],
        out_specs=pl.BlockSpec((1, MOD_TN), lambda j: (0, j)),
        out_shape=jax.ShapeDtypeStruct((1, n), F32),
        compiler_params=pltpu.CompilerParams(vmem_limit_bytes=VMEM_LIMIT),
        name="adaln_mod",
    )(c_col, w_ada, b_ada)


def _in_proj_kernel(x_ref, mod_ref, g_ref, w_ref, q_ref, k_ref, v_ref, xr_ref, yg_ref):
    x = x_ref[...]
    shift1 = mod_ref[:, 0:D_MODEL]
    scale1 = mod_ref[:, D_MODEL:2 * D_MODEL]
    h = _rms_norm(x, g_ref[...]) * (1.0 + scale1) + shift1
    proj = jnp.dot(h.astype(BF16), w_ref[...], preferred_element_type=F32)
    ones = jnp.ones((x.shape[0], V_DIM), BF16)
    for hd in range(N_HEADS):
        lo = hd * V_DIM
        q_ref[hd] = (proj[:, lo:lo + V_DIM] * (HEAD_DIM ** -0.5 * LOG2E)).astype(BF16)
        k_ref[hd, 0] = proj[:, QK_WIDTH + lo:QK_WIDTH + lo + V_DIM].T.astype(BF16)
        v_ref[hd] = jnp.concatenate(
            [proj[:, 2 * QK_WIDTH + lo:2 * QK_WIDTH + lo + V_DIM].astype(BF16), ones], axis=1)
    base = 2 * QK_WIDTH + ATTN_WIDTH
    xr_ref[...] = proj[:, base:base + LRU_WIDTH]
    yg_ref[...] = proj[:, base + LRU_WIDTH:base + 2 * LRU_WIDTH]


def _in_proj(x2d, mod, g1, w_in_bf16, tm):
    s = x2d.shape[0]
    head_spec = pl.BlockSpec((N_HEADS, tm, V_DIM), lambda i: (0, i, 0))
    row_spec = pl.BlockSpec((tm, LRU_WIDTH), lambda i: (i, 0))
    return pl.pallas_call(
        _in_proj_kernel,
        grid=(s // tm,),
        in_specs=[pl.BlockSpec((tm, D_MODEL), lambda i: (i, 0)),
                  pl.BlockSpec((1, 6 * D_MODEL), lambda i: (0, 0)),
                  pl.BlockSpec((1, D_MODEL), lambda i: (0, 0)),
                  pl.BlockSpec((D_MODEL, D_IN), lambda i: (0, 0))],
        out_specs=[head_spec, pl.BlockSpec((N_HEADS, 1, V_DIM, tm), lambda i: (0, i, 0, 0)),
                   pl.BlockSpec((N_HEADS, tm, 2 * V_DIM), lambda i: (0, i, 0)), row_spec, row_spec],
        out_shape=[jax.ShapeDtypeStruct((N_HEADS, s, V_DIM), BF16),
                   jax.ShapeDtypeStruct((N_HEADS, s // tm, V_DIM, tm), BF16),
                   jax.ShapeDtypeStruct((N_HEADS, s, 2 * V_DIM), BF16)]
        + [jax.ShapeDtypeStruct((s, LRU_WIDTH), F32)] * 2,
        compiler_params=pltpu.CompilerParams(dimension_semantics=("parallel",),
                                             vmem_limit_bytes=VMEM_LIMIT),
        name="in_proj",
    )(x2d, mod, g1, w_in_bf16)


def _bias_tiles_kernel(table_ref, o_ref, *, tq, tk, nd):
    hd = pl.program_id(0)
    blk = MAX_DISTANCE
    qpos = lax.broadcasted_iota(jnp.int32, (blk, blk), 0)
    kpos = lax.broadcasted_iota(jnp.int32, (blk, blk), 1)
    far = table_ref[hd * NUM_BUCKETS + NUM_BUCKETS - 1]

    def band_block(offset):
        rel = offset * blk + qpos - kpos
        n = jnp.maximum(rel, 0)
        nf = jnp.maximum(n, 1).astype(F32)
        large = MAX_EXACT + (jnp.log(nf / MAX_EXACT) / math.log(MAX_DISTANCE / MAX_EXACT)
                             * (NUM_BUCKETS - MAX_EXACT)).astype(jnp.int32)
        large = jnp.minimum(large, NUM_BUCKETS - 1)
        bucket = jnp.where(n < MAX_EXACT, n, large)
        val = jnp.zeros((blk, blk), F32)
        for b in range(NUM_BUCKETS):
            val = jnp.where(bucket == b, (table_ref[hd * NUM_BUCKETS + b] - far) * LOG2E, val)
        return jnp.where(rel >= 0, val, NEG_INF)

    blocks = {0: band_block(0), 1: band_block(1)}
    zeros = jnp.zeros((blk, blk), F32)
    neg = jnp.full((blk, blk), NEG_INF, F32)
    for dd in range(nd):
        for a in range(tq // blk):
            for b in range(tk // blk):
                off = ((dd + 1) * tk - tq) // blk + a - b
                o_ref[0, dd, a * blk:(a + 1) * blk, b * blk:(b + 1) * blk] = (
                    neg if off < 0 else blocks.get(off, zeros))
    o_ref[0, nd] = jnp.zeros((tq, tk), F32)
    o_ref[0, nd + 1] = jnp.full((tq, tk), NEG_INF, F32)


def _bias_tiles(table_flat, tq, tk):
    assert tq % tk == 0 and tk % MAX_DISTANCE == 0
    nd = tq // tk + 1
    return pl.pallas_call(
        functools.partial(_bias_tiles_kernel, tq=tq, tk=tk, nd=nd),
        grid=(N_HEADS,),
        in_specs=[pl.BlockSpec(memory_space=pltpu.SMEM)],
        out_specs=pl.BlockSpec((1, nd + 2, tq, tk), lambda h: (h, 0, 0, 0)),
        out_shape=jax.ShapeDtypeStruct((N_HEADS, nd + 2, tq, tk), F32),
        name="bias_tiles",
    )(table_flat)


def _attn_kernel(q_ref, qn_ref, k_ref, v_ref, bias_ref, lamv_ref, gs_ref, o_ref, qs_sc, sa_sc, sb_sc, pa_sc, pb_sc,
                 m_sc, acc_sc, *, tq, tk, nd):
    i = pl.program_id(1)

    def stack_maps(q):
        lane = lax.broadcasted_iota(jnp.int32, q.shape, 1)
        zero = jnp.zeros_like(q)
        return jnp.concatenate([jnp.where(lane < HEAD_DIM, q, zero), jnp.where(lane >= HEAD_DIM, q, zero)], axis=0)

    qs_sc[0] = stack_maps(q_ref[0])
    qs_sc[1] = stack_maps(qn_ref[0])

    def limits(qi):
        q_start = qi * tq
        n_far = jnp.maximum(q_start - (MAX_DISTANCE - 1), 0) // tk
        last = (q_start + tq - 1) // tk
        return q_start, n_far, last

    _, _, last = limits(i)
    n_pairs = (last + 2) // 2

    def scores(slot, qi, j, s_ref, part_ref):
        q_start, n_far, last_q = limits(qi)
        jc = jnp.minimum(j, last_q)
        s = jnp.dot(qs_sc[slot], k_ref[0, jc], preferred_element_type=F32)
        dd = (q_start + tq - (jc + 1) * tk) // tk
        idx = jnp.where(j > last_q, nd + 1, jnp.where(j < n_far, nd, dd))
        s = (s.reshape(2, tq, tk) + bias_ref[0, idx][None]).reshape(2 * tq, tk)
        s_ref[...] = s
        part_ref[...] = functools.reduce(
            jnp.maximum, [s[:, c * LANES:(c + 1) * LANES] for c in range(tk // LANES)])

    def accumulate(s_ref, part_ref, j):
        jc = jnp.minimum(j, last)
        vb = v_ref[0, pl.ds(pl.multiple_of(jc * tk, tk), tk), :]
        m_prev = m_sc[...]
        m_new = jnp.maximum(m_prev, jnp.max(part_ref[...], axis=1, keepdims=True))
        alpha = jnp.exp2(m_prev - m_new)
        p = jnp.exp2(s_ref[...] - jnp.concatenate([m_new] * (tk // LANES), axis=1))
        acc_sc[...] = (jnp.concatenate([alpha, alpha], axis=1) * acc_sc[...]
                       + jnp.dot(p.astype(BF16), vb, preferred_element_type=F32))
        m_sc[...] = m_new

    @pl.when(i == 0)
    def _():
        scores(0, i, 0, sa_sc, pa_sc)

    m_sc[...] = jnp.full(m_sc.shape, NEG_INF, F32)
    acc_sc[...] = jnp.zeros(acc_sc.shape, F32)

    def pair_body(t, carry):
        j = 2 * t
        scores(0, i, j + 1, sb_sc, pb_sc)
        accumulate(sa_sc, pa_sc, j)
        to_next = (j + 2 > last).astype(jnp.int32)
        scores(to_next, i + to_next, (1 - to_next) * (j + 2), sa_sc, pa_sc)
        accumulate(sb_sc, pb_sc, j + 1)
        return carry

    lax.fori_loop(0, n_pairs, pair_body, 0)

    lv = lamv_ref[...]
    d1 = jnp.sum(lv[0:1] * lv[1:2], axis=1, keepdims=True)
    d2 = jnp.sum(lv[2:3] * lv[3:4], axis=1, keepdims=True)
    lam = jnp.exp(d1) - jnp.exp(d2) + LAMBDA_INIT

    acc = acc_sc[...]
    out = acc[:, :V_DIM] / acc[:, V_DIM:]
    diff = out[:tq] - lam * out[tq:]
    o_ref[...] = (_rms_norm(diff, gs_ref[...]) * (1.0 - LAMBDA_INIT)).astype(o_ref.dtype)


def _diff_attn(q, kt, v_aug, bias, lamv, g_subln, tq, tk):
    s = q.shape[1]
    nq = s // tq
    nd = bias.shape[1] - 2
    return pl.pallas_call(
        functools.partial(_attn_kernel, tq=tq, tk=tk, nd=nd),
        grid=(N_HEADS, nq),
        in_specs=[pl.BlockSpec((1, tq, V_DIM), lambda h, i: (h, i, 0)),
                  pl.BlockSpec((1, tq, V_DIM), lambda h, i: (h, jnp.minimum(i + 1, nq - 1), 0)),
                  pl.BlockSpec((1, s // tk, V_DIM, tk), lambda h, i: (h, 0, 0, 0)),
                  pl.BlockSpec((1, s, 2 * V_DIM), lambda h, i: (h, 0, 0)),
                  pl.BlockSpec((1, nd + 2, tq, tk), lambda h, i: (h, 0, 0, 0), pipeline_mode=pl.Buffered(1)),
                  pl.BlockSpec((4, HEAD_DIM), lambda h, i: (0, 0)),
                  pl.BlockSpec((1, V_DIM), lambda h, i: (0, 0))],
        out_specs=pl.BlockSpec((tq, V_DIM), lambda h, i: (i, h)),
        out_shape=jax.ShapeDtypeStruct((s, ATTN_WIDTH), BF16),
        scratch_shapes=[pltpu.VMEM((2, 2 * tq, V_DIM), BF16),
                        pltpu.VMEM((2 * tq, tk), F32),
                        pltpu.VMEM((2 * tq, tk), F32),
                        pltpu.VMEM((2 * tq, LANES), F32),
                        pltpu.VMEM((2 * tq, LANES), F32),
                        pltpu.VMEM((2 * tq, LANES), F32),
                        pltpu.VMEM((2 * tq, 2 * V_DIM), F32)],
        compiler_params=pltpu.CompilerParams(dimension_semantics=("arbitrary", "arbitrary"),
                                             vmem_limit_bytes=VMEM_LIMIT),
        name="diff_attn",
    )(q, q, kt, v_aug, bias, lamv, g_subln)


def _lru_kernel(xr_ref, yg_ref, cw_ref, cb_ref, wg_ref, bg_ref, lam_ref, o_ref, tail_sc, h_sc, *, tc):
    @pl.when(pl.program_id(0) == 0)
    def _():
        tail_sc[...] = jnp.zeros(tail_sc.shape, F32)
        h_sc[...] = jnp.zeros(h_sc.shape, F32)

    x = xr_ref[...]
    prev8 = tail_sc[...]
    cw = cw_ref[...]
    xc = cb_ref[...]
    for j in range(CONV_LRU):
        sh = CONV_LRU - 1 - j
        xs = x if sh == 0 else _shift_rows(prev8, x, sh)
        xc = xc + xs * cw[j:j + 1]
    tail_sc[...] = x[tc - SUBLANES:]

    gates = jnp.dot(xc.astype(BF16), wg_ref[...], preferred_element_type=F32) + bg_ref[...]
    sig = 0.5 * jnp.tanh(0.5 * gates) + 0.5
    r = sig[:, :LRU_WIDTH]
    ig = sig[:, LRU_WIDTH:]
    z = -lam_ref[...]
    softplus = jnp.maximum(z, 0.0) + jnp.log1p(jnp.exp(-jnp.abs(z)))
    log_a = -LRU_C * r * softplus
    a = jnp.exp(log_a)
    th = jnp.tanh(log_a)
    b = jnp.sqrt(-2.0 * th / (1.0 - th)) * (ig * xc)

    ng = tc // SUBLANES
    a = a.reshape(ng, SUBLANES, LRU_WIDTH)
    b = b.reshape(ng, SUBLANES, LRU_WIDTH)
    row = lax.broadcasted_iota(jnp.int32, a.shape, 1)
    sh = 1
    while sh < SUBLANES:
        valid = row >= sh
        a_s = jnp.where(valid, pltpu.roll(a, sh, 1), 1.0)
        b_s = jnp.where(valid, pltpu.roll(b, sh, 1), 0.0)
        b = a * b_s + b
        a = a * a_s
        sh *= 2
    carry = h_sc[0:1]
    hs = []
    for g in range(ng):
        hg = a[g] * carry + b[g]
        hs.append(hg)
        carry = hg[SUBLANES - 1:SUBLANES]
    h = jnp.concatenate(hs, axis=0)
    h_sc[0:1] = carry
    o_ref[...] = (h * _gelu_tanh(yg_ref[...])).astype(o_ref.dtype)


def _rg_lru(xr, yg, conv_w, conv_b, w_gates_bf16, b_gates, lam):
    s = xr.shape[0]
    tc = min(LRU_TC, s)
    row_spec = pl.BlockSpec((tc, LRU_WIDTH), lambda i: (i, 0))
    const = lambda shape: pl.BlockSpec(shape, lambda i: (0, 0))
    return pl.pallas_call(
        functools.partial(_lru_kernel, tc=tc),
        grid=(s // tc,),
        in_specs=[row_spec, row_spec,
                  const((CONV_LRU, LRU_WIDTH)), const((1, LRU_WIDTH)),
                  const((LRU_WIDTH, 2 * LRU_WIDTH)), const((1, 2 * LRU_WIDTH)),
                  const((1, LRU_WIDTH))],
        out_specs=row_spec,
        out_shape=jax.ShapeDtypeStruct((s, LRU_WIDTH), BF16),
        scratch_shapes=[pltpu.VMEM((SUBLANES, LRU_WIDTH), F32),
                        pltpu.VMEM((SUBLANES, LRU_WIDTH), F32)],
        compiler_params=pltpu.CompilerParams(dimension_semantics=("arbitrary",),
                                             vmem_limit_bytes=VMEM_LIMIT),
        name="rg_lru",
    )(xr, yg, conv_w, conv_b, w_gates_bf16, b_gates, lam)


def _ffn_kernel(x_ref, lru_ref, attn_ref, mod_ref, wo_ref, g2_ref, wup_ref, cw_ref, cb_ref, wdn_ref, gf_ref,
                o_ref, tail_sc, *, tm):
    @pl.when(pl.program_id(0) == 0)
    def _():
        tail_sc[...] = jnp.zeros(tail_sc.shape, F32)

    gate1 = mod_ref[:, 2 * D_MODEL:3 * D_MODEL]
    shift2 = mod_ref[:, 3 * D_MODEL:4 * D_MODEL]
    scale2 = mod_ref[:, 4 * D_MODEL:5 * D_MODEL]
    gate2 = mod_ref[:, 5 * D_MODEL:6 * D_MODEL]

    mix = (jnp.dot(lru_ref[...], wo_ref[0:LRU_WIDTH, :], preferred_element_type=F32)
           + jnp.dot(attn_ref[...], wo_ref[LRU_WIDTH:, :], preferred_element_type=F32))
    x1 = x_ref[...] + gate1 * mix
    h2 = (_rms_norm(x1, g2_ref[...]) * (1.0 + scale2) + shift2).astype(BF16)

    ff = jnp.zeros((tm, D_MODEL), F32)
    for c in range(D_FF // FFN_CK):
        lo = c * FFN_CK
        a = jnp.dot(h2, wup_ref[:, lo:lo + FFN_CK], preferred_element_type=F32)
        g = jnp.dot(h2, wup_ref[:, D_FF + lo:D_FF + lo + FFN_CK], preferred_element_type=F32)
        prev8 = tail_sc[:, lo:lo + FFN_CK]
        ac = cb_ref[:, lo:lo + FFN_CK]
        for j in range(CONV_FFN):
            sh = CONV_FFN - 1 - j
            a_s = a if sh == 0 else _shift_rows(prev8, a, sh)
            ac = ac + a_s * cw_ref[j:j + 1, lo:lo + FFN_CK]
        tail_sc[:, lo:lo + FFN_CK] = a[tm - SUBLANES:]
        u = (_gelu_tanh(ac) * g).astype(BF16)
        ff = ff + jnp.dot(u, wdn_ref[lo:lo + FFN_CK, :], preferred_element_type=F32)

    x2 = x1 + gate2 * ff
    o_ref[...] = _rms_norm(x2, gf_ref[...])


def _out_ffn(x2d, lru, attn, mod, w_out, g2, w_up, conv_w, conv_b, w_down, g_final):
    s = x2d.shape[0]
    tm = min(FFN_TM, s)
    row = lambda w: pl.BlockSpec((tm, w), lambda i: (i, 0))
    const = lambda shape: pl.BlockSpec(shape, lambda i: (0, 0), pipeline_mode=pl.Buffered(1))
    return pl.pallas_call(
        functools.partial(_ffn_kernel, tm=tm),
        grid=(s // tm,),
        in_specs=[row(D_MODEL), row(LRU_WIDTH), row(ATTN_WIDTH),
                  const((1, 6 * D_MODEL)),
                  const((D_MODEL, D_MODEL)), const((1, D_MODEL)),
                  const((D_MODEL, 2 * D_FF)),
                  const((CONV_FFN, D_FF)), const((1, D_FF)),
                  const((D_FF, D_MODEL)), const((1, D_MODEL))],
        out_specs=row(D_MODEL),
        out_shape=jax.ShapeDtypeStruct((s, D_MODEL), F32),
        scratch_shapes=[pltpu.VMEM((SUBLANES, D_FF), F32)],
        compiler_params=pltpu.CompilerParams(dimension_semantics=("arbitrary",),
                                             vmem_limit_bytes=VMEM_LIMIT),
        name="out_ffn",
    )(x2d, lru, attn, mod, w_out, g2, w_up, conv_w, conv_b, w_down, g_final)


def _block_diag(w):
    nb, bs, _ = w.shape
    eye = jnp.eye(nb, dtype=w.dtype)
    return (w[:, :, None, :] * eye[:, None, :, None]).reshape(nb * bs, nb * bs)


def kernel(x, c, w_ada, b_ada, g_norm1, w_in, conv_lru_w, conv_lru_b, lru_wa, lru_ba, lru_wx, lru_bx, lru_lambda,
           lam_q1, lam_k1, lam_q2, lam_k2, g_subln, w_out, g_norm2, w_up, conv_ffn_w, conv_ffn_b, w_down, rel_bias,
           g_final):
    b, s, d = x.shape
    assert b == 1 and d == D_MODEL and w_ada.shape[0] == 1
    x2d = x.reshape(s, d)
    row = lambda a: a.reshape(1, -1)

    mod = _adaln_mod(c.reshape(d, 1), w_ada[0], row(b_ada[0]))

    tq, tk = min(ATT_TQ, s), min(ATT_TK, s)
    q, kt, v, xr, yg = _in_proj(x2d, mod, row(g_norm1[0]), w_in[0].astype(BF16), tk)

    bias = _bias_tiles(rel_bias.T.reshape(-1), tq, tk)
    lamv = jnp.stack([lam_q1[0], lam_k1[0], lam_q2[0], lam_k2[0]])
    attn = _diff_attn(q, kt, v, bias, lamv, row(g_subln[0]), tq, tk)

    w_gates = jnp.concatenate([_block_diag(lru_wa[0]), _block_diag(lru_wx[0])], axis=1).astype(BF16)
    b_gates = jnp.concatenate([lru_ba[0], lru_bx[0]]).reshape(1, -1)
    lru = _rg_lru(xr, yg, conv_lru_w[0], row(conv_lru_b[0]), w_gates, b_gates, row(lru_lambda[0]))

    out = _out_ffn(x2d, lru, attn, mod, w_out[0].astype(BF16), row(g_norm2[0]), w_up[0].astype(BF16),
                   conv_ffn_w[0], row(conv_ffn_b[0]), w_down[0].astype(BF16), row(g_final))
    return out.reshape(b, s, d)
```

```python
import functools
import math

import jax
import jax.numpy as jnp
from jax import lax
from jax.experimental import pallas as pl
from jax.experimental.pallas import tpu as pltpu

F32 = jnp.float32
BF16 = jnp.bfloat16

D_MODEL = 1024
LRU_WIDTH = 512
LRU_BLOCKS = 8
LRU_BLOCK = LRU_WIDTH // LRU_BLOCKS
CONV_LRU = 4
LRU_C = 8.0
N_HEADS = 4
HEAD_DIM = 64
V_DIM = 2 * HEAD_DIM
QK_WIDTH = N_HEADS * 2 * HEAD_DIM
ATTN_WIDTH = N_HEADS * V_DIM
D_IN = 2 * QK_WIDTH + ATTN_WIDTH + 2 * LRU_WIDTH
D_FF = 3 * D_MODEL
CONV_FFN = 3
NUM_BUCKETS = 32
MAX_EXACT = NUM_BUCKETS // 2
MAX_DISTANCE = 128
EPS = 1e-6
NEG_INF = -1e30
LAMBDA_INIT = 0.8 - 0.6 * math.exp(-0.3 * 0)
LOG2E = math.log2(math.e)

LANES = 128
SUBLANES = 8
VMEM_LIMIT = 56 * 1024 * 1024

MOD_TN = 1536
ATT_TQ = 512
ATT_TK = 512
LRU_TC = 256
FFN_TM = 512
FFN_CK = 1536


def _rms_norm(x, g):
    y = x * lax.rsqrt(jnp.mean(x * x, axis=-1, keepdims=True) + EPS)
    return y * g


def _gelu_tanh(x):
    cdf = 0.5 * (1.0 + jnp.tanh(math.sqrt(2.0 / math.pi) * (x + 0.044715 * (x * x * x))))
    return x * cdf


def _shift_rows(prev8, x, s):
    ext = jnp.concatenate([prev8, x], axis=0)
    return pltpu.roll(ext, s, 0)[SUBLANES:]


def _adaln_kernel(c_ref, w_ref, b_ref, o_ref):
    c = c_ref[...]
    cond = c * jax.nn.sigmoid(c)
    o_ref[...] = jnp.sum(cond * w_ref[...], axis=0, keepdims=True) + b_ref[...]


def _adaln_mod(c_col, w_ada, b_ada):
    d, n = w_ada.shape
    return pl.pallas_call(
        _adaln_kernel,
        grid=(n // MOD_TN,),
        in_specs=[pl.BlockSpec((d, 1), lambda j: (0, 0)),
                  pl.BlockSpec((d, MOD_TN), lambda j: (0, j)),
                  pl.BlockSpec((1, MOD_TN), lambda j: (0, j))],
        out_specs=pl.BlockSpec((1, MOD_TN), lambda j: (0, j)),
        out_shape=jax.ShapeDtypeStruct((1, n), F32),
        compiler_params=pltpu.CompilerParams(vmem_limit_bytes=VMEM_LIMIT),
        name="adaln_mod",
    )(c_col, w_ada, b_ada)


def _in_proj_kernel(x_ref, mod_ref, g_ref, w_ref, q_ref, k_ref, v_ref, xr_ref, yg_ref):
    x = x_ref[...]
    shift1 = mod_ref[:, 0:D_MODEL]
    scale1 = mod_ref[:, D_MODEL:2 * D_MODEL]
    h = _rms_norm(x, g_ref[...]) * (1.0 + scale1) + shift1
    proj = jnp.dot(h.astype(BF16), w_ref[...], preferred_element_type=F32)
    ones = jnp.ones((x.shape[0], V_DIM), BF16)
    for hd in range(N_HEADS):
        lo = hd * V_DIM
        q_ref[hd] = (proj[:, lo:lo + V_DIM] * (HEAD_DIM ** -0.5 * LOG2E)).astype(BF16)
        k_ref[hd, 0] = proj[:, QK_WIDTH + lo:QK_WIDTH + lo + V_DIM].T.astype(BF16)
        v_ref[hd] = jnp.concatenate(
            [proj[:, 2 * QK_WIDTH + lo:2 * QK_WIDTH + lo + V_DIM].astype(BF16), ones], axis=1)
    base = 2 * QK_WIDTH + ATTN_WIDTH
    xr_ref[...] = proj[:, base:base + LRU_WIDTH]
    yg_ref[...] = proj[:, base + LRU_WIDTH:base + 2 * LRU_WIDTH]


def _in_proj(x2d, mod, g1, w_in_bf16, tm):
    s = x2d.shape[0]
    head_spec = pl.BlockSpec((N_HEADS, tm, V_DIM), lambda i: (0, i, 0))
    row_spec = pl.BlockSpec((tm, LRU_WIDTH), lambda i: (i, 0))
    return pl.pallas_call(
        _in_proj_kernel,
        grid=(s // tm,),
        in_specs=[pl.BlockSpec((tm, D_MODEL), lambda i: (i, 0)),
                  pl.BlockSpec((1, 6 * D_MODEL), lambda i: (0, 0)),
                  pl.BlockSpec((1, D_MODEL), lambda i: (0, 0)),
                  pl.BlockSpec((D_MODEL, D_IN), lambda i: (0, 0))],
        out_specs=[head_spec, pl.BlockSpec((N_HEADS, 1, V_DIM, tm), lambda i: (0, i, 0, 0)),
                   pl.BlockSpec((N_HEADS, tm, 2 * V_DIM), lambda i: (0, i, 0)), row_spec, row_spec],
        out_shape=[jax.ShapeDtypeStruct((N_HEADS, s, V_DIM), BF16),
                   jax.ShapeDtypeStruct((N_HEADS, s // tm, V_DIM, tm), BF16),
                   jax.ShapeDtypeStruct((N_HEADS, s, 2 * V_DIM), BF16)]
        + [jax.ShapeDtypeStruct((s, LRU_WIDTH), F32)] * 2,
        compiler_params=pltpu.CompilerParams(dimension_semantics=("parallel",),
                                             vmem_limit_bytes=VMEM_LIMIT),
        name="in_proj",
    )(x2d, mod, g1, w_in_bf16)


def _bias_tiles_kernel(table_ref, o_ref, *, tq, tk, nd):
    hd = pl.program_id(0)
    blk = MAX_DISTANCE
    qpos = lax.broadcasted_iota(jnp.int32, (blk, blk), 0)
    kpos = lax.broadcasted_iota(jnp.int32, (blk, blk), 1)
    far = table_ref[hd * NUM_BUCKETS + NUM_BUCKETS - 1]

    def band_block(offset):
        rel = offset * blk + qpos - kpos
        n = jnp.maximum(rel, 0)
        nf = jnp.maximum(n, 1).astype(F32)
        y = (jnp.log(nf / MAX_EXACT) / math.log(MAX_DISTANCE / MAX_EXACT) * (NUM_BUCKETS - MAX_EXACT))
        val = jnp.zeros((blk, blk), F32)
        for b in range(NUM_BUCKETS):
            hit = (n == b) if b < MAX_EXACT else ((n >= MAX_EXACT) & (y >= b - MAX_EXACT))
            val = jnp.where(hit, (table_ref[hd * NUM_BUCKETS + b] - far) * LOG2E, val)
        return jnp.where(rel >= 0, val, NEG_INF)

    blocks = {0: band_block(0), 1: band_block(1)}
    zeros = jnp.zeros((blk, blk), F32)
    neg = jnp.full((blk, blk), NEG_INF, F32)
    for dd in range(nd):
        for a in range(tq // blk):
            for b in range(tk // blk):
                off = ((dd + 1) * tk - tq) // blk + a - b
                o_ref[0, dd, a * blk:(a + 1) * blk, b * blk:(b + 1) * blk] = (
                    neg if off < 0 else blocks.get(off, zeros))
    o_ref[0, nd] = jnp.zeros((tq, tk), F32)
    o_ref[0, nd + 1] = jnp.full((tq, tk), NEG_INF, F32)


def _bias_tiles(table_flat, tq, tk):
    assert tq % tk == 0 and tk % MAX_DISTANCE == 0
    nd = tq // tk + 1
    return pl.pallas_call(
        functools.partial(_bias_tiles_kernel, tq=tq, tk=tk, nd=nd),
        grid=(N_HEADS,),
        in_specs=[pl.BlockSpec(memory_space=pltpu.SMEM)],
        out_specs=pl.BlockSpec((1, nd + 2, tq, tk), lambda h: (h, 0, 0, 0)),
        out_shape=jax.ShapeDtypeStruct((N_HEADS, nd + 2, tq, tk), F32),
        name="bias_tiles",
    )(table_flat)


def _attn_kernel(q_ref, qn_ref, k_ref, v_ref, bias_ref, lamv_ref, gs_ref, o_ref, qs_sc, sa_sc, sb_sc, pa_sc, pb_sc,
                 m_sc, acc_sc, *, tq, tk, nd):
    i = pl.program_id(1)

    def stack_maps(q):
        lane = lax.broadcasted_iota(jnp.int32, q.shape, 1)
        zero = jnp.zeros_like(q)
        return jnp.concatenate([jnp.where(lane < HEAD_DIM, q, zero), jnp.where(lane >= HEAD_DIM, q, zero)], axis=0)

    qs_sc[0] = stack_maps(q_ref[0])
    qs_sc[1] = stack_maps(qn_ref[0])

    def limits(qi):
        q_start = qi * tq
        n_far = jnp.maximum(q_start - (MAX_DISTANCE - 1), 0) // tk
        last = (q_start + tq - 1) // tk
        return q_start, n_far, last

    _, _, last = limits(i)
    n_pairs = (last + 2) // 2

    def scores(slot, qi, j, s_ref, part_ref):
        q_start, n_far, last_q = limits(qi)
        jc = jnp.minimum(j, last_q)
        s = jnp.dot(qs_sc[slot], k_ref[0, jc], preferred_element_type=F32)
        dd = (q_start + tq - (jc + 1) * tk) // tk
        idx = jnp.where(j > last_q, nd + 1, jnp.where(j < n_far, nd, dd))
        s = (s.reshape(2, tq, tk) + bias_ref[0, idx][None]).reshape(2 * tq, tk)
        s_ref[...] = s
        part_ref[...] = functools.reduce(
            jnp.maximum, [s[:, c * LANES:(c + 1) * LANES] for c in range(tk // LANES)])

    def accumulate(s_ref, part_ref, j):
        jc = jnp.minimum(j, last)
        vb = v_ref[0, pl.ds(pl.multiple_of(jc * tk, tk), tk), :]
        m_prev = m_sc[...]
        m_new = jnp.maximum(m_prev, jnp.max(part_ref[...], axis=1, keepdims=True))
        alpha = jnp.exp2(m_prev - m_new)
        p = jnp.exp2(s_ref[...] - jnp.concatenate([m_new] * (tk // LANES), axis=1))
        acc_sc[...] = (jnp.concatenate([alpha, alpha], axis=1) * acc_sc[...]
                       + jnp.dot(p.astype(BF16), vb, preferred_element_type=F32))
        m_sc[...] = m_new

    @pl.when(i == 0)
    def _():
        scores(0, i, 0, sa_sc, pa_sc)

    m_sc[...] = jnp.full(m_sc.shape, NEG_INF, F32)
    acc_sc[...] = jnp.zeros(acc_sc.shape, F32)

    def two_blocks(j):
        scores(0, i, j + 1, sb_sc, pb_sc)
        accumulate(sa_sc, pa_sc, j)
        to_next = (j + 2 > last).astype(jnp.int32)
        scores(to_next, i + to_next, (1 - to_next) * (j + 2), sa_sc, pa_sc)
        accumulate(sb_sc, pb_sc, j + 1)

    def quad_body(t, carry):
        two_blocks(4 * t)
        two_blocks(4 * t + 2)
        return carry

    def pair_body(t, carry):
        two_blocks(4 * n_quads + 2 * t)
        return carry

    n_quads = (last + 1) // 4
    lax.fori_loop(0, n_quads, quad_body, 0)
    lax.fori_loop(0, n_pairs - 2 * n_quads, pair_body, 0)

    lv = lamv_ref[...]
    d1 = jnp.sum(lv[0:1] * lv[1:2], axis=1, keepdims=True)
    d2 = jnp.sum(lv[2:3] * lv[3:4], axis=1, keepdims=True)
    lam = jnp.exp(d1) - jnp.exp(d2) + LAMBDA_INIT

    acc = acc_sc[...]
    out = acc[:, :V_DIM] / acc[:, V_DIM:]
    diff = out[:tq] - lam * out[tq:]
    o_ref[...] = (_rms_norm(diff, gs_ref[...]) * (1.0 - LAMBDA_INIT)).astype(o_ref.dtype)


def _diff_attn(q, kt, v_aug, bias, lamv, g_subln, tq, tk):
    s = q.shape[1]
    nq = s // tq
    nd = bias.shape[1] - 2
    return pl.pallas_call(
        functools.partial(_attn_kernel, tq=tq, tk=tk, nd=nd),
        grid=(N_HEADS, nq),
        in_specs=[pl.BlockSpec((1, tq, V_DIM), lambda h, i: (h, i, 0)),
                  pl.BlockSpec((1, tq, V_DIM), lambda h, i: (h, jnp.minimum(i + 1, nq - 1), 0)),
                  pl.BlockSpec((1, s // tk, V_DIM, tk), lambda h, i: (h, 0, 0, 0)),
                  pl.BlockSpec((1, s, 2 * V_DIM), lambda h, i: (h, 0, 0)),
                  pl.BlockSpec((1, nd + 2, tq, tk), lambda h, i: (h, 0, 0, 0), pipeline_mode=pl.Buffered(1)),
                  pl.BlockSpec((4, HEAD_DIM), lambda h, i: (0, 0)),
                  pl.BlockSpec((1, V_DIM), lambda h, i: (0, 0))],
        out_specs=pl.BlockSpec((tq, V_DIM), lambda h, i: (i, h)),
        out_shape=jax.ShapeDtypeStruct((s, ATTN_WIDTH), BF16),
        scratch_shapes=[pltpu.VMEM((2, 2 * tq, V_DIM), BF16),
                        pltpu.VMEM((2 * tq, tk), F32),
                        pltpu.VMEM((2 * tq, tk), F32),
                        pltpu.VMEM((2 * tq, LANES), F32),
                        pltpu.VMEM((2 * tq, LANES), F32),
                        pltpu.VMEM((2 * tq, LANES), F32),
                        pltpu.VMEM((2 * tq, 2 * V_DIM), F32)],
        compiler_params=pltpu.CompilerParams(dimension_semantics=("arbitrary", "arbitrary"),
                                             vmem_limit_bytes=VMEM_LIMIT),
        name="diff_attn",
    )(q, q, kt, v_aug, bias, lamv, g_subln)


def _lru_kernel(xr_ref, yg_ref, cw_ref, cb_ref, wg_ref, bg_ref, lam_ref, o_ref, tail_sc, h_sc, *, tc):
    @pl.when(pl.program_id(0) == 0)
    def _():
        tail_sc[...] = jnp.zeros(tail_sc.shape, F32)
        h_sc[...] = jnp.zeros(h_sc.shape, F32)

    x = xr_ref[...]
    prev8 = tail_sc[...]
    cw = cw_ref[...]
    xc = cb_ref[...]
    for j in range(CONV_LRU):
        sh = CONV_LRU - 1 - j
        xs = x if sh == 0 else _shift_rows(prev8, x, sh)
        xc = xc + xs * cw[j:j + 1]
    tail_sc[...] = x[tc - SUBLANES:]

    gates = jnp.dot(xc.astype(BF16), wg_ref[...], preferred_element_type=F32) + bg_ref[...]
    sig = 0.5 * jnp.tanh(0.5 * gates) + 0.5
    r = sig[:, :LRU_WIDTH]
    ig = sig[:, LRU_WIDTH:]
    z = -lam_ref[...]
    softplus = jnp.maximum(z, 0.0) + jnp.log1p(jnp.exp(-jnp.abs(z)))
    log_a = -LRU_C * r * softplus
    a = jnp.exp(log_a)
    th = jnp.tanh(log_a)
    b = jnp.sqrt(-2.0 * th / (1.0 - th)) * (ig * xc)

    ng = tc // SUBLANES
    a = a.reshape(ng, SUBLANES, LRU_WIDTH)
    b = b.reshape(ng, SUBLANES, LRU_WIDTH)
    row = lax.broadcasted_iota(jnp.int32, a.shape, 1)
    sh = 1
    while sh < SUBLANES:
        valid = row >= sh
        a_s = jnp.where(valid, pltpu.roll(a, sh, 1), 1.0)
        b_s = jnp.where(valid, pltpu.roll(b, sh, 1), 0.0)
        b = a * b_s + b
        a = a * a_s
        sh *= 2
    carry = h_sc[0:1]
    hs = []
    for g in range(ng):
        hg = a[g] * carry + b[g]
        hs.append(hg)
        carry = hg[SUBLANES - 1:SUBLANES]
    h = jnp.concatenate(hs, axis=0)
    h_sc[0:1] = carry
    o_ref[...] = (h * _gelu_tanh(yg_ref[...])).astype(o_ref.dtype)


def _rg_lru(xr, yg, conv_w, conv_b, w_gates_bf16, b_gates, lam):
    s = xr.shape[0]
    tc = min(LRU_TC, s)
    row_spec = pl.BlockSpec((tc, LRU_WIDTH), lambda i: (i, 0))
    const = lambda shape: pl.BlockSpec(shape, lambda i: (0, 0))
    return pl.pallas_call(
        functools.partial(_lru_kernel, tc=tc),
        grid=(s // tc,),
        in_specs=[row_spec, row_spec,
                  const((CONV_LRU, LRU_WIDTH)), const((1, LRU_WIDTH)),
                  const((LRU_WIDTH, 2 * LRU_WIDTH)), const((1, 2 * LRU_WIDTH)),
                  const((1, LRU_WIDTH))],
        out_specs=row_spec,
        out_shape=jax.ShapeDtypeStruct((s, LRU_WIDTH), BF16),
        scratch_shapes=[pltpu.VMEM((SUBLANES, LRU_WIDTH), F32),
                        pltpu.VMEM((SUBLANES, LRU_WIDTH), F32)],
        compiler_params=pltpu.CompilerParams(dimension_semantics=("arbitrary",),
                                             vmem_limit_bytes=VMEM_LIMIT),
        name="rg_lru",
    )(xr, yg, conv_w, conv_b, w_gates_bf16, b_gates, lam)


def _ffn_kernel(x_ref, lru_ref, attn_ref, mod_ref, wo_ref, g2_ref, wup_ref, cw_ref, cb_ref, wdn_ref, gf_ref,
                o_ref, tail_sc, *, tm):
    @pl.when(pl.program_id(0) == 0)
    def _():
        tail_sc[...] = jnp.zeros(tail_sc.shape, F32)

    gate1 = mod_ref[:, 2 * D_MODEL:3 * D_MODEL]
    shift2 = mod_ref[:, 3 * D_MODEL:4 * D_MODEL]
    scale2 = mod_ref[:, 4 * D_MODEL:5 * D_MODEL]
    gate2 = mod_ref[:, 5 * D_MODEL:6 * D_MODEL]

    mix = (jnp.dot(lru_ref[...], wo_ref[0:LRU_WIDTH, :], preferred_element_type=F32)
           + jnp.dot(attn_ref[...], wo_ref[LRU_WIDTH:, :], preferred_element_type=F32))
    x1 = x_ref[...] + gate1 * mix
    h2 = (_rms_norm(x1, g2_ref[...]) * (1.0 + scale2) + shift2).astype(BF16)

    ff = jnp.zeros((tm, D_MODEL), F32)
    for c in range(D_FF // FFN_CK):
        lo = c * FFN_CK
        a = jnp.dot(h2, wup_ref[:, lo:lo + FFN_CK], preferred_element_type=F32)
        g = jnp.dot(h2, wup_ref[:, D_FF + lo:D_FF + lo + FFN_CK], preferred_element_type=F32)
        prev8 = tail_sc[:, lo:lo + FFN_CK]
        ac = cb_ref[:, lo:lo + FFN_CK]
        for j in range(CONV_FFN):
            sh = CONV_FFN - 1 - j
            a_s = a if sh == 0 else _shift_rows(prev8, a, sh)
            ac = ac + a_s * cw_ref[j:j + 1, lo:lo + FFN_CK]
        tail_sc[:, lo:lo + FFN_CK] = a[tm - SUBLANES:]
        u = (_gelu_tanh(ac) * g).astype(BF16)
        ff = ff + jnp.dot(u, wdn_ref[lo:lo + FFN_CK, :], preferred_element_type=F32)

    x2 = x1 + gate2 * ff
    o_ref[...] = _rms_norm(x2, gf_ref[...])


def _out_ffn(x2d, lru, attn, mod, w_out, g2, w_up, conv_w, conv_b, w_down, g_final):
    s = x2d.shape[0]
    tm = min(FFN_TM, s)
    row = lambda w: pl.BlockSpec((tm, w), lambda i: (i, 0))
    const = lambda shape: pl.BlockSpec(shape, lambda i: (0, 0), pipeline_mode=pl.Buffered(1))
    return pl.pallas_call(
        functools.partial(_ffn_kernel, tm=tm),
        grid=(s // tm,),
        in_specs=[row(D_MODEL), row(LRU_WIDTH), row(ATTN_WIDTH),
                  const((1, 6 * D_MODEL)),
                  const((D_MODEL, D_MODEL)), const((1, D_MODEL)),
                  const((D_MODEL, 2 * D_FF)),
                  const((CONV_FFN, D_FF)), const((1, D_FF)),
                  const((D_FF, D_MODEL)), const((1, D_MODEL))],
        out_specs=row(D_MODEL),
        out_shape=jax.ShapeDtypeStruct((s, D_MODEL), F32),
        scratch_shapes=[pltpu.VMEM((SUBLANES, D_FF), F32)],
        compiler_params=pltpu.CompilerParams(dimension_semantics=("arbitrary",),
                                             vmem_limit_bytes=VMEM_LIMIT),
        name="out_ffn",
    )(x2d, lru, attn, mod, w_out, g2, w_up, conv_w, conv_b, w_down, g_final)


def _block_diag(w):
    nb, bs, _ = w.shape
    eye = jnp.eye(nb, dtype=w.dtype)
    return (w[:, :, None, :] * eye[:, None, :, None]).reshape(nb * bs, nb * bs)


def kernel(x, c, w_ada, b_ada, g_norm1, w_in, conv_lru_w, conv_lru_b, lru_wa, lru_ba, lru_wx, lru_bx, lru_lambda,
           lam_q1, lam_k1, lam_q2, lam_k2, g_subln, w_out, g_norm2, w_up, conv_ffn_w, conv_ffn_b, w_down, rel_bias,
           g_final):
    b, s, d = x.shape
    assert b == 1 and d == D_MODEL and w_ada.shape[0] == 1
    x2d = x.reshape(s, d)
    row = lambda a: a.reshape(1, -1)

    mod = _adaln_mod(c.reshape(d, 1), w_ada[0], row(b_ada[0]))

    tq, tk = min(ATT_TQ, s), min(ATT_TK, s)
    q, kt, v, xr, yg = _in_proj(x2d, mod, row(g_norm1[0]), w_in[0].astype(BF16), tk)

    bias = _bias_tiles(rel_bias.T.reshape(-1), tq, tk)
    lamv = jnp.stack([lam_q1[0], lam_k1[0], lam_q2[0], lam_k2[0]])
    attn = _diff_attn(q, kt, v, bias, lamv, row(g_subln[0]), tq, tk)

    w_gates = jnp.concatenate([_block_diag(lru_wa[0]), _block_diag(lru_wx[0])], axis=1).astype(BF16)
    b_gates = jnp.concatenate([lru_ba[0], lru_bx[0]]).reshape(1, -1)
    lru = _rg_lru(xr, yg, conv_lru_w[0], row(conv_lru_b[0]), w_gates, b_gates, row(lru_lambda[0]))

    out = _out_ffn(x2d, lru, attn, mod, w_out[0].astype(BF16), row(g_norm2[0]), w_up[0].astype(BF16),
                   conv_ffn_w[0], row(conv_ffn_b[0]), w_down[0].astype(BF16), row(g_final))
    return out.reshape(b, s, d)
```

```python
import functools
import math

import jax
import jax.numpy as jnp
from jax import lax
from jax.experimental import pallas as pl
from jax.experimental.pallas import tpu as pltpu

F32 = jnp.float32
BF16 = jnp.bfloat16

D_MODEL = 1024
LRU_WIDTH = 512
LRU_BLOCKS = 8
LRU_BLOCK = LRU_WIDTH // LRU_BLOCKS
CONV_LRU = 4
LRU_C = 8.0
N_HEADS = 4
HEAD_DIM = 64
V_DIM = 2 * HEAD_DIM
QK_WIDTH = N_HEADS * 2 * HEAD_DIM
ATTN_WIDTH = N_HEADS * V_DIM
D_IN = 2 * QK_WIDTH + ATTN_WIDTH + 2 * LRU_WIDTH
D_FF = 3 * D_MODEL
CONV_FFN = 3
NUM_BUCKETS = 32
MAX_EXACT = NUM_BUCKETS // 2
MAX_DISTANCE = 128
EPS = 1e-6
NEG_INF = -1e30
LAMBDA_INIT = 0.8 - 0.6 * math.exp(-0.3 * 0)
LOG2E = math.log2(math.e)

LANES = 128
SUBLANES = 8
VMEM_LIMIT = 56 * 1024 * 1024

MOD_TN = 1536
ATT_TQ = 512
ATT_TK = 512
LRU_TC = 256
FFN_TM = 512
FFN_CK = 1536


def _rms_norm(x, g):
    y = x * lax.rsqrt(jnp.mean(x * x, axis=-1, keepdims=True) + EPS)
    return y * g


def _gelu_tanh(x):
    cdf = 0.5 * (1.0 + jnp.tanh(math.sqrt(2.0 / math.pi) * (x + 0.044715 * (x * x * x))))
    return x * cdf


def _shift_rows(prev8, x, s):
    ext = jnp.concatenate([prev8, x], axis=0)
    return pltpu.roll(ext, s, 0)[SUBLANES:]


def _adaln_kernel(c_ref, w_ref, b_ref, o_ref):
    c = c_ref[...]
    cond = c * jax.nn.sigmoid(c)
    o_ref[...] = jnp.sum(cond * w_ref[...], axis=0, keepdims=True) + b_ref[...]


def _adaln_mod(c_col, w_ada, b_ada):
    d, n = w_ada.shape
    return pl.pallas_call(
        _adaln_kernel,
        grid=(n // MOD_TN,),
        in_specs=[pl.BlockSpec((d, 1), lambda j: (0, 0)),
                  pl.BlockSpec((d, MOD_TN), lambda j: (0, j)),
                  pl.BlockSpec((1, MOD_TN), lambda j: (0, j))],
        out_specs=pl.BlockSpec((1, MOD_TN), lambda j: (0, j)),
        out_shape=jax.ShapeDtypeStruct((1, n), F32),
        compiler_params=pltpu.CompilerParams(vmem_limit_bytes=VMEM_LIMIT),
        name="adaln_mod",
    )(c_col, w_ada, b_ada)


def _in_proj_kernel(x_ref, mod_ref, g_ref, w_ref, q_ref, k_ref, v_ref, xr_ref, yg_ref):
    x = x_ref[...]
    shift1 = mod_ref[:, 0:D_MODEL]
    scale1 = mod_ref[:, D_MODEL:2 * D_MODEL]
    h = _rms_norm(x, g_ref[...]) * (1.0 + scale1) + shift1
    proj = jnp.dot(h.astype(BF16), w_ref[...], preferred_element_type=F32)
    ones = jnp.ones((x.shape[0], V_DIM), BF16)
    for hd in range(N_HEADS):
        lo = hd * V_DIM
        q_ref[hd] = (proj[:, lo:lo + V_DIM] * (HEAD_DIM ** -0.5 * LOG2E)).astype(BF16)
        k_ref[hd, 0] = proj[:, QK_WIDTH + lo:QK_WIDTH + lo + V_DIM].T.astype(BF16)
        v_ref[hd] = jnp.concatenate(
            [proj[:, 2 * QK_WIDTH + lo:2 * QK_WIDTH + lo + V_DIM].astype(BF16), ones], axis=1)
    base = 2 * QK_WIDTH + ATTN_WIDTH
    xr_ref[...] = proj[:, base:base + LRU_WIDTH]
    yg_ref[...] = proj[:, base + LRU_WIDTH:base + 2 * LRU_WIDTH]


def _in_proj(x2d, mod, g1, w_in_bf16, tm):
    s = x2d.shape[0]
    head_spec = pl.BlockSpec((N_HEADS, tm, V_DIM), lambda i: (0, i, 0))
    row_spec = pl.BlockSpec((tm, LRU_WIDTH), lambda i: (i, 0))
    return pl.pallas_call(
        _in_proj_kernel,
        grid=(s // tm,),
        in_specs=[pl.BlockSpec((tm, D_MODEL), lambda i: (i, 0)),
                  pl.BlockSpec((1, 6 * D_MODEL), lambda i: (0, 0)),
                  pl.BlockSpec((1, D_MODEL), lambda i: (0, 0)),
                  pl.BlockSpec((D_MODEL, D_IN), lambda i: (0, 0))],
        out_specs=[head_spec, pl.BlockSpec((N_HEADS, 1, V_DIM, tm), lambda i: (0, i, 0, 0)),
                   pl.BlockSpec((N_HEADS, tm, 2 * V_DIM), lambda i: (0, i, 0)), row_spec, row_spec],
        out_shape=[jax.ShapeDtypeStruct((N_HEADS, s, V_DIM), BF16),
                   jax.ShapeDtypeStruct((N_HEADS, s // tm, V_DIM, tm), BF16),
                   jax.ShapeDtypeStruct((N_HEADS, s, 2 * V_DIM), BF16)]
        + [jax.ShapeDtypeStruct((s, LRU_WIDTH), F32)] * 2,
        compiler_params=pltpu.CompilerParams(dimension_semantics=("parallel",),
                                             vmem_limit_bytes=VMEM_LIMIT),
        name="in_proj",
    )(x2d, mod, g1, w_in_bf16)


def _bias_tiles_kernel(table_ref, o_ref, *, tq, tk, nd):
    hd = pl.program_id(0)
    blk = MAX_DISTANCE
    qpos = lax.broadcasted_iota(jnp.int32, (blk, blk), 0)
    kpos = lax.broadcasted_iota(jnp.int32, (blk, blk), 1)
    far = table_ref[hd * NUM_BUCKETS + NUM_BUCKETS - 1]

    def band_block(offset):
        rel = offset * blk + qpos - kpos
        n = jnp.maximum(rel, 0)
        nf = jnp.maximum(n, 1).astype(F32)
        y = (jnp.log(nf / MAX_EXACT) / math.log(MAX_DISTANCE / MAX_EXACT) * (NUM_BUCKETS - MAX_EXACT))
        val = jnp.zeros((blk, blk), F32)
        for b in range(NUM_BUCKETS):
            hit = (n == b) if b < MAX_EXACT else ((n >= MAX_EXACT) & (y >= b - MAX_EXACT))
            val = jnp.where(hit, (table_ref[hd * NUM_BUCKETS + b] - far) * LOG2E, val)
        return jnp.where(rel >= 0, val, NEG_INF)

    blocks = {0: band_block(0), 1: band_block(1)}
    zeros = jnp.zeros((blk, blk), F32)
    neg = jnp.full((blk, blk), NEG_INF, F32)
    for dd in range(nd):
        for a in range(tq // blk):
            for b in range(tk // blk):
                off = ((dd + 1) * tk - tq) // blk + a - b
                o_ref[0, dd, a * blk:(a + 1) * blk, b * blk:(b + 1) * blk] = (
                    neg if off < 0 else blocks.get(off, zeros))
    o_ref[0, nd] = jnp.zeros((tq, tk), F32)
    o_ref[0, nd + 1] = jnp.full((tq, tk), NEG_INF, F32)


def _bias_tiles(table_flat, tq, tk):
    assert tq % tk == 0 and tk % MAX_DISTANCE == 0
    nd = tq // tk + 1
    return pl.pallas_call(
        functools.partial(_bias_tiles_kernel, tq=tq, tk=tk, nd=nd),
        grid=(N_HEADS,),
        in_specs=[pl.BlockSpec(memory_space=pltpu.SMEM)],
        out_specs=pl.BlockSpec((1, nd + 2, tq, tk), lambda h: (h, 0, 0, 0)),
        out_shape=jax.ShapeDtypeStruct((N_HEADS, nd + 2, tq, tk), F32),
        name="bias_tiles",
    )(table_flat)


def _attn_kernel(q_ref, qn_ref, k_ref, v_ref, bias_ref, lamv_ref, gs_ref, o_ref, qs_sc, sa_sc, sb_sc, pa_sc, pb_sc,
                 m_sc, acc_sc, *, tq, tk, nd):
    i = pl.program_id(1)

    def stack_maps(q):
        lane = lax.broadcasted_iota(jnp.int32, q.shape, 1)
        zero = jnp.zeros_like(q)
        return jnp.concatenate([jnp.where(lane < HEAD_DIM, q, zero), jnp.where(lane >= HEAD_DIM, q, zero)], axis=0)

    qs_sc[0] = stack_maps(q_ref[0])
    qs_sc[1] = stack_maps(qn_ref[0])

    def limits(qi):
        q_start = qi * tq
        n_far = jnp.maximum(q_start - (MAX_DISTANCE - 1), 0) // tk
        last = (q_start + tq - 1) // tk
        return q_start, n_far, last

    _, _, last = limits(i)
    n_pairs = (last + 2) // 2

    def scores(slot, qi, j, s_ref, part_ref):
        q_start, n_far, last_q = limits(qi)
        jc = jnp.minimum(j, last_q)
        s = jnp.dot(qs_sc[slot], k_ref[0, jc], preferred_element_type=F32)
        dd = (q_start + tq - (jc + 1) * tk) // tk
        idx = jnp.where(j > last_q, nd + 1, jnp.where(j < n_far, nd, dd))
        s = (s.reshape(2, tq, tk) + bias_ref[0, idx][None]).reshape(2 * tq, tk)
        s_ref[...] = s
        part_ref[...] = functools.reduce(
            jnp.maximum, [s[:, c * LANES:(c + 1) * LANES] for c in range(tk // LANES)])

    def accumulate(s_ref, part_ref, j):
        jc = jnp.minimum(j, last)
        vb = v_ref[0, pl.ds(pl.multiple_of(jc * tk, tk), tk), :]
        m_prev = m_sc[...]
        m_new = jnp.maximum(m_prev, jnp.max(part_ref[...], axis=1, keepdims=True))
        alpha = jnp.exp2(m_prev - m_new)
        p = jnp.exp2(s_ref[...] - jnp.concatenate([m_new] * (tk // LANES), axis=1))
        acc_sc[...] = (jnp.concatenate([alpha, alpha], axis=1) * acc_sc[...]
                       + jnp.dot(p.astype(BF16), vb, preferred_element_type=F32))
        m_sc[...] = m_new

    @pl.when(i == 0)
    def _():
        scores(0, i, 0, sa_sc, pa_sc)

    m_sc[...] = jnp.full(m_sc.shape, NEG_INF, F32)
    acc_sc[...] = jnp.zeros(acc_sc.shape, F32)

    def two_blocks(j):
        scores(0, i, j + 1, sb_sc, pb_sc)
        accumulate(sa_sc, pa_sc, j)
        to_next = (j + 2 > last).astype(jnp.int32)
        scores(to_next, i + to_next, (1 - to_next) * (j + 2), sa_sc, pa_sc)
        accumulate(sb_sc, pb_sc, j + 1)

    def oct_body(t, carry):
        for u in range(4):
            two_blocks(8 * t + 2 * u)
        return carry

    def quad_body(t, carry):
        two_blocks(8 * n_octs + 4 * t)
        two_blocks(8 * n_octs + 4 * t + 2)
        return carry

    def pair_body(t, carry):
        two_blocks(4 * n_quads + 2 * t)
        return carry

    n_octs = (last + 1) // 8
    n_quads = (last + 1) // 4
    lax.fori_loop(0, n_octs, oct_body, 0)
    lax.fori_loop(0, n_quads - 2 * n_octs, quad_body, 0)
    lax.fori_loop(0, n_pairs - 2 * n_quads, pair_body, 0)

    lv = lamv_ref[...]
    d1 = jnp.sum(lv[0:1] * lv[1:2], axis=1, keepdims=True)
    d2 = jnp.sum(lv[2:3] * lv[3:4], axis=1, keepdims=True)
    lam = jnp.exp(d1) - jnp.exp(d2) + LAMBDA_INIT

    acc = acc_sc[...]
    out = acc[:, :V_DIM] / acc[:, V_DIM:]
    diff = out[:tq] - lam * out[tq:]
    o_ref[...] = (_rms_norm(diff, gs_ref[...]) * (1.0 - LAMBDA_INIT)).astype(o_ref.dtype)


def _diff_attn(q, kt, v_aug, bias, lamv, g_subln, tq, tk):
    s = q.shape[1]
    nq = s // tq
    nd = bias.shape[1] - 2
    return pl.pallas_call(
        functools.partial(_attn_kernel, tq=tq, tk=tk, nd=nd),
        grid=(N_HEADS, nq),
        in_specs=[pl.BlockSpec((1, tq, V_DIM), lambda h, i: (h, i, 0)),
                  pl.BlockSpec((1, tq, V_DIM), lambda h, i: (h, jnp.minimum(i + 1, nq - 1), 0)),
                  pl.BlockSpec((1, s // tk, V_DIM, tk), lambda h, i: (h, 0, 0, 0)),
                  pl.BlockSpec((1, s, 2 * V_DIM), lambda h, i: (h, 0, 0)),
                  pl.BlockSpec((1, nd + 2, tq, tk), lambda h, i: (h, 0, 0, 0), pipeline_mode=pl.Buffered(1)),
                  pl.BlockSpec((4, HEAD_DIM), lambda h, i: (0, 0)),
                  pl.BlockSpec((1, V_DIM), lambda h, i: (0, 0))],
        out_specs=pl.BlockSpec((tq, V_DIM), lambda h, i: (i, h)),
        out_shape=jax.ShapeDtypeStruct((s, ATTN_WIDTH), BF16),
        scratch_shapes=[pltpu.VMEM((2, 2 * tq, V_DIM), BF16),
                        pltpu.VMEM((2 * tq, tk), F32),
                        pltpu.VMEM((2 * tq, tk), F32),
                        pltpu.VMEM((2 * tq, LANES), F32),
                        pltpu.VMEM((2 * tq, LANES), F32),
                        pltpu.VMEM((2 * tq, LANES), F32),
                        pltpu.VMEM((2 * tq, 2 * V_DIM), F32)],
        compiler_params=pltpu.CompilerParams(dimension_semantics=("arbitrary", "arbitrary"),
                                             vmem_limit_bytes=VMEM_LIMIT),
        name="diff_attn",
    )(q, q, kt, v_aug, bias, lamv, g_subln)


def _lru_kernel(xr_ref, yg_ref, cw_ref, cb_ref, wg_ref, bg_ref, lam_ref, o_ref, tail_sc, h_sc, *, tc):
    @pl.when(pl.program_id(0) == 0)
    def _():
        tail_sc[...] = jnp.zeros(tail_sc.shape, F32)
        h_sc[...] = jnp.zeros(h_sc.shape, F32)

    x = xr_ref[...]
    prev8 = tail_sc[...]
    cw = cw_ref[...]
    xc = cb_ref[...]
    for j in range(CONV_LRU):
        sh = CONV_LRU - 1 - j
        xs = x if sh == 0 else _shift_rows(prev8, x, sh)
        xc = xc + xs * cw[j:j + 1]
    tail_sc[...] = x[tc - SUBLANES:]

    gates = jnp.dot(xc.astype(BF16), wg_ref[...], preferred_element_type=F32) + bg_ref[...]
    sig = 0.5 * jnp.tanh(0.5 * gates) + 0.5
    r = sig[:, :LRU_WIDTH]
    ig = sig[:, LRU_WIDTH:]
    z = -lam_ref[...]
    softplus = jnp.maximum(z, 0.0) + jnp.log1p(jnp.exp(-jnp.abs(z)))
    log_a = -LRU_C * r * softplus
    a = jnp.exp(log_a)
    th = jnp.tanh(log_a)
    b = jnp.sqrt(-2.0 * th / (1.0 - th)) * (ig * xc)

    ng = tc // SUBLANES
    a = a.reshape(ng, SUBLANES, LRU_WIDTH)
    b = b.reshape(ng, SUBLANES, LRU_WIDTH)
    row = lax.broadcasted_iota(jnp.int32, a.shape, 1)
    sh = 1
    while sh < SUBLANES:
        valid = row >= sh
        a_s = jnp.where(valid, pltpu.roll(a, sh, 1), 1.0)
        b_s = jnp.where(valid, pltpu.roll(b, sh, 1), 0.0)
        b = a * b_s + b
        a = a * a_s
        sh *= 2
    carry = h_sc[0:1]
    hs = []
    for g in range(ng):
        hg = a[g] * carry + b[g]
        hs.append(hg)
        carry = hg[SUBLANES - 1:SUBLANES]
    h = jnp.concatenate(hs, axis=0)
    h_sc[0:1] = carry
    o_ref[...] = (h * _gelu_tanh(yg_ref[...])).astype(o_ref.dtype)


def _rg_lru(xr, yg, conv_w, conv_b, w_gates_bf16, b_gates, lam):
    s = xr.shape[0]
    tc = min(LRU_TC, s)
    row_spec = pl.BlockSpec((tc, LRU_WIDTH), lambda i: (i, 0))
    const = lambda shape: pl.BlockSpec(shape, lambda i: (0, 0))
    return pl.pallas_call(
        functools.partial(_lru_kernel, tc=tc),
        grid=(s // tc,),
        in_specs=[row_spec, row_spec,
                  const((CONV_LRU, LRU_WIDTH)), const((1, LRU_WIDTH)),
                  const((LRU_WIDTH, 2 * LRU_WIDTH)), const((1, 2 * LRU_WIDTH)),
                  const((1, LRU_WIDTH))],
        out_specs=row_spec,
        out_shape=jax.ShapeDtypeStruct((s, LRU_WIDTH), BF16),
        scratch_shapes=[pltpu.VMEM((SUBLANES, LRU_WIDTH), F32),
                        pltpu.VMEM((SUBLANES, LRU_WIDTH), F32)],
        compiler_params=pltpu.CompilerParams(dimension_semantics=("arbitrary",),
                                             vmem_limit_bytes=VMEM_LIMIT),
        name="rg_lru",
    )(xr, yg, conv_w, conv_b, w_gates_bf16, b_gates, lam)


def _ffn_kernel(x_ref, lru_ref, attn_ref, mod_ref, wo_ref, g2_ref, wup_ref, cw_ref, cb_ref, wdn_ref, gf_ref,
                o_ref, tail_sc, *, tm):
    @pl.when(pl.program_id(0) == 0)
    def _():
        tail_sc[...] = jnp.zeros(tail_sc.shape, F32)

    gate1 = mod_ref[:, 2 * D_MODEL:3 * D_MODEL]
    shift2 = mod_ref[:, 3 * D_MODEL:4 * D_MODEL]
    scale2 = mod_ref[:, 4 * D_MODEL:5 * D_MODEL]
    gate2 = mod_ref[:, 5 * D_MODEL:6 * D_MODEL]

    mix = (jnp.dot(lru_ref[...], wo_ref[0:LRU_WIDTH, :], preferred_element_type=F32)
           + jnp.dot(attn_ref[...], wo_ref[LRU_WIDTH:, :], preferred_element_type=F32))
    x1 = x_ref[...] + gate1 * mix
    h2 = (_rms_norm(x1, g2_ref[...]) * (1.0 + scale2) + shift2).astype(BF16)

    ff = jnp.zeros((tm, D_MODEL), F32)
    for c in range(D_FF // FFN_CK):
        lo = c * FFN_CK
        a = jnp.dot(h2, wup_ref[:, lo:lo + FFN_CK], preferred_element_type=F32)
        g = jnp.dot(h2, wup_ref[:, D_FF + lo:D_FF + lo + FFN_CK], preferred_element_type=F32)
        prev8 = tail_sc[:, lo:lo + FFN_CK]
        ac = cb_ref[:, lo:lo + FFN_CK]
        for j in range(CONV_FFN):
            sh = CONV_FFN - 1 - j
            a_s = a if sh == 0 else _shift_rows(prev8, a, sh)
            ac = ac + a_s * cw_ref[j:j + 1, lo:lo + FFN_CK]
        tail_sc[:, lo:lo + FFN_CK] = a[tm - SUBLANES:]
        u = (_gelu_tanh(ac) * g).astype(BF16)
        ff = ff + jnp.dot(u, wdn_ref[lo:lo + FFN_CK, :], preferred_element_type=F32)

    x2 = x1 + gate2 * ff
    o_ref[...] = _rms_norm(x2, gf_ref[...])


def _out_ffn(x2d, lru, attn, mod, w_out, g2, w_up, conv_w, conv_b, w_down, g_final):
    s = x2d.shape[0]
    tm = min(FFN_TM, s)
    row = lambda w: pl.BlockSpec((tm, w), lambda i: (i, 0))
    const = lambda shape: pl.BlockSpec(shape, lambda i: (0, 0), pipeline_mode=pl.Buffered(1))
    return pl.pallas_call(
        functools.partial(_ffn_kernel, tm=tm),
        grid=(s // tm,),
        in_specs=[row(D_MODEL), row(LRU_WIDTH), row(ATTN_WIDTH),
                  const((1, 6 * D_MODEL)),
                  const((D_MODEL, D_MODEL)), const((1, D_MODEL)),
                  const((D_MODEL, 2 * D_FF)),
                  const((CONV_FFN, D_FF)), const((1, D_FF)),
                  const((D_FF, D_MODEL)), const((1, D_MODEL))],
        out_specs=row(D_MODEL),
        out_shape=jax.ShapeDtypeStruct((s, D_MODEL), F32),
        scratch_shapes=[pltpu.VMEM((SUBLANES, D_FF), F32)],
        compiler_params=pltpu.CompilerParams(dimension_semantics=("arbitrary",),
                                             vmem_limit_bytes=VMEM_LIMIT),
        name="out_ffn",
    )(x2d, lru, attn, mod, w_out, g2, w_up, conv_w, conv_b, w_down, g_final)


def _block_diag(w):
    nb, bs, _ = w.shape
    eye = jnp.eye(nb, dtype=w.dtype)
    return (w[:, :, None, :] * eye[:, None, :, None]).reshape(nb * bs, nb * bs)


def kernel(x, c, w_ada, b_ada, g_norm1, w_in, conv_lru_w, conv_lru_b, lru_wa, lru_ba, lru_wx, lru_bx, lru_lambda,
           lam_q1, lam_k1, lam_q2, lam_k2, g_subln, w_out, g_norm2, w_up, conv_ffn_w, conv_ffn_b, w_down, rel_bias,
           g_final):
    b, s, d = x.shape
    assert b == 1 and d == D_MODEL and w_ada.shape[0] == 1
    x2d = x.reshape(s, d)
    row = lambda a: a.reshape(1, -1)

    mod = _adaln_mod(c.reshape(d, 1), w_ada[0], row(b_ada[0]))

    tq, tk = min(ATT_TQ, s), min(ATT_TK, s)
    q, kt, v, xr, yg = _in_proj(x2d, mod, row(g_norm1[0]), w_in[0].astype(BF16), tk)

    bias = _bias_tiles(rel_bias.T.reshape(-1), tq, tk)
    lamv = jnp.stack([lam_q1[0], lam_k1[0], lam_q2[0], lam_k2[0]])
    attn = _diff_attn(q, kt, v, bias, lamv, row(g_subln[0]), tq, tk)

    w_gates = jnp.concatenate([_block_diag(lru_wa[0]), _block_diag(lru_wx[0])], axis=1).astype(BF16)
    b_gates = jnp.concatenate([lru_ba[0], lru_bx[0]]).reshape(1, -1)
    lru = _rg_lru(xr, yg, conv_lru_w[0], row(conv_lru_b[0]), w_gates, b_gates, row(lru_lambda[0]))

    out = _out_ffn(x2d, lru, attn, mod, w_out[0].astype(BF16), row(g_norm2[0]), w_up[0].astype(BF16),
                   conv_ffn_w[0], row(conv_ffn_b[0]), w_down[0].astype(BF16), row(g_final))
    return out.reshape(b, s, d)
```

```python
import functools
import math

import jax
import jax.numpy as jnp
from jax import lax
from jax.experimental import pallas as pl
from jax.experimental.pallas import tpu as pltpu

F32 = jnp.float32
BF16 = jnp.bfloat16

D_MODEL = 1024
LRU_WIDTH = 512
LRU_BLOCKS = 8
LRU_BLOCK = LRU_WIDTH // LRU_BLOCKS
CONV_LRU = 4
LRU_C = 8.0
N_HEADS = 4
HEAD_DIM = 64
V_DIM = 2 * HEAD_DIM
QK_WIDTH = N_HEADS * 2 * HEAD_DIM
ATTN_WIDTH = N_HEADS * V_DIM
D_IN = 2 * QK_WIDTH + ATTN_WIDTH + 2 * LRU_WIDTH
D_FF = 3 * D_MODEL
CONV_FFN = 3
NUM_BUCKETS = 32
MAX_EXACT = NUM_BUCKETS // 2
MAX_DISTANCE = 128
EPS = 1e-6
NEG_INF = -1e30
LAMBDA_INIT = 0.8 - 0.6 * math.exp(-0.3 * 0)
LOG2E = math.log2(math.e)

LANES = 128
SUBLANES = 8
VMEM_LIMIT = 56 * 1024 * 1024

MOD_TN = 1536
ATT_TQ = 512
ATT_TK = 512
LRU_TC = 256
FFN_TM = 512
FFN_CK = 1536


def _rms_norm(x, g):
    y = x * lax.rsqrt(jnp.mean(x * x, axis=-1, keepdims=True) + EPS)
    return y * g


def _gelu_tanh(x):
    cdf = 0.5 * (1.0 + jnp.tanh(math.sqrt(2.0 / math.pi) * (x + 0.044715 * (x * x * x))))
    return x * cdf


def _shift_rows(prev8, x, s):
    ext = jnp.concatenate([prev8, x], axis=0)
    return pltpu.roll(ext, s, 0)[SUBLANES:]


def _adaln_kernel(c_ref, w_ref, b_ref, o_ref):
    c = c_ref[...]
    cond = c * jax.nn.sigmoid(c)
    o_ref[...] = jnp.sum(cond * w_ref[...], axis=0, keepdims=True) + b_ref[...]


def _adaln_mod(c_col, w_ada, b_ada):
    d, n = w_ada.shape
    return pl.pallas_call(
        _adaln_kernel,
        grid=(n // MOD_TN,),
        in_specs=[pl.BlockSpec((d, 1), lambda j: (0, 0)),
                  pl.BlockSpec((d, MOD_TN), lambda j: (0, j)),
                  pl.BlockSpec((1, MOD_TN), lambda j: (0, j))],
        out_specs=pl.BlockSpec((1, MOD_TN), lambda j: (0, j)),
        out_shape=jax.ShapeDtypeStruct((1, n), F32),
        compiler_params=pltpu.CompilerParams(vmem_limit_bytes=VMEM_LIMIT),
        name="adaln_mod",
    )(c_col, w_ada, b_ada)


def _in_proj_kernel(x_ref, mod_ref, g_ref, w_ref, q_ref, k_ref, v_ref, xr_ref, yg_ref):
    x = x_ref[...]
    shift1 = mod_ref[:, 0:D_MODEL]
    scale1 = mod_ref[:, D_MODEL:2 * D_MODEL]
    h = _rms_norm(x, g_ref[...]) * (1.0 + scale1) + shift1
    proj = jnp.dot(h.astype(BF16), w_ref[...].astype(BF16), preferred_element_type=F32)
    ones = jnp.ones((x.shape[0], V_DIM), BF16)
    for hd in range(N_HEADS):
        lo = hd * V_DIM
        q_ref[hd] = (proj[:, lo:lo + V_DIM] * (HEAD_DIM ** -0.5 * LOG2E)).astype(BF16)
        k_ref[hd, 0] = proj[:, QK_WIDTH + lo:QK_WIDTH + lo + V_DIM].T.astype(BF16)
        v_ref[hd] = jnp.concatenate(
            [proj[:, 2 * QK_WIDTH + lo:2 * QK_WIDTH + lo + V_DIM].astype(BF16), ones], axis=1)
    base = 2 * QK_WIDTH + ATTN_WIDTH
    xr_ref[...] = proj[:, base:base + LRU_WIDTH]
    yg_ref[...] = proj[:, base + LRU_WIDTH:base + 2 * LRU_WIDTH]


def _in_proj(x2d, mod, g1, w_in, tm):
    s = x2d.shape[0]
    head_spec = pl.BlockSpec((N_HEADS, tm, V_DIM), lambda i: (0, i, 0))
    row_spec = pl.BlockSpec((tm, LRU_WIDTH), lambda i: (i, 0))
    return pl.pallas_call(
        _in_proj_kernel,
        grid=(s // tm,),
        in_specs=[pl.BlockSpec((tm, D_MODEL), lambda i: (i, 0)),
                  pl.BlockSpec((1, 6 * D_MODEL), lambda i: (0, 0)),
                  pl.BlockSpec((1, D_MODEL), lambda i: (0, 0)),
                  pl.BlockSpec((D_MODEL, D_IN), lambda i: (0, 0), pipeline_mode=pl.Buffered(1))],
        out_specs=[head_spec, pl.BlockSpec((N_HEADS, 1, V_DIM, tm), lambda i: (0, i, 0, 0)),
                   pl.BlockSpec((N_HEADS, tm, 2 * V_DIM), lambda i: (0, i, 0)), row_spec, row_spec],
        out_shape=[jax.ShapeDtypeStruct((N_HEADS, s, V_DIM), BF16),
                   jax.ShapeDtypeStruct((N_HEADS, s // tm, V_DIM, tm), BF16),
                   jax.ShapeDtypeStruct((N_HEADS, s, 2 * V_DIM), BF16)]
        + [jax.ShapeDtypeStruct((s, LRU_WIDTH), F32)] * 2,
        compiler_params=pltpu.CompilerParams(dimension_semantics=("parallel",),
                                             vmem_limit_bytes=VMEM_LIMIT),
        name="in_proj",
    )(x2d, mod, g1, w_in)


def _bias_tiles_kernel(table_ref, o_ref, *, tq, tk, nd):
    hd = pl.program_id(0)
    blk = MAX_DISTANCE
    qpos = lax.broadcasted_iota(jnp.int32, (blk, blk), 0)
    kpos = lax.broadcasted_iota(jnp.int32, (blk, blk), 1)
    far = table_ref[hd * NUM_BUCKETS + NUM_BUCKETS - 1]

    def band_block(offset):
        rel = offset * blk + qpos - kpos
        n = jnp.maximum(rel, 0)
        nf = jnp.maximum(n, 1).astype(F32)
        y = (jnp.log(nf / MAX_EXACT) / math.log(MAX_DISTANCE / MAX_EXACT) * (NUM_BUCKETS - MAX_EXACT))
        val = jnp.zeros((blk, blk), F32)
        for b in range(NUM_BUCKETS):
            hit = (n == b) if b < MAX_EXACT else ((n >= MAX_EXACT) & (y >= b - MAX_EXACT))
            val = jnp.where(hit, (table_ref[hd * NUM_BUCKETS + b] - far) * LOG2E, val)
        return jnp.where(rel >= 0, val, NEG_INF)

    blocks = {0: band_block(0), 1: band_block(1)}
    zeros = jnp.zeros((blk, blk), F32)
    neg = jnp.full((blk, blk), NEG_INF, F32)
    for dd in range(nd):
        for a in range(tq // blk):
            for b in range(tk // blk):
                off = ((dd + 1) * tk - tq) // blk + a - b
                o_ref[0, dd, a * blk:(a + 1) * blk, b * blk:(b + 1) * blk] = (
                    neg if off < 0 else blocks.get(off, zeros))
    o_ref[0, nd] = jnp.zeros((tq, tk), F32)
    o_ref[0, nd + 1] = jnp.full((tq, tk), NEG_INF, F32)


def _bias_tiles(table_flat, tq, tk):
    assert tq % tk == 0 and tk % MAX_DISTANCE == 0
    nd = tq // tk + 1
    return pl.pallas_call(
        functools.partial(_bias_tiles_kernel, tq=tq, tk=tk, nd=nd),
        grid=(N_HEADS,),
        in_specs=[pl.BlockSpec(memory_space=pltpu.SMEM)],
        out_specs=pl.BlockSpec((1, nd + 2, tq, tk), lambda h: (h, 0, 0, 0)),
        out_shape=jax.ShapeDtypeStruct((N_HEADS, nd + 2, tq, tk), F32),
        name="bias_tiles",
    )(table_flat)


def _attn_kernel(q_ref, qn_ref, k_ref, v_ref, bias_ref, lamv_ref, gs_ref, o_ref, qs_sc, sa_sc, sb_sc, pa_sc, pb_sc,
                 m_sc, acc_sc, *, tq, tk, nd):
    i = pl.program_id(1)

    def stack_maps(q):
        lane = lax.broadcasted_iota(jnp.int32, q.shape, 1)
        zero = jnp.zeros_like(q)
        return jnp.concatenate([jnp.where(lane < HEAD_DIM, q, zero), jnp.where(lane >= HEAD_DIM, q, zero)], axis=0)

    qs_sc[0] = stack_maps(q_ref[0])
    qs_sc[1] = stack_maps(qn_ref[0])

    def limits(qi):
        q_start = qi * tq
        n_far = jnp.maximum(q_start - (MAX_DISTANCE - 1), 0) // tk
        last = (q_start + tq - 1) // tk
        return q_start, n_far, last

    _, _, last = limits(i)
    n_pairs = (last + 1) // 2

    def scores(slot, qi, j, s_ref, part_ref):
        q_start, n_far, last_q = limits(qi)
        jc = jnp.minimum(j, last_q)
        s = jnp.dot(qs_sc[slot], k_ref[0, jc], preferred_element_type=F32)
        dd = (q_start + tq - (jc + 1) * tk) // tk
        idx = jnp.where(j > last_q, nd + 1, jnp.where(j < n_far, nd, dd))
        s = (s.reshape(2, tq, tk) + bias_ref[0, idx][None]).reshape(2 * tq, tk)
        s_ref[...] = s
        part_ref[...] = functools.reduce(
            jnp.maximum, [s[:, c * LANES:(c + 1) * LANES] for c in range(tk // LANES)])

    def accumulate(s_ref, part_ref, j):
        jc = jnp.minimum(j, last)
        vb = v_ref[0, pl.ds(pl.multiple_of(jc * tk, tk), tk), :]
        m_prev = m_sc[...]
        m_new = jnp.maximum(m_prev, jnp.max(part_ref[...], axis=1, keepdims=True))
        alpha = jnp.exp2(m_prev - m_new)
        p = jnp.exp2(s_ref[...] - jnp.concatenate([m_new] * (tk // LANES), axis=1))
        acc_sc[...] = (jnp.concatenate([alpha, alpha], axis=1) * acc_sc[...]
                       + jnp.dot(p.astype(BF16), vb, preferred_element_type=F32))
        m_sc[...] = m_new

    @pl.when(i == 0)
    def _():
        scores(0, i, 0, sa_sc, pa_sc)

    m_sc[...] = jnp.full(m_sc.shape, NEG_INF, F32)
    acc_sc[...] = jnp.zeros(acc_sc.shape, F32)

    def two_blocks(j):
        scores(0, i, j + 1, sb_sc, pb_sc)
        accumulate(sa_sc, pa_sc, j)
        to_next = (j + 2 > last).astype(jnp.int32)
        scores(to_next, i + to_next, (1 - to_next) * (j + 2), sa_sc, pa_sc)
        accumulate(sb_sc, pb_sc, j + 1)

    def oct_body(t, carry):
        for u in range(4):
            two_blocks(8 * t + 2 * u)
        return carry

    def quad_body(t, carry):
        two_blocks(8 * n_octs + 4 * t)
        two_blocks(8 * n_octs + 4 * t + 2)
        return carry

    def pair_body(t, carry):
        two_blocks(4 * n_quads + 2 * t)
        return carry

    n_octs = (last + 1) // 8
    n_quads = (last + 1) // 4
    lax.fori_loop(0, n_octs, oct_body, 0)
    lax.fori_loop(0, n_quads - 2 * n_octs, quad_body, 0)
    lax.fori_loop(0, n_pairs - 2 * n_quads, pair_body, 0)

    def single_body(t, carry):
        scores(1, i + 1, 0, sb_sc, pb_sc)
        accumulate(sa_sc, pa_sc, last)
        sa_sc[...] = sb_sc[...]
        pa_sc[...] = pb_sc[...]
        return carry

    lax.fori_loop(0, (last + 1) % 2, single_body, 0)

    lv = lamv_ref[...]
    d1 = jnp.sum(lv[0:1] * lv[1:2], axis=1, keepdims=True)
    d2 = jnp.sum(lv[2:3] * lv[3:4], axis=1, keepdims=True)
    lam = jnp.exp(d1) - jnp.exp(d2) + LAMBDA_INIT

    acc = acc_sc[...]
    out = acc[:, :V_DIM] / acc[:, V_DIM:]
    diff = out[:tq] - lam * out[tq:]
    o_ref[...] = (_rms_norm(diff, gs_ref[...]) * (1.0 - LAMBDA_INIT)).astype(o_ref.dtype)


def _diff_attn(q, kt, v_aug, bias, lamv, g_subln, tq, tk):
    s = q.shape[1]
    nq = s // tq
    nd = bias.shape[1] - 2
    return pl.pallas_call(
        functools.partial(_attn_kernel, tq=tq, tk=tk, nd=nd),
        grid=(N_HEADS, nq),
        in_specs=[pl.BlockSpec((1, tq, V_DIM), lambda h, i: (h, i, 0)),
                  pl.BlockSpec((1, tq, V_DIM), lambda h, i: (h, jnp.minimum(i + 1, nq - 1), 0)),
                  pl.BlockSpec((1, s // tk, V_DIM, tk), lambda h, i: (h, 0, 0, 0)),
                  pl.BlockSpec((1, s, 2 * V_DIM), lambda h, i: (h, 0, 0)),
                  pl.BlockSpec((1, nd + 2, tq, tk), lambda h, i: (h, 0, 0, 0), pipeline_mode=pl.Buffered(1)),
                  pl.BlockSpec((4, HEAD_DIM), lambda h, i: (0, 0)),
                  pl.BlockSpec((1, V_DIM), lambda h, i: (0, 0))],
        out_specs=pl.BlockSpec((tq, V_DIM), lambda h, i: (i, h)),
        out_shape=jax.ShapeDtypeStruct((s, ATTN_WIDTH), BF16),
        scratch_shapes=[pltpu.VMEM((2, 2 * tq, V_DIM), BF16),
                        pltpu.VMEM((2 * tq, tk), F32),
                        pltpu.VMEM((2 * tq, tk), F32),
                        pltpu.VMEM((2 * tq, LANES), F32),
                        pltpu.VMEM((2 * tq, LANES), F32),
                        pltpu.VMEM((2 * tq, LANES), F32),
                        pltpu.VMEM((2 * tq, 2 * V_DIM), F32)],
        compiler_params=pltpu.CompilerParams(dimension_semantics=("arbitrary", "arbitrary"),
                                             vmem_limit_bytes=VMEM_LIMIT),
        name="diff_attn",
    )(q, q, kt, v_aug, bias, lamv, g_subln)


def _lru_kernel(xr_ref, yg_ref, cw_ref, cb_ref, wg_ref, bg_ref, lam_ref, o_ref, tail_sc, h_sc, *, tc):
    @pl.when(pl.program_id(0) == 0)
    def _():
        tail_sc[...] = jnp.zeros(tail_sc.shape, F32)
        h_sc[...] = jnp.zeros(h_sc.shape, F32)

    x = xr_ref[...]
    prev8 = tail_sc[...]
    cw = cw_ref[...]
    xc = cb_ref[...]
    for j in range(CONV_LRU):
        sh = CONV_LRU - 1 - j
        xs = x if sh == 0 else _shift_rows(prev8, x, sh)
        xc = xc + xs * cw[j:j + 1]
    tail_sc[...] = x[tc - SUBLANES:]

    gates = jnp.dot(xc.astype(BF16), wg_ref[...], preferred_element_type=F32) + bg_ref[...]
    sig = 0.5 * jnp.tanh(0.5 * gates) + 0.5
    r = sig[:, :LRU_WIDTH]
    ig = sig[:, LRU_WIDTH:]
    z = -lam_ref[...]
    softplus = jnp.maximum(z, 0.0) + jnp.log1p(jnp.exp(-jnp.abs(z)))
    log_a = -LRU_C * r * softplus
    a = jnp.exp(log_a)
    th = jnp.tanh(log_a)
    b = jnp.sqrt(-2.0 * th / (1.0 - th)) * (ig * xc)

    ng = tc // SUBLANES
    a = a.reshape(ng, SUBLANES, LRU_WIDTH)
    b = b.reshape(ng, SUBLANES, LRU_WIDTH)
    row = lax.broadcasted_iota(jnp.int32, a.shape, 1)
    sh = 1
    while sh < SUBLANES:
        valid = row >= sh
        a_s = jnp.where(valid, pltpu.roll(a, sh, 1), 1.0)
        b_s = jnp.where(valid, pltpu.roll(b, sh, 1), 0.0)
        b = a * b_s + b
        a = a * a_s
        sh *= 2
    carry = h_sc[0:1]
    hs = []
    for g in range(ng):
        hg = a[g] * carry + b[g]
        hs.append(hg)
        carry = hg[SUBLANES - 1:SUBLANES]
    h = jnp.concatenate(hs, axis=0)
    h_sc[0:1] = carry
    o_ref[...] = (h * _gelu_tanh(yg_ref[...])).astype(o_ref.dtype)


def _rg_lru(xr, yg, conv_w, conv_b, w_gates_bf16, b_gates, lam):
    s = xr.shape[0]
    tc = min(LRU_TC, s)
    row_spec = pl.BlockSpec((tc, LRU_WIDTH), lambda i: (i, 0))
    const = lambda shape: pl.BlockSpec(shape, lambda i: (0, 0))
    return pl.pallas_call(
        functools.partial(_lru_kernel, tc=tc),
        grid=(s // tc,),
        in_specs=[row_spec, row_spec,
                  const((CONV_LRU, LRU_WIDTH)), const((1, LRU_WIDTH)),
                  const((LRU_WIDTH, 2 * LRU_WIDTH)), const((1, 2 * LRU_WIDTH)),
                  const((1, LRU_WIDTH))],
        out_specs=row_spec,
        out_shape=jax.ShapeDtypeStruct((s, LRU_WIDTH), BF16),
        scratch_shapes=[pltpu.VMEM((SUBLANES, LRU_WIDTH), F32),
                        pltpu.VMEM((SUBLANES, LRU_WIDTH), F32)],
        compiler_params=pltpu.CompilerParams(dimension_semantics=("arbitrary",),
                                             vmem_limit_bytes=VMEM_LIMIT),
        name="rg_lru",
    )(xr, yg, conv_w, conv_b, w_gates_bf16, b_gates, lam)


def _ffn_kernel(x_ref, lru_ref, attn_ref, mod_ref, wo_ref, g2_ref, wup_ref, cw_ref, cb_ref, wdn_ref, gf_ref,
                o_ref, tail_sc, *, tm):
    @pl.when(pl.program_id(0) == 0)
    def _():
        tail_sc[...] = jnp.zeros(tail_sc.shape, F32)

    gate1 = mod_ref[:, 2 * D_MODEL:3 * D_MODEL]
    shift2 = mod_ref[:, 3 * D_MODEL:4 * D_MODEL]
    scale2 = mod_ref[:, 4 * D_MODEL:5 * D_MODEL]
    gate2 = mod_ref[:, 5 * D_MODEL:6 * D_MODEL]

    mix = (jnp.dot(lru_ref[...], wo_ref[0:LRU_WIDTH, :], preferred_element_type=F32)
           + jnp.dot(attn_ref[...], wo_ref[LRU_WIDTH:, :], preferred_element_type=F32))
    x1 = x_ref[...] + gate1 * mix
    h2 = (_rms_norm(x1, g2_ref[...]) * (1.0 + scale2) + shift2).astype(BF16)

    ff = jnp.zeros((tm, D_MODEL), F32)
    for c in range(D_FF // FFN_CK):
        lo = c * FFN_CK
        a = jnp.dot(h2, wup_ref[:, lo:lo + FFN_CK], preferred_element_type=F32)
        g = jnp.dot(h2, wup_ref[:, D_FF + lo:D_FF + lo + FFN_CK], preferred_element_type=F32)
        prev8 = tail_sc[:, lo:lo + FFN_CK]
        ac = cb_ref[:, lo:lo + FFN_CK]
        for j in range(CONV_FFN):
            sh = CONV_FFN - 1 - j
            a_s = a if sh == 0 else _shift_rows(prev8, a, sh)
            ac = ac + a_s * cw_ref[j:j + 1, lo:lo + FFN_CK]
        tail_sc[:, lo:lo + FFN_CK] = a[tm - SUBLANES:]
        u = (_gelu_tanh(ac) * g).astype(BF16)
        ff = ff + jnp.dot(u, wdn_ref[lo:lo + FFN_CK, :], preferred_element_type=F32)

    x2 = x1 + gate2 * ff
    o_ref[...] = _rms_norm(x2, gf_ref[...])


def _out_ffn(x2d, lru, attn, mod, w_out, g2, w_up, conv_w, conv_b, w_down, g_final):
    s = x2d.shape[0]
    tm = min(FFN_TM, s)
    row = lambda w: pl.BlockSpec((tm, w), lambda i: (i, 0))
    const = lambda shape: pl.BlockSpec(shape, lambda i: (0, 0), pipeline_mode=pl.Buffered(1))
    return pl.pallas_call(
        functools.partial(_ffn_kernel, tm=tm),
        grid=(s // tm,),
        in_specs=[row(D_MODEL), row(LRU_WIDTH), row(ATTN_WIDTH),
                  const((1, 6 * D_MODEL)),
                  const((D_MODEL, D_MODEL)), const((1, D_MODEL)),
                  const((D_MODEL, 2 * D_FF)),
                  const((CONV_FFN, D_FF)), const((1, D_FF)),
                  const((D_FF, D_MODEL)), const((1, D_MODEL))],
        out_specs=row(D_MODEL),
        out_shape=jax.ShapeDtypeStruct((s, D_MODEL), F32),
        scratch_shapes=[pltpu.VMEM((SUBLANES, D_FF), F32)],
        compiler_params=pltpu.CompilerParams(dimension_semantics=("arbitrary",),
                                             vmem_limit_bytes=VMEM_LIMIT),
        name="out_ffn",
    )(x2d, lru, attn, mod, w_out, g2, w_up, conv_w, conv_b, w_down, g_final)


def _block_diag(w):
    nb, bs, _ = w.shape
    eye = jnp.eye(nb, dtype=w.dtype)
    return (w[:, :, None, :] * eye[:, None, :, None]).reshape(nb * bs, nb * bs)


def kernel(x, c, w_ada, b_ada, g_norm1, w_in, conv_lru_w, conv_lru_b, lru_wa, lru_ba, lru_wx, lru_bx, lru_lambda,
           lam_q1, lam_k1, lam_q2, lam_k2, g_subln, w_out, g_norm2, w_up, conv_ffn_w, conv_ffn_b, w_down, rel_bias,
           g_final):
    b, s, d = x.shape
    assert b == 1 and d == D_MODEL and w_ada.shape[0] == 1
    x2d = x.reshape(s, d)
    row = lambda a: a.reshape(1, -1)

    mod = _adaln_mod(c.reshape(d, 1), w_ada[0], row(b_ada[0]))

    tq, tk = min(ATT_TQ, s), min(ATT_TK, s)
    q, kt, v, xr, yg = _in_proj(x2d, mod, row(g_norm1[0]), w_in[0], tk)

    bias = _bias_tiles(rel_bias.T.reshape(-1), tq, tk)
    lamv = jnp.stack([lam_q1[0], lam_k1[0], lam_q2[0], lam_k2[0]])
    attn = _diff_attn(q, kt, v, bias, lamv, row(g_subln[0]), tq, tk)

    w_gates = jnp.concatenate([_block_diag(lru_wa[0]), _block_diag(lru_wx[0])], axis=1).astype(BF16)
    b_gates = jnp.concatenate([lru_ba[0], lru_bx[0]]).reshape(1, -1)
    lru = _rg_lru(xr, yg, conv_lru_w[0], row(conv_lru_b[0]), w_gates, b_gates, row(lru_lambda[0]))

    out = _out_ffn(x2d, lru, attn, mod, w_out[0].astype(BF16), row(g_norm2[0]), w_up[0].astype(BF16),
                   conv_ffn_w[0], row(conv_ffn_b[0]), w_down[0].astype(BF16), row(g_final))
    return out.reshape(b, s, d)
```

```python
import functools
import math

import jax
import jax.numpy as jnp
from jax import lax
from jax.experimental import pallas as pl
from jax.experimental.pallas import tpu as pltpu

F32 = jnp.float32
BF16 = jnp.bfloat16

D_MODEL = 1024
LRU_WIDTH = 512
LRU_BLOCKS = 8
LRU_BLOCK = LRU_WIDTH // LRU_BLOCKS
CONV_LRU = 4
LRU_C = 8.0
N_HEADS = 4
HEAD_DIM = 64
V_DIM = 2 * HEAD_DIM
QK_WIDTH = N_HEADS * 2 * HEAD_DIM
ATTN_WIDTH = N_HEADS * V_DIM
D_IN = 2 * QK_WIDTH + ATTN_WIDTH + 2 * LRU_WIDTH
D_FF = 3 * D_MODEL
CONV_FFN = 3
NUM_BUCKETS = 32
MAX_EXACT = NUM_BUCKETS // 2
MAX_DISTANCE = 128
EPS = 1e-6
NEG_INF = -1e30
LAMBDA_INIT = 0.8 - 0.6 * math.exp(-0.3 * 0)
LOG2E = math.log2(math.e)

LANES = 128
SUBLANES = 8
VMEM_LIMIT = 56 * 1024 * 1024

MOD_TN = 768
ATT_TQ = 512
ATT_TK = 512
LRU_TC = 256
FFN_TM = 512
FFN_CK = 1536


def _rms_norm(x, g):
    y = x * lax.rsqrt(jnp.mean(x * x, axis=-1, keepdims=True) + EPS)
    return y * g


def _gelu_tanh(x):
    cdf = 0.5 * (1.0 + jnp.tanh(math.sqrt(2.0 / math.pi) * (x + 0.044715 * (x * x * x))))
    return x * cdf


def _shift_rows(prev8, x, s):
    ext = jnp.concatenate([prev8, x], axis=0)
    return pltpu.roll(ext, s, 0)[SUBLANES:]


def _adaln_kernel(c_ref, w_ref, b_ref, o_ref):
    c = c_ref[...]
    cond = c * jax.nn.sigmoid(c)
    o_ref[...] = jnp.sum(cond * w_ref[...], axis=0, keepdims=True) + b_ref[...]


def _adaln_mod(c_col, w_ada, b_ada):
    d, n = w_ada.shape
    return pl.pallas_call(
        _adaln_kernel,
        grid=(n // MOD_TN,),
        in_specs=[pl.BlockSpec((d, 1), lambda j: (0, 0)),
                  pl.BlockSpec((d, MOD_TN), lambda j: (0, j)),
                  pl.BlockSpec((1, MOD_TN), lambda j: (0, j))],
        out_specs=pl.BlockSpec((1, MOD_TN), lambda j: (0, j)),
        out_shape=jax.ShapeDtypeStruct((1, n), F32),
        compiler_params=pltpu.CompilerParams(vmem_limit_bytes=VMEM_LIMIT),
        name="adaln_mod",
    )(c_col, w_ada, b_ada)


def _in_proj_kernel(x_ref, mod_ref, g_ref, w_ref, q_ref, k_ref, v_ref, xr_ref, yg_ref):
    x = x_ref[...]
    shift1 = mod_ref[:, 0:D_MODEL]
    scale1 = mod_ref[:, D_MODEL:2 * D_MODEL]
    h = _rms_norm(x, g_ref[...]) * (1.0 + scale1) + shift1
    proj = jnp.dot(h.astype(BF16), w_ref[...].astype(BF16), preferred_element_type=F32)
    ones = jnp.ones((x.shape[0], V_DIM), BF16)
    for hd in range(N_HEADS):
        lo = hd * V_DIM
        q_ref[hd] = (proj[:, lo:lo + V_DIM] * (HEAD_DIM ** -0.5 * LOG2E)).astype(BF16)
        k_ref[hd, 0] = proj[:, QK_WIDTH + lo:QK_WIDTH + lo + V_DIM].T.astype(BF16)
        v_ref[hd] = jnp.concatenate(
            [proj[:, 2 * QK_WIDTH + lo:2 * QK_WIDTH + lo + V_DIM].astype(BF16), ones], axis=1)
    base = 2 * QK_WIDTH + ATTN_WIDTH
    xr_ref[...] = proj[:, base:base + LRU_WIDTH]
    yg_ref[...] = proj[:, base + LRU_WIDTH:base + 2 * LRU_WIDTH]


def _in_proj(x2d, mod, g1, w_in, tm):
    s = x2d.shape[0]
    head_spec = pl.BlockSpec((N_HEADS, tm, V_DIM), lambda i: (0, i, 0))
    row_spec = pl.BlockSpec((tm, LRU_WIDTH), lambda i: (i, 0))
    return pl.pallas_call(
        _in_proj_kernel,
        grid=(s // tm,),
        in_specs=[pl.BlockSpec((tm, D_MODEL), lambda i: (i, 0)),
                  pl.BlockSpec((1, 6 * D_MODEL), lambda i: (0, 0)),
                  pl.BlockSpec((1, D_MODEL), lambda i: (0, 0)),
                  pl.BlockSpec((D_MODEL, D_IN), lambda i: (0, 0), pipeline_mode=pl.Buffered(1))],
        out_specs=[head_spec, pl.BlockSpec((N_HEADS, 1, V_DIM, tm), lambda i: (0, i, 0, 0)),
                   pl.BlockSpec((N_HEADS, tm, 2 * V_DIM), lambda i: (0, i, 0)), row_spec, row_spec],
        out_shape=[jax.ShapeDtypeStruct((N_HEADS, s, V_DIM), BF16),
                   jax.ShapeDtypeStruct((N_HEADS, s // tm, V_DIM, tm), BF16),
                   jax.ShapeDtypeStruct((N_HEADS, s, 2 * V_DIM), BF16)]
        + [jax.ShapeDtypeStruct((s, LRU_WIDTH), F32)] * 2,
        compiler_params=pltpu.CompilerParams(dimension_semantics=("parallel",),
                                             vmem_limit_bytes=VMEM_LIMIT),
        name="in_proj",
    )(x2d, mod, g1, w_in)


def _bias_tiles_kernel(table_ref, o_ref, *, tq, tk, nd):
    hd = pl.program_id(0)
    blk = MAX_DISTANCE
    qpos = lax.broadcasted_iota(jnp.int32, (blk, blk), 0)
    kpos = lax.broadcasted_iota(jnp.int32, (blk, blk), 1)
    far = table_ref[hd * NUM_BUCKETS + NUM_BUCKETS - 1]

    def band_block(offset):
        rel = offset * blk + qpos - kpos
        n = jnp.maximum(rel, 0)
        nf = jnp.maximum(n, 1).astype(F32)
        y = (jnp.log(nf / MAX_EXACT) / math.log(MAX_DISTANCE / MAX_EXACT) * (NUM_BUCKETS - MAX_EXACT))
        val = jnp.zeros((blk, blk), F32)
        for b in range(NUM_BUCKETS):
            hit = (n == b) if b < MAX_EXACT else ((n >= MAX_EXACT) & (y >= b - MAX_EXACT))
            val = jnp.where(hit, (table_ref[hd * NUM_BUCKETS + b] - far) * LOG2E, val)
        return jnp.where(rel >= 0, val, NEG_INF)

    blocks = {0: band_block(0), 1: band_block(1)}
    zeros = jnp.zeros((blk, blk), F32)
    neg = jnp.full((blk, blk), NEG_INF, F32)
    for dd in range(nd):
        for a in range(tq // blk):
            for b in range(tk // blk):
                off = ((dd + 1) * tk - tq) // blk + a - b
                o_ref[0, dd, a * blk:(a + 1) * blk, b * blk:(b + 1) * blk] = (
                    neg if off < 0 else blocks.get(off, zeros))
    o_ref[0, nd] = jnp.zeros((tq, tk), F32)
    o_ref[0, nd + 1] = jnp.full((tq, tk), NEG_INF, F32)


def _bias_tiles(table_flat, tq, tk):
    assert tq % tk == 0 and tk % MAX_DISTANCE == 0
    nd = tq // tk + 1
    return pl.pallas_call(
        functools.partial(_bias_tiles_kernel, tq=tq, tk=tk, nd=nd),
        grid=(N_HEADS,),
        in_specs=[pl.BlockSpec(memory_space=pltpu.SMEM)],
        out_specs=pl.BlockSpec((1, nd + 2, tq, tk), lambda h: (h, 0, 0, 0)),
        out_shape=jax.ShapeDtypeStruct((N_HEADS, nd + 2, tq, tk), F32),
        name="bias_tiles",
    )(table_flat)


def _attn_kernel(q_ref, qn_ref, k_ref, v_ref, bias_ref, lamv_ref, gs_ref, o_ref, qs_sc, sa_sc, sb_sc, pa_sc, pb_sc,
                 m_sc, acc_sc, *, tq, tk, nd):
    i = pl.program_id(1)

    def stack_maps(q):
        lane = lax.broadcasted_iota(jnp.int32, q.shape, 1)
        zero = jnp.zeros_like(q)
        return jnp.concatenate([jnp.where(lane < HEAD_DIM, q, zero), jnp.where(lane >= HEAD_DIM, q, zero)], axis=0)

    qs_sc[0] = stack_maps(q_ref[0])
    qs_sc[1] = stack_maps(qn_ref[0])

    def limits(qi):
        q_start = qi * tq
        n_far = jnp.maximum(q_start - (MAX_DISTANCE - 1), 0) // tk
        last = (q_start + tq - 1) // tk
        return q_start, n_far, last

    _, _, last = limits(i)

    def scores(slot, qi, j, s_ref, part_ref, far_only=False):
        q_start, n_far, last_q = limits(qi)
        jc = jnp.minimum(j, last_q)
        s = jnp.dot(qs_sc[slot], k_ref[0, jc], preferred_element_type=F32)
        if not far_only:
            dd = (q_start + tq - (jc + 1) * tk) // tk
            idx = jnp.where(j > last_q, nd + 1, jnp.where(j < n_far, nd, dd))
            s = (s.reshape(2, tq, tk) + bias_ref[0, idx][None]).reshape(2 * tq, tk)
        s_ref[...] = s
        part_ref[...] = functools.reduce(
            jnp.maximum, [s[:, c * LANES:(c + 1) * LANES] for c in range(tk // LANES)])

    def accumulate(s_ref, part_ref, j):
        jc = jnp.minimum(j, last)
        vb = v_ref[0, pl.ds(pl.multiple_of(jc * tk, tk), tk), :]
        m_prev = m_sc[...]
        m_new = jnp.maximum(m_prev, jnp.max(part_ref[...], axis=1, keepdims=True))
        alpha = jnp.exp2(m_prev - m_new)
        p = jnp.exp2(s_ref[...] - jnp.concatenate([m_new] * (tk // LANES), axis=1))
        acc_sc[...] = (jnp.concatenate([alpha, alpha], axis=1) * acc_sc[...]
                       + jnp.dot(p.astype(BF16), vb, preferred_element_type=F32))
        m_sc[...] = m_new

    @pl.when(i == 0)
    def _():
        scores(0, i, 0, sa_sc, pa_sc)

    m_sc[...] = jnp.full(m_sc.shape, NEG_INF, F32)
    acc_sc[...] = jnp.zeros(acc_sc.shape, F32)

    def two_blocks(j):
        scores(0, i, j + 1, sb_sc, pb_sc)
        accumulate(sa_sc, pa_sc, j)
        to_next = (j + 2 > last).astype(jnp.int32)
        scores(to_next, i + to_next, (1 - to_next) * (j + 2), sa_sc, pa_sc)
        accumulate(sb_sc, pb_sc, j + 1)

    def far_oct_body(t, carry):
        for u in range(4):
            j = 8 * t + 2 * u
            scores(0, i, j + 1, sb_sc, pb_sc, far_only=True)
            accumulate(sa_sc, pa_sc, j)
            scores(0, i, j + 2, sa_sc, pa_sc, far_only=True)
            accumulate(sb_sc, pb_sc, j + 1)
        return carry

    def oct_body(t, carry):
        for u in range(4):
            two_blocks(j0 + 8 * t + 2 * u)
        return carry

    def quad_body(t, carry):
        two_blocks(j0 + 8 * n_octs + 4 * t)
        two_blocks(j0 + 8 * n_octs + 4 * t + 2)
        return carry

    def pair_body(t, carry):
        two_blocks(j0 + 4 * n_quads + 2 * t)
        return carry

    _, n_far, _ = limits(i)
    n_far_octs = jnp.maximum(n_far - 1, 0) // 8
    j0 = 8 * n_far_octs
    n_octs = (last + 1 - j0) // 8
    n_quads = (last + 1 - j0) // 4
    lax.fori_loop(0, n_far_octs, far_oct_body, 0)
    lax.fori_loop(0, n_octs, oct_body, 0)
    lax.fori_loop(0, n_quads - 2 * n_octs, quad_body, 0)
    lax.fori_loop(0, (last + 1 - j0) // 2 - 2 * n_quads, pair_body, 0)

    def single_body(t, carry):
        scores(1, i + 1, 0, sb_sc, pb_sc)
        accumulate(sa_sc, pa_sc, last)
        sa_sc[...] = sb_sc[...]
        pa_sc[...] = pb_sc[...]
        return carry

    lax.fori_loop(0, (last + 1) % 2, single_body, 0)

    lv = lamv_ref[...]
    d1 = jnp.sum(lv[0:1] * lv[1:2], axis=1, keepdims=True)
    d2 = jnp.sum(lv[2:3] * lv[3:4], axis=1, keepdims=True)
    lam = jnp.exp(d1) - jnp.exp(d2) + LAMBDA_INIT

    acc = acc_sc[...]
    out = acc[:, :V_DIM] / acc[:, V_DIM:]
    diff = out[:tq] - lam * out[tq:]
    o_ref[...] = (_rms_norm(diff, gs_ref[...]) * (1.0 - LAMBDA_INIT)).astype(o_ref.dtype)


def _diff_attn(q, kt, v_aug, bias, lamv, g_subln, tq, tk):
    s = q.shape[1]
    nq = s // tq
    nd = bias.shape[1] - 2
    return pl.pallas_call(
        functools.partial(_attn_kernel, tq=tq, tk=tk, nd=nd),
        grid=(N_HEADS, nq),
        in_specs=[pl.BlockSpec((1, tq, V_DIM), lambda h, i: (h, i, 0)),
                  pl.BlockSpec((1, tq, V_DIM), lambda h, i: (h, jnp.minimum(i + 1, nq - 1), 0)),
                  pl.BlockSpec((1, s // tk, V_DIM, tk), lambda h, i: (h, 0, 0, 0)),
                  pl.BlockSpec((1, s, 2 * V_DIM), lambda h, i: (h, 0, 0)),
                  pl.BlockSpec((1, nd + 2, tq, tk), lambda h, i: (h, 0, 0, 0), pipeline_mode=pl.Buffered(1)),
                  pl.BlockSpec((4, HEAD_DIM), lambda h, i: (0, 0)),
                  pl.BlockSpec((1, V_DIM), lambda h, i: (0, 0))],
        out_specs=pl.BlockSpec((tq, V_DIM), lambda h, i: (i, h)),
        out_shape=jax.ShapeDtypeStruct((s, ATTN_WIDTH), BF16),
        scratch_shapes=[pltpu.VMEM((2, 2 * tq, V_DIM), BF16),
                        pltpu.VMEM((2 * tq, tk), F32),
                        pltpu.VMEM((2 * tq, tk), F32),
                        pltpu.VMEM((2 * tq, LANES), F32),
                        pltpu.VMEM((2 * tq, LANES), F32),
                        pltpu.VMEM((2 * tq, LANES), F32),
                        pltpu.VMEM((2 * tq, 2 * V_DIM), F32)],
        compiler_params=pltpu.CompilerParams(dimension_semantics=("arbitrary", "arbitrary"),
                                             vmem_limit_bytes=VMEM_LIMIT),
        name="diff_attn",
    )(q, q, kt, v_aug, bias, lamv, g_subln)


def _lru_kernel(xr_ref, yg_ref, cw_ref, cb_ref, wg_ref, bg_ref, lam_ref, o_ref, tail_sc, h_sc, *, tc):
    @pl.when(pl.program_id(0) == 0)
    def _():
        tail_sc[...] = jnp.zeros(tail_sc.shape, F32)
        h_sc[...] = jnp.zeros(h_sc.shape, F32)

    x = xr_ref[...]
    prev8 = tail_sc[...]
    cw = cw_ref[...]
    xc = cb_ref[...]
    for j in range(CONV_LRU):
        sh = CONV_LRU - 1 - j
        xs = x if sh == 0 else _shift_rows(prev8, x, sh)
        xc = xc + xs * cw[j:j + 1]
    tail_sc[...] = x[tc - SUBLANES:]

    gates = jnp.dot(xc.astype(BF16), wg_ref[...], preferred_element_type=F32) + bg_ref[...]
    sig = 0.5 * jnp.tanh(0.5 * gates) + 0.5
    r = sig[:, :LRU_WIDTH]
    ig = sig[:, LRU_WIDTH:]
    z = -lam_ref[...]
    softplus = jnp.maximum(z, 0.0) + jnp.log1p(jnp.exp(-jnp.abs(z)))
    log_a = -LRU_C * r * softplus
    a = jnp.exp(log_a)
    th = jnp.tanh(log_a)
    b = jnp.sqrt(-2.0 * th / (1.0 - th)) * (ig * xc)

    ng = tc // SUBLANES
    a = a.reshape(ng, SUBLANES, LRU_WIDTH)
    b = b.reshape(ng, SUBLANES, LRU_WIDTH)
    row = lax.broadcasted_iota(jnp.int32, a.shape, 1)
    sh = 1
    while sh < SUBLANES:
        valid = row >= sh
        a_s = jnp.where(valid, pltpu.roll(a, sh, 1), 1.0)
        b_s = jnp.where(valid, pltpu.roll(b, sh, 1), 0.0)
        b = a * b_s + b
        a = a * a_s
        sh *= 2
    carry = h_sc[0:1]
    hs = []
    for g in range(ng):
        hg = a[g] * carry + b[g]
        hs.append(hg)
        carry = hg[SUBLANES - 1:SUBLANES]
    h = jnp.concatenate(hs, axis=0)
    h_sc[0:1] = carry
    o_ref[...] = (h * _gelu_tanh(yg_ref[...])).astype(o_ref.dtype)


def _rg_lru(xr, yg, conv_w, conv_b, w_gates_bf16, b_gates, lam):
    s = xr.shape[0]
    tc = min(LRU_TC, s)
    row_spec = pl.BlockSpec((tc, LRU_WIDTH), lambda i: (i, 0))
    const = lambda shape: pl.BlockSpec(shape, lambda i: (0, 0))
    return pl.pallas_call(
        functools.partial(_lru_kernel, tc=tc),
        grid=(s // tc,),
        in_specs=[row_spec, row_spec,
                  const((CONV_LRU, LRU_WIDTH)), const((1, LRU_WIDTH)),
                  const((LRU_WIDTH, 2 * LRU_WIDTH)), const((1, 2 * LRU_WIDTH)),
                  const((1, LRU_WIDTH))],
        out_specs=row_spec,
        out_shape=jax.ShapeDtypeStruct((s, LRU_WIDTH), BF16),
        scratch_shapes=[pltpu.VMEM((SUBLANES, LRU_WIDTH), F32),
                        pltpu.VMEM((SUBLANES, LRU_WIDTH), F32)],
        compiler_params=pltpu.CompilerParams(dimension_semantics=("arbitrary",),
                                             vmem_limit_bytes=VMEM_LIMIT),
        name="rg_lru",
    )(xr, yg, conv_w, conv_b, w_gates_bf16, b_gates, lam)


def _ffn_kernel(x_ref, lru_ref, attn_ref, mod_ref, wo_ref, g2_ref, wup_ref, cw_ref, cb_ref, wdn_ref, gf_ref,
                o_ref, tail_sc, *, tm):
    @pl.when(pl.program_id(0) == 0)
    def _():
        tail_sc[...] = jnp.zeros(tail_sc.shape, F32)

    gate1 = mod_ref[:, 2 * D_MODEL:3 * D_MODEL]
    shift2 = mod_ref[:, 3 * D_MODEL:4 * D_MODEL]
    scale2 = mod_ref[:, 4 * D_MODEL:5 * D_MODEL]
    gate2 = mod_ref[:, 5 * D_MODEL:6 * D_MODEL]

    mix = (jnp.dot(lru_ref[...], wo_ref[0:LRU_WIDTH, :], preferred_element_type=F32)
           + jnp.dot(attn_ref[...], wo_ref[LRU_WIDTH:, :], preferred_element_type=F32))
    x1 = x_ref[...] + gate1 * mix
    h2 = (_rms_norm(x1, g2_ref[...]) * (1.0 + scale2) + shift2).astype(BF16)

    ff = jnp.zeros((tm, D_MODEL), F32)
    for c in range(D_FF // FFN_CK):
        lo = c * FFN_CK
        a = jnp.dot(h2, wup_ref[:, lo:lo + FFN_CK], preferred_element_type=F32)
        g = jnp.dot(h2, wup_ref[:, D_FF + lo:D_FF + lo + FFN_CK], preferred_element_type=F32)
        prev8 = tail_sc[:, lo:lo + FFN_CK]
        ac = cb_ref[:, lo:lo + FFN_CK]
        for j in range(CONV_FFN):
            sh = CONV_FFN - 1 - j
            a_s = a if sh == 0 else _shift_rows(prev8, a, sh)
            ac = ac + a_s * cw_ref[j:j + 1, lo:lo + FFN_CK]
        tail_sc[:, lo:lo + FFN_CK] = a[tm - SUBLANES:]
        u = (_gelu_tanh(ac) * g).astype(BF16)
        ff = ff + jnp.dot(u, wdn_ref[lo:lo + FFN_CK, :], preferred_element_type=F32)

    x2 = x1 + gate2 * ff
    o_ref[...] = _rms_norm(x2, gf_ref[...])


def _out_ffn(x2d, lru, attn, mod, w_out, g2, w_up, conv_w, conv_b, w_down, g_final):
    s = x2d.shape[0]
    tm = min(FFN_TM, s)
    row = lambda w: pl.BlockSpec((tm, w), lambda i: (i, 0))
    const = lambda shape: pl.BlockSpec(shape, lambda i: (0, 0), pipeline_mode=pl.Buffered(1))
    return pl.pallas_call(
        functools.partial(_ffn_kernel, tm=tm),
        grid=(s // tm,),
        in_specs=[row(D_MODEL), row(LRU_WIDTH), row(ATTN_WIDTH),
                  const((1, 6 * D_MODEL)),
                  const((D_MODEL, D_MODEL)), const((1, D_MODEL)),
                  const((D_MODEL, 2 * D_FF)),
                  const((CONV_FFN, D_FF)), const((1, D_FF)),
                  const((D_FF, D_MODEL)), const((1, D_MODEL))],
        out_specs=row(D_MODEL),
        out_shape=jax.ShapeDtypeStruct((s, D_MODEL), F32),
        scratch_shapes=[pltpu.VMEM((SUBLANES, D_FF), F32)],
        compiler_params=pltpu.CompilerParams(dimension_semantics=("arbitrary",),
                                             vmem_limit_bytes=VMEM_LIMIT),
        name="out_ffn",
    )(x2d, lru, attn, mod, w_out, g2, w_up, conv_w, conv_b, w_down, g_final)


def _block_diag(w):
    nb, bs, _ = w.shape
    eye = jnp.eye(nb, dtype=w.dtype)
    return (w[:, :, None, :] * eye[:, None, :, None]).reshape(nb * bs, nb * bs)


def kernel(x, c, w_ada, b_ada, g_norm1, w_in, conv_lru_w, conv_lru_b, lru_wa, lru_ba, lru_wx, lru_bx, lru_lambda,
           lam_q1, lam_k1, lam_q2, lam_k2, g_subln, w_out, g_norm2, w_up, conv_ffn_w, conv_ffn_b, w_down, rel_bias,
           g_final):
    b, s, d = x.shape
    assert b == 1 and d == D_MODEL and w_ada.shape[0] == 1
    x2d = x.reshape(s, d)
    row = lambda a: a.reshape(1, -1)

    mod = _adaln_mod(c.reshape(d, 1), w_ada[0], row(b_ada[0]))

    tq, tk = min(ATT_TQ, s), min(ATT_TK, s)
    q, kt, v, xr, yg = _in_proj(x2d, mod, row(g_norm1[0]), w_in[0], tk)

    bias = _bias_tiles(rel_bias.T.reshape(-1), tq, tk)
    lamv = jnp.stack([lam_q1[0], lam_k1[0], lam_q2[0], lam_k2[0]])
    attn = _diff_attn(q, kt, v, bias, lamv, row(g_subln[0]), tq, tk)

    w_gates = jnp.concatenate([_block_diag(lru_wa[0]), _block_diag(lru_wx[0])], axis=1).astype(BF16)
    b_gates = jnp.concatenate([lru_ba[0], lru_bx[0]]).reshape(1, -1)
    lru = _rg_lru(xr, yg, conv_lru_w[0], row(conv_lru_b[0]), w_gates, b_gates, row(lru_lambda[0]))

    out = _out_ffn(x2d, lru, attn, mod, w_out[0].astype(BF16), row(g_norm2[0]), w_up[0].astype(BF16),
                   conv_ffn_w[0], row(conv_ffn_b[0]), w_down[0].astype(BF16), row(g_final))
    return out.reshape(b, s, d)
```

```python
import functools
import math

import jax
import jax.numpy as jnp
from jax import lax
from jax.experimental import pallas as pl
from jax.experimental.pallas import tpu as pltpu

F32 = jnp.float32
BF16 = jnp.bfloat16

D_MODEL = 1024
LRU_WIDTH = 512
LRU_BLOCKS = 8
LRU_BLOCK = LRU_WIDTH // LRU_BLOCKS
CONV_LRU = 4
LRU_C = 8.0
N_HEADS = 4
HEAD_DIM = 64
V_DIM = 2 * HEAD_DIM
QK_WIDTH = N_HEADS * 2 * HEAD_DIM
ATTN_WIDTH = N_HEADS * V_DIM
D_IN = 2 * QK_WIDTH + ATTN_WIDTH + 2 * LRU_WIDTH
D_FF = 3 * D_MODEL
CONV_FFN = 3
NUM_BUCKETS = 32
MAX_EXACT = NUM_BUCKETS // 2
MAX_DISTANCE = 128
EPS = 1e-6
NEG_INF = -1e30
LAMBDA_INIT = 0.8 - 0.6 * math.exp(-0.3 * 0)
LOG2E = math.log2(math.e)

LANES = 128
SUBLANES = 8
VMEM_BYTES = 64 * 1024 * 1024
VMEM_LIMIT = VMEM_BYTES - 8 * 1024 * 1024
FFN_VMEM_LIMIT = VMEM_BYTES - 1024 * 1024

MOD_TN = 1536
ATT_TQ = 512
ATT_TK = 512
FFN_TM = 512
FFN_CK = 1536


def _rms_norm(x, g):
    y = x * lax.rsqrt(jnp.mean(x * x, axis=-1, keepdims=True) + EPS)
    return y * g


def _gelu_tanh(x):
    cdf = 0.5 * (1.0 + jnp.tanh(math.sqrt(2.0 / math.pi) * (x + 0.044715 * (x * x * x))))
    return x * cdf


def _shift_rows(prev8, x, s):
    ext = jnp.concatenate([prev8, x], axis=0)
    return pltpu.roll(ext, s, 0)[SUBLANES:]


def _adaln_kernel(c_ref, w_ref, b_ref, o_ref):
    c = c_ref[...]
    cond = c * jax.nn.sigmoid(c)
    o_ref[...] = jnp.sum(cond * w_ref[...], axis=0, keepdims=True) + b_ref[...]


def _adaln_mod(c_col, w_ada, b_ada):
    d, n = w_ada.shape
    return pl.pallas_call(
        _adaln_kernel,
        grid=(n // MOD_TN,),
        in_specs=[pl.BlockSpec((d, 1), lambda j: (0, 0)),
                  pl.BlockSpec((d, MOD_TN), lambda j: (0, j)),
                  pl.BlockSpec((1, MOD_TN), lambda j: (0, j))],
        out_specs=pl.BlockSpec((1, MOD_TN), lambda j: (0, j)),
        out_shape=jax.ShapeDtypeStruct((1, n), F32),
        compiler_params=pltpu.CompilerParams(vmem_limit_bytes=VMEM_LIMIT),
        name="adaln_mod",
    )(c_col, w_ada, b_ada)


def _lru_gates(x, prev8, cw, cb, wg, bg):
    xc = cb
    for j in range(CONV_LRU):
        sh = CONV_LRU - 1 - j
        xs = x if sh == 0 else _shift_rows(prev8, x, sh)
        xc = xc + xs * cw[j:j + 1]
    gates = jnp.dot(xc.astype(BF16), wg, preferred_element_type=F32) + bg
    return xc, gates


def _lru_scan(xc, gates, gate, carry, softplus):
    tc = xc.shape[0]
    sig = 0.5 * jnp.tanh(0.5 * gates) + 0.5
    r = sig[:, :LRU_WIDTH]
    ig = sig[:, LRU_WIDTH:]
    log_a = -LRU_C * r * softplus
    a = jnp.exp(log_a)
    th = jnp.tanh(log_a)
    u = -2.0 * th / (1.0 - th)
    b = jnp.where(u > 0.0, u * lax.rsqrt(u), 0.0) * (ig * xc)

    ng = tc // SUBLANES
    a = a.reshape(ng, SUBLANES, LRU_WIDTH)
    b = b.reshape(ng, SUBLANES, LRU_WIDTH)
    row = lax.broadcasted_iota(jnp.int32, a.shape, 1)
    sh = 1
    while sh < SUBLANES:
        valid = row >= sh
        a_s = jnp.where(valid, pltpu.roll(a, sh, 1), 1.0)
        b_s = jnp.where(valid, pltpu.roll(b, sh, 1), 0.0)
        b = a * b_s + b
        a = a * a_s
        sh *= 2
    hs = []
    for g in range(ng):
        hg = a[g] * carry + b[g]
        hs.append(hg)
        carry = hg[SUBLANES - 1:SUBLANES]
    return jnp.concatenate(hs, axis=0) * gate, carry


def _in_proj_lru_kernel(x_ref, mod_ref, g_ref, w_ref, cw_ref, cb_ref, wg_ref, bg_ref, lam_ref,
                        q_ref, k_ref, v_ref, lru_ref, xr_sc, yg_sc, tail_sc, h_sc, *, tc):
    i = pl.program_id(0)

    @pl.when(i == 0)
    def _():
        xr_sc[...] = jnp.zeros(xr_sc.shape, F32)
        yg_sc[...] = jnp.zeros(yg_sc.shape, F32)

    @pl.when(i <= 1)
    def _():
        tail_sc[...] = jnp.zeros(tail_sc.shape, F32)
        h_sc[...] = jnp.zeros(h_sc.shape, F32)

    tm = xr_sc.shape[0]
    assert tm == 2 * tc
    z = -lam_ref[...]
    softplus = jnp.maximum(z, 0.0) + jnp.log1p(jnp.exp(-jnp.abs(z)))
    lru_w = (cw_ref[...], cb_ref[...], wg_ref[...], bg_ref[...])

    x = x_ref[...]
    shift1 = mod_ref[:, 0:D_MODEL]
    scale1 = mod_ref[:, D_MODEL:2 * D_MODEL]
    h = (_rms_norm(x, g_ref[...]) * (1.0 + scale1) + shift1).astype(BF16)
    ones = jnp.ones((x.shape[0], V_DIM), BF16)

    def proj(lo, width):
        return jnp.dot(h, w_ref[:, lo:lo + width].astype(BF16), preferred_element_type=F32)

    pq = proj(0, QK_WIDTH)
    x0 = xr_sc[0:tc, :]
    xc0, gates0 = _lru_gates(x0, tail_sc[...], *lru_w)
    for hd in range(N_HEADS):
        q_ref[hd] = (pq[:, hd * V_DIM:(hd + 1) * V_DIM] * (HEAD_DIM ** -0.5 * LOG2E)).astype(BF16)

    pk = proj(QK_WIDTH, QK_WIDTH)
    out0, carry = _lru_scan(xc0, gates0, _gelu_tanh(yg_sc[0:tc, :]), h_sc[0:1], softplus)
    lru_ref[0:tc, :] = out0.astype(lru_ref.dtype)
    for hd in range(N_HEADS):
        k_ref[hd, 0] = pk[:, hd * V_DIM:(hd + 1) * V_DIM].T.astype(BF16)

    pv = proj(2 * QK_WIDTH, ATTN_WIDTH)
    x1 = xr_sc[tc:2 * tc, :]
    xc1, gates1 = _lru_gates(x1, x0[tc - SUBLANES:], *lru_w)
    for hd in range(N_HEADS):
        v_ref[hd] = jnp.concatenate([pv[:, hd * V_DIM:(hd + 1) * V_DIM].astype(BF16), ones], axis=1)

    base = 2 * QK_WIDTH + ATTN_WIDTH
    pxr = proj(base, LRU_WIDTH)
    out1, carry = _lru_scan(xc1, gates1, _gelu_tanh(yg_sc[tc:2 * tc, :]), carry, softplus)
    lru_ref[tc:2 * tc, :] = out1.astype(lru_ref.dtype)
    h_sc[0:1] = carry
    tail_sc[...] = x1[tc - SUBLANES:]

    pyg = proj(base + LRU_WIDTH, LRU_WIDTH)
    xr_sc[...] = pxr
    yg_sc[...] = pyg


def _in_proj_lru(x2d, mod, g1, w_in, conv_w, conv_b, w_gates_bf16, b_gates, lam, tm):
    s = x2d.shape[0]
    n = s // tm
    tc = tm // 2
    cur = lambda i: jnp.minimum(i, n - 1)
    const = lambda shape: pl.BlockSpec(shape, lambda i: (0, 0))
    return pl.pallas_call(
        functools.partial(_in_proj_lru_kernel, tc=tc),
        grid=(n + 1,),
        in_specs=[pl.BlockSpec((tm, D_MODEL), lambda i: (cur(i), 0)),
                  const((1, 6 * D_MODEL)), const((1, D_MODEL)),
                  pl.BlockSpec((D_MODEL, D_IN), lambda i: (0, 0), pipeline_mode=pl.Buffered(1)),
                  const((CONV_LRU, LRU_WIDTH)), const((1, LRU_WIDTH)),
                  const((LRU_WIDTH, 2 * LRU_WIDTH)), const((1, 2 * LRU_WIDTH)),
                  const((1, LRU_WIDTH))],
        out_specs=[pl.BlockSpec((N_HEADS, tm, V_DIM), lambda i: (0, cur(i), 0)),
                   pl.BlockSpec((N_HEADS, 1, V_DIM, tm), lambda i: (0, cur(i), 0, 0)),
                   pl.BlockSpec((N_HEADS, tm, 2 * V_DIM), lambda i: (0, cur(i), 0)),
                   pl.BlockSpec((tm, LRU_WIDTH), lambda i: (jnp.maximum(i - 1, 0), 0))],
        out_shape=[jax.ShapeDtypeStruct((N_HEADS, s, V_DIM), BF16),
                   jax.ShapeDtypeStruct((N_HEADS, n, V_DIM, tm), BF16),
                   jax.ShapeDtypeStruct((N_HEADS, s, 2 * V_DIM), BF16),
                   jax.ShapeDtypeStruct((s, LRU_WIDTH), BF16)],
        scratch_shapes=[pltpu.VMEM((tm, LRU_WIDTH), F32),
                        pltpu.VMEM((tm, LRU_WIDTH), F32),
                        pltpu.VMEM((SUBLANES, LRU_WIDTH), F32),
                        pltpu.VMEM((SUBLANES, LRU_WIDTH), F32)],
        compiler_params=pltpu.CompilerParams(dimension_semantics=("arbitrary",),
                                             vmem_limit_bytes=VMEM_LIMIT),
        name="in_proj_lru",
    )(x2d, mod, g1, w_in, conv_w, conv_b, w_gates_bf16, b_gates, lam)


def _bias_tiles_kernel(table_ref, o_ref, *, tq, tk, nd):
    hd = pl.program_id(0)
    blk = MAX_DISTANCE
    qpos = lax.broadcasted_iota(jnp.int32, (blk, blk), 0)
    kpos = lax.broadcasted_iota(jnp.int32, (blk, blk), 1)
    far = table_ref[hd * NUM_BUCKETS + NUM_BUCKETS - 1]

    def band_block(offset):
        rel = offset * blk + qpos - kpos
        n = jnp.maximum(rel, 0)
        nf = jnp.maximum(n, 1).astype(F32)
        y = (jnp.log(nf / MAX_EXACT) / math.log(MAX_DISTANCE / MAX_EXACT) * (NUM_BUCKETS - MAX_EXACT))
        val = jnp.zeros((blk, blk), F32)
        for b in range(NUM_BUCKETS):
            hit = (n == b) if b < MAX_EXACT else ((n >= MAX_EXACT) & (y >= b - MAX_EXACT))
            val = jnp.where(hit, (table_ref[hd * NUM_BUCKETS + b] - far) * LOG2E, val)
        return jnp.where(rel >= 0, val, NEG_INF)

    blocks = {0: band_block(0), 1: band_block(1)}
    zeros = jnp.zeros((blk, blk), F32)
    neg = jnp.full((blk, blk), NEG_INF, F32)
    for dd in range(nd):
        for a in range(tq // blk):
            for b in range(tk // blk):
                off = ((dd + 1) * tk - tq) // blk + a - b
                o_ref[0, dd, a * blk:(a + 1) * blk, b * blk:(b + 1) * blk] = (
                    neg if off < 0 else blocks.get(off, zeros))
    o_ref[0, nd] = jnp.zeros((tq, tk), F32)
    o_ref[0, nd + 1] = jnp.full((tq, tk), NEG_INF, F32)


def _bias_tiles(table_flat, tq, tk):
    assert tq % tk == 0 and tk % MAX_DISTANCE == 0
    nd = tq // tk + 1
    return pl.pallas_call(
        functools.partial(_bias_tiles_kernel, tq=tq, tk=tk, nd=nd),
        grid=(N_HEADS,),
        in_specs=[pl.BlockSpec(memory_space=pltpu.SMEM)],
        out_specs=pl.BlockSpec((1, nd + 2, tq, tk), lambda h: (h, 0, 0, 0)),
        out_shape=jax.ShapeDtypeStruct((N_HEADS, nd + 2, tq, tk), F32),
        name="bias_tiles",
    )(table_flat)


def _attn_kernel(q_ref, qn_ref, k_ref, v_ref, bias_ref, lamv_ref, gs_ref, o_ref, qs_sc, sa_sc, sb_sc, pa_sc, pb_sc,
                 m_sc, acc_sc, *, tq, tk, nd):
    i = pl.program_id(1)

    def stack_maps(q):
        lane = lax.broadcasted_iota(jnp.int32, q.shape, 1)
        zero = jnp.zeros_like(q)
        return jnp.concatenate([jnp.where(lane < HEAD_DIM, q, zero), jnp.where(lane >= HEAD_DIM, q, zero)], axis=0)

    qs_sc[0] = stack_maps(q_ref[0])
    qs_sc[1] = stack_maps(qn_ref[0])

    def limits(qi):
        q_start = qi * tq
        n_far = jnp.maximum(q_start - (MAX_DISTANCE - 1), 0) // tk
        last = (q_start + tq - 1) // tk
        return q_start, n_far, last

    _, _, last = limits(i)

    def scores(slot, qi, j, s_ref, part_ref, far_only=False):
        q_start, n_far, last_q = limits(qi)
        jc = jnp.minimum(j, last_q)
        s = jnp.dot(qs_sc[slot], k_ref[0, jc], preferred_element_type=F32)
        if not far_only:
            dd = (q_start + tq - (jc + 1) * tk) // tk
            idx = jnp.where(j > last_q, nd + 1, jnp.where(j < n_far, nd, dd))
            s = (s.reshape(2, tq, tk) + bias_ref[0, idx][None]).reshape(2 * tq, tk)
        s_ref[...] = s
        part_ref[...] = functools.reduce(
            jnp.maximum, [s[:, c * LANES:(c + 1) * LANES] for c in range(tk // LANES)])

    def accumulate(s_ref, part_ref, j):
        jc = jnp.minimum(j, last)
        vb = v_ref[0, pl.ds(pl.multiple_of(jc * tk, tk), tk), :]
        m_prev = m_sc[...]
        m_new = jnp.maximum(m_prev, jnp.max(part_ref[...], axis=1, keepdims=True))
        alpha = jnp.exp2(m_prev - m_new)
        p = jnp.exp2(s_ref[...] - jnp.concatenate([m_new] * (tk // LANES), axis=1))
        acc_sc[...] = (jnp.concatenate([alpha, alpha], axis=1) * acc_sc[...]
                       + jnp.dot(p.astype(BF16), vb, preferred_element_type=F32))
        m_sc[...] = m_new

    @pl.when(i == 0)
    def _():
        scores(0, i, 0, sa_sc, pa_sc)

    m_sc[...] = jnp.full(m_sc.shape, NEG_INF, F32)
    acc_sc[...] = jnp.zeros(acc_sc.shape, F32)

    def two_blocks(j):
        scores(0, i, j + 1, sb_sc, pb_sc)
        accumulate(sa_sc, pa_sc, j)
        to_next = (j + 2 > last).astype(jnp.int32)
        scores(to_next, i + to_next, (1 - to_next) * (j + 2), sa_sc, pa_sc)
        accumulate(sb_sc, pb_sc, j + 1)

    def far_oct_body(t, carry):
        for u in range(4):
            j = 8 * t + 2 * u
            scores(0, i, j + 1, sb_sc, pb_sc, far_only=True)
            accumulate(sa_sc, pa_sc, j)
            scores(0, i, j + 2, sa_sc, pa_sc, far_only=True)
            accumulate(sb_sc, pb_sc, j + 1)
        return carry

    def oct_body(t, carry):
        for u in range(4):
            two_blocks(j0 + 8 * t + 2 * u)
        return carry

    def quad_body(t, carry):
        two_blocks(j0 + 8 * n_octs + 4 * t)
        two_blocks(j0 + 8 * n_octs + 4 * t + 2)
        return carry

    def pair_body(t, carry):
        two_blocks(j0 + 4 * n_quads + 2 * t)
        return carry

    _, n_far, _ = limits(i)
    n_far_octs = jnp.maximum(n_far - 1, 0) // 8
    j0 = 8 * n_far_octs
    n_octs = (last + 1 - j0) // 8
    n_quads = (last + 1 - j0) // 4
    lax.fori_loop(0, n_far_octs, far_oct_body, 0)
    lax.fori_loop(0, n_octs, oct_body, 0)
    lax.fori_loop(0, n_quads - 2 * n_octs, quad_body, 0)
    lax.fori_loop(0, (last + 1 - j0) // 2 - 2 * n_quads, pair_body, 0)

    def single_body(t, carry):
        scores(1, i + 1, 0, sb_sc, pb_sc)
        accumulate(sa_sc, pa_sc, last)
        sa_sc[...] = sb_sc[...]
        pa_sc[...] = pb_sc[...]
        return carry

    lax.fori_loop(0, (last + 1) % 2, single_body, 0)

    lv = lamv_ref[...]
    d1 = jnp.sum(lv[0:1] * lv[1:2], axis=1, keepdims=True)
    d2 = jnp.sum(lv[2:3] * lv[3:4], axis=1, keepdims=True)
    lam = jnp.exp(d1) - jnp.exp(d2) + LAMBDA_INIT

    acc = acc_sc[...]
    out = acc[:, :V_DIM] / acc[:, V_DIM:]
    diff = out[:tq] - lam * out[tq:]
    o_ref[...] = (_rms_norm(diff, gs_ref[...]) * (1.0 - LAMBDA_INIT)).astype(o_ref.dtype)


def _diff_attn(q, kt, v_aug, bias, lamv, g_subln, tq, tk):
    s = q.shape[1]
    nq = s // tq
    nd = bias.shape[1] - 2
    return pl.pallas_call(
        functools.partial(_attn_kernel, tq=tq, tk=tk, nd=nd),
        grid=(N_HEADS, nq),
        in_specs=[pl.BlockSpec((1, tq, V_DIM), lambda h, i: (h, i, 0)),
                  pl.BlockSpec((1, tq, V_DIM), lambda h, i: (h, jnp.minimum(i + 1, nq - 1), 0)),
                  pl.BlockSpec((1, s // tk, V_DIM, tk), lambda h, i: (h, 0, 0, 0)),
                  pl.BlockSpec((1, s, 2 * V_DIM), lambda h, i: (h, 0, 0)),
                  pl.BlockSpec((1, nd + 2, tq, tk), lambda h, i: (h, 0, 0, 0), pipeline_mode=pl.Buffered(1)),
                  pl.BlockSpec((4, HEAD_DIM), lambda h, i: (0, 0)),
                  pl.BlockSpec((1, V_DIM), lambda h, i: (0, 0))],
        out_specs=pl.BlockSpec((tq, V_DIM), lambda h, i: (i, h)),
        out_shape=jax.ShapeDtypeStruct((s, ATTN_WIDTH), BF16),
        scratch_shapes=[pltpu.VMEM((2, 2 * tq, V_DIM), BF16),
                        pltpu.VMEM((2 * tq, tk), F32),
                        pltpu.VMEM((2 * tq, tk), F32),
                        pltpu.VMEM((2 * tq, LANES), F32),
                        pltpu.VMEM((2 * tq, LANES), F32),
                        pltpu.VMEM((2 * tq, LANES), F32),
                        pltpu.VMEM((2 * tq, 2 * V_DIM), F32)],
        compiler_params=pltpu.CompilerParams(dimension_semantics=("arbitrary", "arbitrary"),
                                             vmem_limit_bytes=VMEM_LIMIT),
        name="diff_attn",
    )(q, q, kt, v_aug, bias, lamv, g_subln)


def _ffn_kernel(x_ref, lru_ref, attn_ref, mod_ref, wo_ref, g2_ref, wup_ref, cw_ref, cb_ref, wdn_ref, gf_ref,
                o_ref, tail_sc, *, tm):
    @pl.when(pl.program_id(0) == 0)
    def _():
        tail_sc[...] = jnp.zeros(tail_sc.shape, F32)

    gate1 = mod_ref[:, 2 * D_MODEL:3 * D_MODEL]
    shift2 = mod_ref[:, 3 * D_MODEL:4 * D_MODEL]
    scale2 = mod_ref[:, 4 * D_MODEL:5 * D_MODEL]
    gate2 = mod_ref[:, 5 * D_MODEL:6 * D_MODEL]

    mix = (jnp.dot(lru_ref[...], wo_ref[0:LRU_WIDTH, :], preferred_element_type=F32)
           + jnp.dot(attn_ref[...], wo_ref[LRU_WIDTH:, :], preferred_element_type=F32))
    x1 = x_ref[...] + gate1 * mix
    h2 = (_rms_norm(x1, g2_ref[...]) * (1.0 + scale2) + shift2).astype(BF16)

    ff = jnp.zeros((tm, D_MODEL), F32)
    for c in range(D_FF // FFN_CK):
        lo = c * FFN_CK
        a = jnp.dot(h2, wup_ref[:, lo:lo + FFN_CK].astype(BF16), preferred_element_type=F32)
        g = jnp.dot(h2, wup_ref[:, D_FF + lo:D_FF + lo + FFN_CK].astype(BF16), preferred_element_type=F32)
        prev8 = tail_sc[:, lo:lo + FFN_CK]
        ac = cb_ref[:, lo:lo + FFN_CK]
        for j in range(CONV_FFN):
            sh = CONV_FFN - 1 - j
            a_s = a if sh == 0 else _shift_rows(prev8, a, sh)
            ac = ac + a_s * cw_ref[j:j + 1, lo:lo + FFN_CK]
        tail_sc[:, lo:lo + FFN_CK] = a[tm - SUBLANES:]
        u = (_gelu_tanh(ac) * g).astype(BF16)
        ff = ff + jnp.dot(u, wdn_ref[lo:lo + FFN_CK, :].astype(BF16), preferred_element_type=F32)

    x2 = x1 + gate2 * ff
    o_ref[...] = _rms_norm(x2, gf_ref[...])


def _out_ffn(x2d, lru, attn, mod, w_out, g2, w_up, conv_w, conv_b, w_down, g_final):
    s = x2d.shape[0]
    tm = min(FFN_TM, s)
    row = lambda w: pl.BlockSpec((tm, w), lambda i: (i, 0))
    const = lambda shape: pl.BlockSpec(shape, lambda i: (0, 0), pipeline_mode=pl.Buffered(1))
    return pl.pallas_call(
        functools.partial(_ffn_kernel, tm=tm),
        grid=(s // tm,),
        in_specs=[row(D_MODEL), row(LRU_WIDTH), row(ATTN_WIDTH),
                  const((1, 6 * D_MODEL)),
                  const((D_MODEL, D_MODEL)), const((1, D_MODEL)),
                  const((D_MODEL, 2 * D_FF)),
                  const((CONV_FFN, D_FF)), const((1, D_FF)),
                  const((D_FF, D_MODEL)), const((1, D_MODEL))],
        out_specs=row(D_MODEL),
        out_shape=jax.ShapeDtypeStruct((s, D_MODEL), F32),
        scratch_shapes=[pltpu.VMEM((SUBLANES, D_FF), F32)],
        compiler_params=pltpu.CompilerParams(dimension_semantics=("arbitrary",),
                                             vmem_limit_bytes=FFN_VMEM_LIMIT),
        name="out_ffn",
    )(x2d, lru, attn, mod, w_out, g2, w_up, conv_w, conv_b, w_down, g_final)


def _block_diag(w):
    nb, bs, _ = w.shape
    eye = jnp.eye(nb, dtype=w.dtype)
    return (w[:, :, None, :] * eye[:, None, :, None]).reshape(nb * bs, nb * bs)


def kernel(x, c, w_ada, b_ada, g_norm1, w_in, conv_lru_w, conv_lru_b, lru_wa, lru_ba, lru_wx, lru_bx, lru_lambda,
           lam_q1, lam_k1, lam_q2, lam_k2, g_subln, w_out, g_norm2, w_up, conv_ffn_w, conv_ffn_b, w_down, rel_bias,
           g_final):
    b, s, d = x.shape
    assert b == 1 and d == D_MODEL and w_ada.shape[0] == 1
    x2d = x.reshape(s, d)
    row = lambda a: a.reshape(1, -1)

    mod = _adaln_mod(c.reshape(d, 1), w_ada[0], row(b_ada[0]))

    tq, tk = min(ATT_TQ, s), min(ATT_TK, s)
    w_gates = jnp.concatenate([_block_diag(lru_wa[0]), _block_diag(lru_wx[0])], axis=1).astype(BF16)
    b_gates = jnp.concatenate([lru_ba[0], lru_bx[0]]).reshape(1, -1)
    q, kt, v, lru = _in_proj_lru(x2d, mod, row(g_norm1[0]), w_in[0], conv_lru_w[0], row(conv_lru_b[0]),
                                 w_gates, b_gates, row(lru_lambda[0]), tk)

    bias = _bias_tiles(rel_bias.T.reshape(-1), tq, tk)
    lamv = jnp.stack([lam_q1[0], lam_k1[0], lam_q2[0], lam_k2[0]])
    attn = _diff_attn(q, kt, v, bias, lamv, row(g_subln[0]), tq, tk)

    out = _out_ffn(x2d, lru, attn, mod, w_out[0].astype(BF16), row(g_norm2[0]), w_up[0],
                   conv_ffn_w[0], row(conv_ffn_b[0]), w_down[0], row(g_final))
    return out.reshape(b, s, d)
```

```python
import functools
import math

import jax
import jax.numpy as jnp
from jax import lax
from jax.experimental import pallas as pl
from jax.experimental.pallas import tpu as pltpu

F32 = jnp.float32
BF16 = jnp.bfloat16

D_MODEL = 1024
LRU_WIDTH = 512
LRU_BLOCKS = 8
LRU_BLOCK = LRU_WIDTH // LRU_BLOCKS
CONV_LRU = 4
LRU_C = 8.0
N_HEADS = 4
HEAD_DIM = 64
V_DIM = 2 * HEAD_DIM
QK_WIDTH = N_HEADS * 2 * HEAD_DIM
ATTN_WIDTH = N_HEADS * V_DIM
D_IN = 2 * QK_WIDTH + ATTN_WIDTH + 2 * LRU_WIDTH
D_FF = 3 * D_MODEL
CONV_FFN = 3
NUM_BUCKETS = 32
MAX_EXACT = NUM_BUCKETS // 2
MAX_DISTANCE = 128
EPS = 1e-6
NEG_INF = -1e30
LAMBDA_INIT = 0.8 - 0.6 * math.exp(-0.3 * 0)
LOG2E = math.log2(math.e)

LANES = 128
SUBLANES = 8
VMEM_BYTES = 64 * 1024 * 1024
VMEM_LIMIT = VMEM_BYTES - 8 * 1024 * 1024
FFN_VMEM_LIMIT = VMEM_BYTES - 1024 * 1024

LRU_BLOCK_T = 32
LRU_PITCH = 40
LANE_COLS = LRU_WIDTH // LANES

MOD_TN = 1536
ATT_TQ = 512
ATT_TK = 512
FFN_TM = 512
FFN_CK = 1536


def _rms_norm(x, g):
    y = x * lax.rsqrt(jnp.mean(x * x, axis=-1, keepdims=True) + EPS)
    return y * g


def _gelu_tanh(x):
    cdf = 0.5 * (1.0 + jnp.tanh(math.sqrt(2.0 / math.pi) * (x + 0.044715 * (x * x * x))))
    return x * cdf


def _shift_rows(prev8, x, s):
    ext = jnp.concatenate([prev8, x], axis=0)
    return pltpu.roll(ext, s, 0)[SUBLANES:]


def _adaln_kernel(c_ref, w_ref, b_ref, o_ref):
    c = c_ref[...]
    cond = c * jax.nn.sigmoid(c)
    o_ref[...] = jnp.sum(cond * w_ref[...], axis=0, keepdims=True) + b_ref[...]


def _adaln_mod(c_col, w_ada, b_ada):
    d, n = w_ada.shape
    return pl.pallas_call(
        _adaln_kernel,
        grid=(n // MOD_TN,),
        in_specs=[pl.BlockSpec((d, 1), lambda j: (0, 0)),
                  pl.BlockSpec((d, MOD_TN), lambda j: (0, j)),
                  pl.BlockSpec((1, MOD_TN), lambda j: (0, j))],
        out_specs=pl.BlockSpec((1, MOD_TN), lambda j: (0, j)),
        out_shape=jax.ShapeDtypeStruct((1, n), F32),
        compiler_params=pltpu.CompilerParams(vmem_limit_bytes=VMEM_LIMIT),
        name="adaln_mod",
    )(c_col, w_ada, b_ada)


def _time_major_rows(ref, base):
    return [jnp.concatenate([ref[c, pl.ds(base + k, SUBLANES, stride=LRU_PITCH), :] for c in range(LANE_COLS)],
                            axis=1) for k in range(LRU_BLOCK_T)]


def _lru_conv_gates(xs, xprev, cw, cb, wg, bg):
    sub = lax.broadcasted_iota(jnp.int32, xs[0].shape, 0)
    back = {}
    for j in range(1, CONV_LRU):
        y = jnp.where(sub == SUBLANES - 1, xprev[CONV_LRU - 1 - j], xs[LRU_BLOCK_T - j])
        back[-j] = pltpu.roll(y, 1, 0)
    at = lambda k: xs[k] if k >= 0 else back[k]
    xcs = []
    for k in range(LRU_BLOCK_T):
        xc = cb
        for j in range(CONV_LRU):
            xc = xc + at(k - (CONV_LRU - 1 - j)) * cw[j:j + 1]
        xcs.append(xc)
    xc = jnp.concatenate(xcs, axis=0)
    gates = jnp.dot(xc.astype(BF16), wg, preferred_element_type=F32) + bg
    return xc, gates


def _lru_recur(xc, gates, gate, f_prev, softplus):
    sig = 0.5 * jnp.tanh(0.5 * gates) + 0.5
    r = sig[:, :LRU_WIDTH]
    ig = sig[:, LRU_WIDTH:]
    log_a = -LRU_C * r * softplus
    a = jnp.exp(log_a)
    th = jnp.tanh(log_a)
    u = -2.0 * th / (1.0 - th)
    b = jnp.where(u > 0.0, u * lax.rsqrt(u), 0.0) * (ig * xc)

    hs, prods = [], []
    for k in range(LRU_BLOCK_T):
        ak = a[k * SUBLANES:(k + 1) * SUBLANES]
        bk = b[k * SUBLANES:(k + 1) * SUBLANES]
        hs.append(bk if k == 0 else ak * hs[-1] + bk)
        prods.append(ak if k == 0 else ak * prods[-1])

    sub = lax.broadcasted_iota(jnp.int32, f_prev.shape, 0)
    start = pltpu.roll(f_prev, 1, 0)
    fa = prods[-1]
    fb = jnp.where(sub == 0, hs[-1] + fa * start, hs[-1])
    sh = 1
    while sh < SUBLANES:
        valid = sub >= sh
        fa_s = jnp.where(valid, pltpu.roll(fa, sh, 0), 1.0)
        fb_s = jnp.where(valid, pltpu.roll(fb, sh, 0), 0.0)
        fb = fa * fb_s + fb
        fa = fa * fa_s
        sh *= 2
    f = fb
    before = jnp.where(sub == 0, start, pltpu.roll(f, 1, 0))
    outs = [(hs[k] + prods[k] * before) * gate[k * SUBLANES:(k + 1) * SUBLANES] for k in range(LRU_BLOCK_T)]
    return outs, f


def _in_proj_lru_kernel(x_ref, mod_ref, g_ref, w_ref, cw_ref, cb_ref, wg_ref, bg_ref, lam_ref,
                        q_ref, k_ref, v_ref, lru_ref, xr_sc, yg_sc, o_sc, tail_sc, f_sc, *, tm):
    i = pl.program_id(0)

    @pl.when(i == 0)
    def _():
        xr_sc[...] = jnp.zeros(xr_sc.shape, F32)
        yg_sc[...] = jnp.zeros(yg_sc.shape, F32)

    @pl.when(i <= 1)
    def _():
        tail_sc[...] = jnp.zeros(tail_sc.shape, F32)
        f_sc[...] = jnp.zeros(f_sc.shape, F32)

    tc = SUBLANES * LRU_BLOCK_T
    assert tm == 2 * tc
    chunk_rows = SUBLANES * LRU_PITCH
    z = -lam_ref[...]
    softplus = jnp.maximum(z, 0.0) + jnp.log1p(jnp.exp(-jnp.abs(z)))
    lru_w = (cw_ref[...], cb_ref[...], wg_ref[...], bg_ref[...])

    x = x_ref[...]
    shift1 = mod_ref[:, 0:D_MODEL]
    scale1 = mod_ref[:, D_MODEL:2 * D_MODEL]
    h = (_rms_norm(x, g_ref[...]) * (1.0 + scale1) + shift1).astype(BF16)
    ones = jnp.ones((x.shape[0], V_DIM), BF16)

    def proj(lo, width):
        return jnp.dot(h, w_ref[:, lo:lo + width].astype(BF16), preferred_element_type=F32)

    def emit(outs, base):
        for k, out in enumerate(outs):
            for c in range(LANE_COLS):
                o_sc[c, pl.ds(base + k, SUBLANES, stride=LRU_PITCH), :] = out[:, c * LANES:(c + 1) * LANES]

    pq = proj(0, QK_WIDTH)
    xs0 = _time_major_rows(xr_sc, 0)
    xprev = [tail_sc[j] for j in range(CONV_LRU - 1)]
    xc0, gates0 = _lru_conv_gates(xs0, xprev, *lru_w)
    for hd in range(N_HEADS):
        q_ref[hd] = (pq[:, hd * V_DIM:(hd + 1) * V_DIM] * (HEAD_DIM ** -0.5 * LOG2E)).astype(BF16)

    pk = proj(QK_WIDTH, QK_WIDTH)
    gate0 = _gelu_tanh(jnp.concatenate(_time_major_rows(yg_sc, 0), axis=0))
    outs0, f0 = _lru_recur(xc0, gates0, gate0, f_sc[...], softplus)
    emit(outs0, 0)
    for hd in range(N_HEADS):
        k_ref[hd, 0] = pk[:, hd * V_DIM:(hd + 1) * V_DIM].T.astype(BF16)

    pv = proj(2 * QK_WIDTH, ATTN_WIDTH)
    xs1 = _time_major_rows(xr_sc, chunk_rows)
    xc1, gates1 = _lru_conv_gates(xs1, xs0[LRU_BLOCK_T - (CONV_LRU - 1):], *lru_w)
    for hd in range(N_HEADS):
        v_ref[hd] = jnp.concatenate([pv[:, hd * V_DIM:(hd + 1) * V_DIM].astype(BF16), ones], axis=1)

    base = 2 * QK_WIDTH + ATTN_WIDTH
    pxr = proj(base, LRU_WIDTH)
    gate1 = _gelu_tanh(jnp.concatenate(_time_major_rows(yg_sc, chunk_rows), axis=0))
    outs1, f1 = _lru_recur(xc1, gates1, gate1, f0, softplus)
    emit(outs1, chunk_rows)
    f_sc[...] = f1
    for j in range(CONV_LRU - 1):
        tail_sc[j] = xs1[LRU_BLOCK_T - (CONV_LRU - 1) + j]

    pyg = proj(base + LRU_WIDTH, LRU_WIDTH)
    for n in range(tm // LRU_BLOCK_T):
        rows = slice(n * LRU_BLOCK_T, (n + 1) * LRU_BLOCK_T)
        pad = slice(n * LRU_PITCH, n * LRU_PITCH + LRU_BLOCK_T)
        for c in range(LANE_COLS):
            cols = slice(c * LANES, (c + 1) * LANES)
            lru_ref[rows, cols] = o_sc[c, pad, :].astype(lru_ref.dtype)
            xr_sc[c, pad, :] = pxr[rows, cols]
            yg_sc[c, pad, :] = pyg[rows, cols]


def _in_proj_lru(x2d, mod, g1, w_in, conv_w, conv_b, w_gates_bf16, b_gates, lam, tm):
    s = x2d.shape[0]
    n = s // tm
    assert tm == 2 * SUBLANES * LRU_BLOCK_T
    pad_rows = (tm // LRU_BLOCK_T) * LRU_PITCH
    cur = lambda i: jnp.minimum(i, n - 1)
    const = lambda shape: pl.BlockSpec(shape, lambda i: (0, 0))
    return pl.pallas_call(
        functools.partial(_in_proj_lru_kernel, tm=tm),
        grid=(n + 1,),
        in_specs=[pl.BlockSpec((tm, D_MODEL), lambda i: (cur(i), 0)),
                  const((1, 6 * D_MODEL)), const((1, D_MODEL)),
                  pl.BlockSpec((D_MODEL, D_IN), lambda i: (0, 0), pipeline_mode=pl.Buffered(1)),
                  const((CONV_LRU, LRU_WIDTH)), const((1, LRU_WIDTH)),
                  const((LRU_WIDTH, 2 * LRU_WIDTH)), const((1, 2 * LRU_WIDTH)),
                  const((1, LRU_WIDTH))],
        out_specs=[pl.BlockSpec((N_HEADS, tm, V_DIM), lambda i: (0, cur(i), 0)),
                   pl.BlockSpec((N_HEADS, 1, V_DIM, tm), lambda i: (0, cur(i), 0, 0)),
                   pl.BlockSpec((N_HEADS, tm, 2 * V_DIM), lambda i: (0, cur(i), 0)),
                   pl.BlockSpec((tm, LRU_WIDTH), lambda i: (jnp.maximum(i - 1, 0), 0))],
        out_shape=[jax.ShapeDtypeStruct((N_HEADS, s, V_DIM), BF16),
                   jax.ShapeDtypeStruct((N_HEADS, n, V_DIM, tm), BF16),
                   jax.ShapeDtypeStruct((N_HEADS, s, 2 * V_DIM), BF16),
                   jax.ShapeDtypeStruct((s, LRU_WIDTH), BF16)],
        scratch_shapes=[pltpu.VMEM((LANE_COLS, pad_rows, LANES), F32),
                        pltpu.VMEM((LANE_COLS, pad_rows, LANES), F32),
                        pltpu.VMEM((LANE_COLS, pad_rows, LANES), F32),
                        pltpu.VMEM((CONV_LRU - 1, SUBLANES, LRU_WIDTH), F32),
                        pltpu.VMEM((SUBLANES, LRU_WIDTH), F32)],
        compiler_params=pltpu.CompilerParams(dimension_semantics=("arbitrary",),
                                             vmem_limit_bytes=VMEM_LIMIT),
        name="in_proj_lru",
    )(x2d, mod, g1, w_in, conv_w, conv_b, w_gates_bf16, b_gates, lam)


def _bias_tiles_kernel(table_ref, o_ref, *, tq, tk, nd):
    hd = pl.program_id(0)
    blk = MAX_DISTANCE
    qpos = lax.broadcasted_iota(jnp.int32, (blk, blk), 0)
    kpos = lax.broadcasted_iota(jnp.int32, (blk, blk), 1)
    far = table_ref[hd * NUM_BUCKETS + NUM_BUCKETS - 1]

    def band_block(offset):
        rel = offset * blk + qpos - kpos
        n = jnp.maximum(rel, 0)
        nf = jnp.maximum(n, 1).astype(F32)
        y = (jnp.log(nf / MAX_EXACT) / math.log(MAX_DISTANCE / MAX_EXACT) * (NUM_BUCKETS - MAX_EXACT))
        val = jnp.zeros((blk, blk), F32)
        for b in range(NUM_BUCKETS):
            hit = (n == b) if b < MAX_EXACT else ((n >= MAX_EXACT) & (y >= b - MAX_EXACT))
            val = jnp.where(hit, (table_ref[hd * NUM_BUCKETS + b] - far) * LOG2E, val)
        return jnp.where(rel >= 0, val, NEG_INF)

    blocks = {0: band_block(0), 1: band_block(1)}
    zeros = jnp.zeros((blk, blk), F32)
    neg = jnp.full((blk, blk), NEG_INF, F32)
    for dd in range(nd):
        for a in range(tq // blk):
            for b in range(tk // blk):
                off = ((dd + 1) * tk - tq) // blk + a - b
                o_ref[0, dd, a * blk:(a + 1) * blk, b * blk:(b + 1) * blk] = (
                    neg if off < 0 else blocks.get(off, zeros))
    o_ref[0, nd] = jnp.zeros((tq, tk), F32)
    o_ref[0, nd + 1] = jnp.full((tq, tk), NEG_INF, F32)


def _bias_tiles(table_flat, tq, tk):
    assert tq % tk == 0 and tk % MAX_DISTANCE == 0
    nd = tq // tk + 1
    return pl.pallas_call(
        functools.partial(_bias_tiles_kernel, tq=tq, tk=tk, nd=nd),
        grid=(N_HEADS,),
        in_specs=[pl.BlockSpec(memory_space=pltpu.SMEM)],
        out_specs=pl.BlockSpec((1, nd + 2, tq, tk), lambda h: (h, 0, 0, 0)),
        out_shape=jax.ShapeDtypeStruct((N_HEADS, nd + 2, tq, tk), F32),
        name="bias_tiles",
    )(table_flat)


def _attn_kernel(q_ref, qn_ref, k_ref, v_ref, bias_ref, lamv_ref, gs_ref, o_ref, qs_sc, sa_sc, sb_sc, pa_sc, pb_sc,
                 m_sc, acc_sc, *, tq, tk, nd):
    i = pl.program_id(1)

    def stack_maps(q):
        lane = lax.broadcasted_iota(jnp.int32, q.shape, 1)
        zero = jnp.zeros_like(q)
        return jnp.concatenate([jnp.where(lane < HEAD_DIM, q, zero), jnp.where(lane >= HEAD_DIM, q, zero)], axis=0)

    qs_sc[0] = stack_maps(q_ref[0])
    qs_sc[1] = stack_maps(qn_ref[0])

    def limits(qi):
        q_start = qi * tq
        n_far = jnp.maximum(q_start - (MAX_DISTANCE - 1), 0) // tk
        last = (q_start + tq - 1) // tk
        return q_start, n_far, last

    _, _, last = limits(i)

    def scores(slot, qi, j, s_ref, part_ref, far_only=False):
        q_start, n_far, last_q = limits(qi)
        jc = jnp.minimum(j, last_q)
        s = jnp.dot(qs_sc[slot], k_ref[0, jc], preferred_element_type=F32)
        if not far_only:
            dd = (q_start + tq - (jc + 1) * tk) // tk
            idx = jnp.where(j > last_q, nd + 1, jnp.where(j < n_far, nd, dd))
            s = (s.reshape(2, tq, tk) + bias_ref[0, idx][None]).reshape(2 * tq, tk)
        s_ref[...] = s
        part_ref[...] = functools.reduce(
            jnp.maximum, [s[:, c * LANES:(c + 1) * LANES] for c in range(tk // LANES)])

    def accumulate(s_ref, part_ref, j):
        jc = jnp.minimum(j, last)
        vb = v_ref[0, pl.ds(pl.multiple_of(jc * tk, tk), tk), :]
        m_prev = m_sc[...]
        m_new = jnp.maximum(m_prev, jnp.max(part_ref[...], axis=1, keepdims=True))
        alpha = jnp.exp2(m_prev - m_new)
        p = jnp.exp2(s_ref[...] - jnp.concatenate([m_new] * (tk // LANES), axis=1))
        acc_sc[...] = (jnp.concatenate([alpha, alpha], axis=1) * acc_sc[...]
                       + jnp.dot(p.astype(BF16), vb, preferred_element_type=F32))
        m_sc[...] = m_new

    @pl.when(i == 0)
    def _():
        scores(0, i, 0, sa_sc, pa_sc)

    m_sc[...] = jnp.full(m_sc.shape, NEG_INF, F32)
    acc_sc[...] = jnp.zeros(acc_sc.shape, F32)

    def two_blocks(j):
        scores(0, i, j + 1, sb_sc, pb_sc)
        accumulate(sa_sc, pa_sc, j)
        to_next = (j + 2 > last).astype(jnp.int32)
        scores(to_next, i + to_next, (1 - to_next) * (j + 2), sa_sc, pa_sc)
        accumulate(sb_sc, pb_sc, j + 1)

    def far_oct_body(t, carry):
        for u in range(4):
            j = 8 * t + 2 * u
            scores(0, i, j + 1, sb_sc, pb_sc, far_only=True)
            accumulate(sa_sc, pa_sc, j)
            scores(0, i, j + 2, sa_sc, pa_sc, far_only=True)
            accumulate(sb_sc, pb_sc, j + 1)
        return carry

    def oct_body(t, carry):
        for u in range(4):
            two_blocks(j0 + 8 * t + 2 * u)
        return carry

    def quad_body(t, carry):
        two_blocks(j0 + 8 * n_octs + 4 * t)
        two_blocks(j0 + 8 * n_octs + 4 * t + 2)
        return carry

    def pair_body(t, carry):
        two_blocks(j0 + 4 * n_quads + 2 * t)
        return carry

    _, n_far, _ = limits(i)
    n_far_octs = jnp.maximum(n_far - 1, 0) // 8
    j0 = 8 * n_far_octs
    n_octs = (last + 1 - j0) // 8
    n_quads = (last + 1 - j0) // 4
    lax.fori_loop(0, n_far_octs, far_oct_body, 0)
    lax.fori_loop(0, n_octs, oct_body, 0)
    lax.fori_loop(0, n_quads - 2 * n_octs, quad_body, 0)
    lax.fori_loop(0, (last + 1 - j0) // 2 - 2 * n_quads, pair_body, 0)

    def single_body(t, carry):
        scores(1, i + 1, 0, sb_sc, pb_sc)
        accumulate(sa_sc, pa_sc, last)
        sa_sc[...] = sb_sc[...]
        pa_sc[...] = pb_sc[...]
        return carry

    lax.fori_loop(0, (last + 1) % 2, single_body, 0)

    lv = lamv_ref[...]
    d1 = jnp.sum(lv[0:1] * lv[1:2], axis=1, keepdims=True)
    d2 = jnp.sum(lv[2:3] * lv[3:4], axis=1, keepdims=True)
    lam = jnp.exp(d1) - jnp.exp(d2) + LAMBDA_INIT

    acc = acc_sc[...]
    out = acc[:, :V_DIM] / acc[:, V_DIM:]
    diff = out[:tq] - lam * out[tq:]
    o_ref[...] = (_rms_norm(diff, gs_ref[...]) * (1.0 - LAMBDA_INIT)).astype(o_ref.dtype)


def _diff_attn(q, kt, v_aug, bias, lamv, g_subln, tq, tk):
    s = q.shape[1]
    nq = s // tq
    nd = bias.shape[1] - 2
    return pl.pallas_call(
        functools.partial(_attn_kernel, tq=tq, tk=tk, nd=nd),
        grid=(N_HEADS, nq),
        in_specs=[pl.BlockSpec((1, tq, V_DIM), lambda h, i: (h, i, 0)),
                  pl.BlockSpec((1, tq, V_DIM), lambda h, i: (h, jnp.minimum(i + 1, nq - 1), 0)),
                  pl.BlockSpec((1, s // tk, V_DIM, tk), lambda h, i: (h, 0, 0, 0)),
                  pl.BlockSpec((1, s, 2 * V_DIM), lambda h, i: (h, 0, 0)),
                  pl.BlockSpec((1, nd + 2, tq, tk), lambda h, i: (h, 0, 0, 0), pipeline_mode=pl.Buffered(1)),
                  pl.BlockSpec((4, HEAD_DIM), lambda h, i: (0, 0)),
                  pl.BlockSpec((1, V_DIM), lambda h, i: (0, 0))],
        out_specs=pl.BlockSpec((tq, V_DIM), lambda h, i: (i, h)),
        out_shape=jax.ShapeDtypeStruct((s, ATTN_WIDTH), BF16),
        scratch_shapes=[pltpu.VMEM((2, 2 * tq, V_DIM), BF16),
                        pltpu.VMEM((2 * tq, tk), F32),
                        pltpu.VMEM((2 * tq, tk), F32),
                        pltpu.VMEM((2 * tq, LANES), F32),
                        pltpu.VMEM((2 * tq, LANES), F32),
                        pltpu.VMEM((2 * tq, LANES), F32),
                        pltpu.VMEM((2 * tq, 2 * V_DIM), F32)],
        compiler_params=pltpu.CompilerParams(dimension_semantics=("arbitrary", "arbitrary"),
                                             vmem_limit_bytes=VMEM_LIMIT),
        name="diff_attn",
    )(q, q, kt, v_aug, bias, lamv, g_subln)


def _ffn_kernel(x_ref, lru_ref, attn_ref, mod_ref, wo_ref, g2_ref, wup_ref, cw_ref, cb_ref, wdn_ref, gf_ref,
                o_ref, tail_sc, *, tm):
    @pl.when(pl.program_id(0) == 0)
    def _():
        tail_sc[...] = jnp.zeros(tail_sc.shape, F32)

    gate1 = mod_ref[:, 2 * D_MODEL:3 * D_MODEL]
    shift2 = mod_ref[:, 3 * D_MODEL:4 * D_MODEL]
    scale2 = mod_ref[:, 4 * D_MODEL:5 * D_MODEL]
    gate2 = mod_ref[:, 5 * D_MODEL:6 * D_MODEL]

    mix = (jnp.dot(lru_ref[...], wo_ref[0:LRU_WIDTH, :], preferred_element_type=F32)
           + jnp.dot(attn_ref[...], wo_ref[LRU_WIDTH:, :], preferred_element_type=F32))
    x1 = x_ref[...] + gate1 * mix
    h2 = (_rms_norm(x1, g2_ref[...]) * (1.0 + scale2) + shift2).astype(BF16)

    ff = jnp.zeros((tm, D_MODEL), F32)
    for c in range(D_FF // FFN_CK):
        lo = c * FFN_CK
        a = jnp.dot(h2, wup_ref[:, lo:lo + FFN_CK].astype(BF16), preferred_element_type=F32)
        g = jnp.dot(h2, wup_ref[:, D_FF + lo:D_FF + lo + FFN_CK].astype(BF16), preferred_element_type=F32)
        prev8 = tail_sc[:, lo:lo + FFN_CK]
        ac = cb_ref[:, lo:lo + FFN_CK]
        for j in range(CONV_FFN):
            sh = CONV_FFN - 1 - j
            a_s = a if sh == 0 else _shift_rows(prev8, a, sh)
            ac = ac + a_s * cw_ref[j:j + 1, lo:lo + FFN_CK]
        tail_sc[:, lo:lo + FFN_CK] = a[tm - SUBLANES:]
        u = (_gelu_tanh(ac) * g).astype(BF16)
        ff = ff + jnp.dot(u, wdn_ref[lo:lo + FFN_CK, :].astype(BF16), preferred_element_type=F32)

    x2 = x1 + gate2 * ff
    o_ref[...] = _rms_norm(x2, gf_ref[...])


def _out_ffn(x2d, lru, attn, mod, w_out, g2, w_up, conv_w, conv_b, w_down, g_final):
    s = x2d.shape[0]
    tm = min(FFN_TM, s)
    row = lambda w: pl.BlockSpec((tm, w), lambda i: (i, 0))
    const = lambda shape: pl.BlockSpec(shape, lambda i: (0, 0), pipeline_mode=pl.Buffered(1))
    return pl.pallas_call(
        functools.partial(_ffn_kernel, tm=tm),
        grid=(s // tm,),
        in_specs=[row(D_MODEL), row(LRU_WIDTH), row(ATTN_WIDTH),
                  const((1, 6 * D_MODEL)),
                  const((D_MODEL, D_MODEL)), const((1, D_MODEL)),
                  const((D_MODEL, 2 * D_FF)),
                  const((CONV_FFN, D_FF)), const((1, D_FF)),
                  const((D_FF, D_MODEL)), const((1, D_MODEL))],
        out_specs=row(D_MODEL),
        out_shape=jax.ShapeDtypeStruct((s, D_MODEL), F32),
        scratch_shapes=[pltpu.VMEM((SUBLANES, D_FF), F32)],
        compiler_params=pltpu.CompilerParams(dimension_semantics=("arbitrary",),
                                             vmem_limit_bytes=FFN_VMEM_LIMIT),
        name="out_ffn",
    )(x2d, lru, attn, mod, w_out, g2, w_up, conv_w, conv_b, w_down, g_final)


def _block_diag(w):
    nb, bs, _ = w.shape
    eye = jnp.eye(nb, dtype=w.dtype)
    return (w[:, :, None, :] * eye[:, None, :, None]).reshape(nb * bs, nb * bs)


def kernel(x, c, w_ada, b_ada, g_norm1, w_in, conv_lru_w, conv_lru_b, lru_wa, lru_ba, lru_wx, lru_bx, lru_lambda,
           lam_q1, lam_k1, lam_q2, lam_k2, g_subln, w_out, g_norm2, w_up, conv_ffn_w, conv_ffn_b, w_down, rel_bias,
           g_final):
    b, s, d = x.shape
    assert b == 1 and d == D_MODEL and w_ada.shape[0] == 1
    x2d = x.reshape(s, d)
    row = lambda a: a.reshape(1, -1)

    mod = _adaln_mod(c.reshape(d, 1), w_ada[0], row(b_ada[0]))

    tq, tk = min(ATT_TQ, s), min(ATT_TK, s)
    w_gates = jnp.concatenate([_block_diag(lru_wa[0]), _block_diag(lru_wx[0])], axis=1).astype(BF16)
    b_gates = jnp.concatenate([lru_ba[0], lru_bx[0]]).reshape(1, -1)
    q, kt, v, lru = _in_proj_lru(x2d, mod, row(g_norm1[0]), w_in[0], conv_lru_w[0], row(conv_lru_b[0]),
                                 w_gates, b_gates, row(lru_lambda[0]), tk)

    bias = _bias_tiles(rel_bias.T.reshape(-1), tq, tk)
    lamv = jnp.stack([lam_q1[0], lam_k1[0], lam_q2[0], lam_k2[0]])
    attn = _diff_attn(q, kt, v, bias, lamv, row(g_subln[0]), tq, tk)

    out = _out_ffn(x2d, lru, attn, mod, w_out[0].astype(BF16), row(g_norm2[0]), w_up[0],
                   conv_ffn_w[0], row(conv_ffn_b[0]), w_down[0], row(g_final))
    return out.reshape(b, s, d)
```

```python
import functools
import math

import jax
import jax.numpy as jnp
from jax import lax
from jax.experimental import pallas as pl
from jax.experimental.pallas import tpu as pltpu

F32 = jnp.float32
BF16 = jnp.bfloat16

D_MODEL = 1024
LRU_WIDTH = 512
LRU_BLOCKS = 8
LRU_BLOCK = LRU_WIDTH // LRU_BLOCKS
CONV_LRU = 4
LRU_C = 8.0
N_HEADS = 4
HEAD_DIM = 64
V_DIM = 2 * HEAD_DIM
QK_WIDTH = N_HEADS * 2 * HEAD_DIM
ATTN_WIDTH = N_HEADS * V_DIM
D_IN = 2 * QK_WIDTH + ATTN_WIDTH + 2 * LRU_WIDTH
D_FF = 3 * D_MODEL
CONV_FFN = 3
NUM_BUCKETS = 32
MAX_EXACT = NUM_BUCKETS // 2
MAX_DISTANCE = 128
EPS = 1e-6
NEG_INF = -1e30
LAMBDA_INIT = 0.8 - 0.6 * math.exp(-0.3 * 0)
LOG2E = math.log2(math.e)

LANES = 128
SUBLANES = 8
VMEM_BYTES = 64 * 1024 * 1024
VMEM_LIMIT = VMEM_BYTES - 8 * 1024 * 1024
FFN_VMEM_LIMIT = VMEM_BYTES - 1024 * 1024

LRU_BLOCK_T = 32
LRU_PITCH = 40
LANE_COLS = LRU_WIDTH // LANES

MOD_TN = 1536
ATT_TQ = 512
ATT_TK = 512
FFN_TM = 512
FFN_CK = 1536


def _rms_norm(x, g):
    y = x * lax.rsqrt(jnp.mean(x * x, axis=-1, keepdims=True) + EPS)
    return y * g


def _gelu_tanh(x):
    cdf = 0.5 * (1.0 + jnp.tanh(math.sqrt(2.0 / math.pi) * (x + 0.044715 * (x * x * x))))
    return x * cdf


def _shift_rows(prev8, x, s):
    ext = jnp.concatenate([prev8, x], axis=0)
    return pltpu.roll(ext, s, 0)[SUBLANES:]


def _adaln_kernel(c_ref, w_ref, b_ref, o_ref):
    c = c_ref[...]
    cond = c * jax.nn.sigmoid(c)
    o_ref[...] = jnp.sum(cond * w_ref[...], axis=0, keepdims=True) + b_ref[...]


def _adaln_mod(c_col, w_ada, b_ada):
    d, n = w_ada.shape
    return pl.pallas_call(
        _adaln_kernel,
        grid=(n // MOD_TN,),
        in_specs=[pl.BlockSpec((d, 1), lambda j: (0, 0)),
                  pl.BlockSpec((d, MOD_TN), lambda j: (0, j)),
                  pl.BlockSpec((1, MOD_TN), lambda j: (0, j))],
        out_specs=pl.BlockSpec((1, MOD_TN), lambda j: (0, j)),
        out_shape=jax.ShapeDtypeStruct((1, n), F32),
        compiler_params=pltpu.CompilerParams(vmem_limit_bytes=VMEM_LIMIT),
        name="adaln_mod",
    )(c_col, w_ada, b_ada)


def _time_major_rows(ref, base):
    return [jnp.concatenate([ref[c, pl.ds(base + k, SUBLANES, stride=LRU_PITCH), :] for c in range(LANE_COLS)],
                            axis=1) for k in range(LRU_BLOCK_T)]


def _lru_conv_gates(xs, xprev, cw, cb, wg, bg):
    sub = lax.broadcasted_iota(jnp.int32, xs[0].shape, 0)
    back = {}
    for j in range(1, CONV_LRU):
        y = jnp.where(sub == SUBLANES - 1, xprev[CONV_LRU - 1 - j], xs[LRU_BLOCK_T - j])
        back[-j] = pltpu.roll(y, 1, 0)
    at = lambda k: xs[k] if k >= 0 else back[k]
    xcs = []
    for k in range(LRU_BLOCK_T):
        xc = cb
        for j in range(CONV_LRU):
            xc = xc + at(k - (CONV_LRU - 1 - j)) * cw[j:j + 1]
        xcs.append(xc)
    xc = jnp.concatenate(xcs, axis=0)
    gates = jnp.dot(xc.astype(BF16), wg, preferred_element_type=F32) + bg
    return xc, gates


def _lru_recur(xc, gates, gate, f_prev, softplus):
    sig = 0.5 * jnp.tanh(0.5 * gates) + 0.5
    r = sig[:, :LRU_WIDTH]
    ig = sig[:, LRU_WIDTH:]
    log_a = -LRU_C * r * softplus
    a = jnp.exp(log_a)
    th = jnp.tanh(log_a)
    u = -2.0 * th / (1.0 - th)
    b = jnp.where(u > 0.0, u * lax.rsqrt(u), 0.0) * (ig * xc)

    hs, prods = [], []
    for k in range(LRU_BLOCK_T):
        ak = a[k * SUBLANES:(k + 1) * SUBLANES]
        bk = b[k * SUBLANES:(k + 1) * SUBLANES]
        hs.append(bk if k == 0 else ak * hs[-1] + bk)
        prods.append(ak if k == 0 else ak * prods[-1])

    sub = lax.broadcasted_iota(jnp.int32, f_prev.shape, 0)
    start = pltpu.roll(f_prev, 1, 0)
    fa = prods[-1]
    fb = jnp.where(sub == 0, hs[-1] + fa * start, hs[-1])
    sh = 1
    while sh < SUBLANES:
        valid = sub >= sh
        fa_s = jnp.where(valid, pltpu.roll(fa, sh, 0), 1.0)
        fb_s = jnp.where(valid, pltpu.roll(fb, sh, 0), 0.0)
        fb = fa * fb_s + fb
        fa = fa * fa_s
        sh *= 2
    f = fb
    before = jnp.where(sub == 0, start, pltpu.roll(f, 1, 0))
    outs = [(hs[k] + prods[k] * before) * gate[k * SUBLANES:(k + 1) * SUBLANES] for k in range(LRU_BLOCK_T)]
    return outs, f


def _in_proj_lru_kernel(x_ref, mod_ref, g_ref, w_ref, cw_ref, cb_ref, wg_ref, bg_ref, lam_ref,
                        q_ref, k_ref, v_ref, lru_ref, xr_sc, yg_sc, o_sc, tail_sc, f_sc, *, tm):
    i = pl.program_id(0)

    @pl.when(i == 0)
    def _():
        xr_sc[...] = jnp.zeros(xr_sc.shape, F32)
        yg_sc[...] = jnp.zeros(yg_sc.shape, F32)

    @pl.when(i <= 1)
    def _():
        tail_sc[...] = jnp.zeros(tail_sc.shape, F32)
        f_sc[...] = jnp.zeros(f_sc.shape, F32)

    tc = SUBLANES * LRU_BLOCK_T
    assert tm == 2 * tc
    chunk_rows = SUBLANES * LRU_PITCH
    z = -lam_ref[...]
    softplus = jnp.maximum(z, 0.0) + jnp.log1p(jnp.exp(-jnp.abs(z)))
    lru_w = (cw_ref[...], cb_ref[...], wg_ref[...], bg_ref[...])

    x = x_ref[...]
    shift1 = mod_ref[:, 0:D_MODEL]
    scale1 = mod_ref[:, D_MODEL:2 * D_MODEL]
    h = (_rms_norm(x, g_ref[...]) * (1.0 + scale1) + shift1).astype(BF16)
    ones = jnp.ones((x.shape[0], V_DIM), BF16)

    def proj(lo, width):
        return jnp.dot(h, w_ref[:, lo:lo + width].astype(BF16), preferred_element_type=F32)

    def emit(outs, base):
        for k, out in enumerate(outs):
            for c in range(LANE_COLS):
                o_sc[c, pl.ds(base + k, SUBLANES, stride=LRU_PITCH), :] = out[:, c * LANES:(c + 1) * LANES]

    half = QK_WIDTH // 2
    pq_a = proj(0, half)
    xs0 = _time_major_rows(xr_sc, 0)
    xprev = [tail_sc[j] for j in range(CONV_LRU - 1)]
    xc0, gates0 = _lru_conv_gates(xs0, xprev, *lru_w)
    pq = jnp.concatenate([pq_a, proj(half, half)], axis=1)
    for hd in range(N_HEADS):
        q_ref[hd] = (pq[:, hd * V_DIM:(hd + 1) * V_DIM] * (HEAD_DIM ** -0.5 * LOG2E)).astype(BF16)

    pk_a = proj(QK_WIDTH, half)
    gate0 = _gelu_tanh(jnp.concatenate(_time_major_rows(yg_sc, 0), axis=0))
    outs0, f0 = _lru_recur(xc0, gates0, gate0, f_sc[...], softplus)
    emit(outs0, 0)
    pk = jnp.concatenate([pk_a, proj(QK_WIDTH + half, half)], axis=1)
    for hd in range(N_HEADS):
        k_ref[hd, 0] = pk[:, hd * V_DIM:(hd + 1) * V_DIM].T.astype(BF16)

    pv_a = proj(2 * QK_WIDTH, half)
    xs1 = _time_major_rows(xr_sc, chunk_rows)
    xc1, gates1 = _lru_conv_gates(xs1, xs0[LRU_BLOCK_T - (CONV_LRU - 1):], *lru_w)
    pv = jnp.concatenate([pv_a, proj(2 * QK_WIDTH + half, half)], axis=1)
    for hd in range(N_HEADS):
        v_ref[hd] = jnp.concatenate([pv[:, hd * V_DIM:(hd + 1) * V_DIM].astype(BF16), ones], axis=1)

    base = 2 * QK_WIDTH + ATTN_WIDTH
    pxr_a = proj(base, half)
    gate1 = _gelu_tanh(jnp.concatenate(_time_major_rows(yg_sc, chunk_rows), axis=0))
    outs1, f1 = _lru_recur(xc1, gates1, gate1, f0, softplus)
    emit(outs1, chunk_rows)
    pxr = jnp.concatenate([pxr_a, proj(base + half, half)], axis=1)
    f_sc[...] = f1
    for j in range(CONV_LRU - 1):
        tail_sc[j] = xs1[LRU_BLOCK_T - (CONV_LRU - 1) + j]

    pyg = proj(base + LRU_WIDTH, LRU_WIDTH)
    for n in range(tm // LRU_BLOCK_T):
        rows = slice(n * LRU_BLOCK_T, (n + 1) * LRU_BLOCK_T)
        pad = slice(n * LRU_PITCH, n * LRU_PITCH + LRU_BLOCK_T)
        for c in range(LANE_COLS):
            cols = slice(c * LANES, (c + 1) * LANES)
            lru_ref[rows, cols] = o_sc[c, pad, :].astype(lru_ref.dtype)
            xr_sc[c, pad, :] = pxr[rows, cols]
            yg_sc[c, pad, :] = pyg[rows, cols]


def _in_proj_lru(x2d, mod, g1, w_in, conv_w, conv_b, w_gates_bf16, b_gates, lam, tm):
    s = x2d.shape[0]
    n = s // tm
    assert tm == 2 * SUBLANES * LRU_BLOCK_T
    pad_rows = (tm // LRU_BLOCK_T) * LRU_PITCH
    cur = lambda i: jnp.minimum(i, n - 1)
    const = lambda shape: pl.BlockSpec(shape, lambda i: (0, 0))
    return pl.pallas_call(
        functools.partial(_in_proj_lru_kernel, tm=tm),
        grid=(n + 1,),
        in_specs=[pl.BlockSpec((tm, D_MODEL), lambda i: (cur(i), 0)),
                  const((1, 6 * D_MODEL)), const((1, D_MODEL)),
                  pl.BlockSpec((D_MODEL, D_IN), lambda i: (0, 0), pipeline_mode=pl.Buffered(1)),
                  const((CONV_LRU, LRU_WIDTH)), const((1, LRU_WIDTH)),
                  const((LRU_WIDTH, 2 * LRU_WIDTH)), const((1, 2 * LRU_WIDTH)),
                  const((1, LRU_WIDTH))],
        out_specs=[pl.BlockSpec((N_HEADS, tm, V_DIM), lambda i: (0, cur(i), 0)),
                   pl.BlockSpec((N_HEADS, 1, V_DIM, tm), lambda i: (0, cur(i), 0, 0)),
                   pl.BlockSpec((N_HEADS, tm, 2 * V_DIM), lambda i: (0, cur(i), 0)),
                   pl.BlockSpec((tm, LRU_WIDTH), lambda i: (jnp.maximum(i - 1, 0), 0))],
        out_shape=[jax.ShapeDtypeStruct((N_HEADS, s, V_DIM), BF16),
                   jax.ShapeDtypeStruct((N_HEADS, n, V_DIM, tm), BF16),
                   jax.ShapeDtypeStruct((N_HEADS, s, 2 * V_DIM), BF16),
                   jax.ShapeDtypeStruct((s, LRU_WIDTH), BF16)],
        scratch_shapes=[pltpu.VMEM((LANE_COLS, pad_rows, LANES), F32),
                        pltpu.VMEM((LANE_COLS, pad_rows, LANES), F32),
                        pltpu.VMEM((LANE_COLS, pad_rows, LANES), F32),
                        pltpu.VMEM((CONV_LRU - 1, SUBLANES, LRU_WIDTH), F32),
                        pltpu.VMEM((SUBLANES, LRU_WIDTH), F32)],
        compiler_params=pltpu.CompilerParams(dimension_semantics=("arbitrary",),
                                             vmem_limit_bytes=VMEM_LIMIT),
        name="in_proj_lru",
    )(x2d, mod, g1, w_in, conv_w, conv_b, w_gates_bf16, b_gates, lam)


def _bias_tiles_kernel(table_ref, o_ref, *, tq, tk, nd):
    hd = pl.program_id(0)
    blk = MAX_DISTANCE
    qpos = lax.broadcasted_iota(jnp.int32, (blk, blk), 0)
    kpos = lax.broadcasted_iota(jnp.int32, (blk, blk), 1)
    far = table_ref[hd * NUM_BUCKETS + NUM_BUCKETS - 1]

    def band_block(offset):
        rel = offset * blk + qpos - kpos
        n = jnp.maximum(rel, 0)
        nf = jnp.maximum(n, 1).astype(F32)
        y = (jnp.log(nf / MAX_EXACT) / math.log(MAX_DISTANCE / MAX_EXACT) * (NUM_BUCKETS - MAX_EXACT))
        val = jnp.zeros((blk, blk), F32)
        for b in range(NUM_BUCKETS):
            hit = (n == b) if b < MAX_EXACT else ((n >= MAX_EXACT) & (y >= b - MAX_EXACT))
            val = jnp.where(hit, (table_ref[hd * NUM_BUCKETS + b] - far) * LOG2E, val)
        return jnp.where(rel >= 0, val, NEG_INF)

    blocks = {0: band_block(0), 1: band_block(1)}
    zeros = jnp.zeros((blk, blk), F32)
    neg = jnp.full((blk, blk), NEG_INF, F32)
    for dd in range(nd):
        for a in range(tq // blk):
            for b in range(tk // blk):
                off = ((dd + 1) * tk - tq) // blk + a - b
                o_ref[0, dd, a * blk:(a + 1) * blk, b * blk:(b + 1) * blk] = (
                    neg if off < 0 else blocks.get(off, zeros))
    o_ref[0, nd] = jnp.zeros((tq, tk), F32)
    o_ref[0, nd + 1] = jnp.full((tq, tk), NEG_INF, F32)


def _bias_tiles(table_flat, tq, tk):
    assert tq % tk == 0 and tk % MAX_DISTANCE == 0
    nd = tq // tk + 1
    return pl.pallas_call(
        functools.partial(_bias_tiles_kernel, tq=tq, tk=tk, nd=nd),
        grid=(N_HEADS,),
        in_specs=[pl.BlockSpec(memory_space=pltpu.SMEM)],
        out_specs=pl.BlockSpec((1, nd + 2, tq, tk), lambda h: (h, 0, 0, 0)),
        out_shape=jax.ShapeDtypeStruct((N_HEADS, nd + 2, tq, tk), F32),
        name="bias_tiles",
    )(table_flat)


def _attn_kernel(q_ref, qn_ref, k_ref, v_ref, bias_ref, lamv_ref, gs_ref, o_ref, qs_sc, sa_sc, sb_sc, pa_sc, pb_sc,
                 m_sc, acc_sc, *, tq, tk, nd):
    i = pl.program_id(1)

    def stack_maps(q):
        lane = lax.broadcasted_iota(jnp.int32, q.shape, 1)
        zero = jnp.zeros_like(q)
        return jnp.concatenate([jnp.where(lane < HEAD_DIM, q, zero), jnp.where(lane >= HEAD_DIM, q, zero)], axis=0)

    qs_sc[0] = stack_maps(q_ref[0])
    qs_sc[1] = stack_maps(qn_ref[0])

    def limits(qi):
        q_start = qi * tq
        n_far = jnp.maximum(q_start - (MAX_DISTANCE - 1), 0) // tk
        last = (q_start + tq - 1) // tk
        return q_start, n_far, last

    _, _, last = limits(i)

    def scores(slot, qi, j, s_ref, part_ref, far_only=False):
        q_start, n_far, last_q = limits(qi)
        jc = jnp.minimum(j, last_q)
        s = jnp.dot(qs_sc[slot], k_ref[0, jc], preferred_element_type=F32)
        if not far_only:
            dd = (q_start + tq - (jc + 1) * tk) // tk
            idx = jnp.where(j > last_q, nd + 1, jnp.where(j < n_far, nd, dd))
            s = (s.reshape(2, tq, tk) + bias_ref[0, idx][None]).reshape(2 * tq, tk)
        s_ref[...] = s
        part_ref[...] = functools.reduce(
            jnp.maximum, [s[:, c * LANES:(c + 1) * LANES] for c in range(tk // LANES)])

    def accumulate(s_ref, part_ref, j):
        jc = jnp.minimum(j, last)
        vb = v_ref[0, pl.ds(pl.multiple_of(jc * tk, tk), tk), :]
        m_prev = m_sc[...]
        m_new = jnp.maximum(m_prev, jnp.max(part_ref[...], axis=1, keepdims=True))
        alpha = jnp.exp2(m_prev - m_new)
        p = jnp.exp2(s_ref[...] - jnp.concatenate([m_new] * (tk // LANES), axis=1))
        acc_sc[...] = (jnp.concatenate([alpha, alpha], axis=1) * acc_sc[...]
                       + jnp.dot(p.astype(BF16), vb, preferred_element_type=F32))
        m_sc[...] = m_new

    @pl.when(i == 0)
    def _():
        scores(0, i, 0, sa_sc, pa_sc)

    m_sc[...] = jnp.full(m_sc.shape, NEG_INF, F32)
    acc_sc[...] = jnp.zeros(acc_sc.shape, F32)

    def two_blocks(j):
        scores(0, i, j + 1, sb_sc, pb_sc)
        accumulate(sa_sc, pa_sc, j)
        to_next = (j + 2 > last).astype(jnp.int32)
        scores(to_next, i + to_next, (1 - to_next) * (j + 2), sa_sc, pa_sc)
        accumulate(sb_sc, pb_sc, j + 1)

    def far_oct_body(t, carry):
        for u in range(4):
            j = 8 * t + 2 * u
            scores(0, i, j + 1, sb_sc, pb_sc, far_only=True)
            accumulate(sa_sc, pa_sc, j)
            scores(0, i, j + 2, sa_sc, pa_sc, far_only=True)
            accumulate(sb_sc, pb_sc, j + 1)
        return carry

    def oct_body(t, carry):
        for u in range(4):
            two_blocks(j0 + 8 * t + 2 * u)
        return carry

    def quad_body(t, carry):
        two_blocks(j0 + 8 * n_octs + 4 * t)
        two_blocks(j0 + 8 * n_octs + 4 * t + 2)
        return carry

    def pair_body(t, carry):
        two_blocks(j0 + 4 * n_quads + 2 * t)
        return carry

    _, n_far, _ = limits(i)
    n_far_octs = jnp.maximum(n_far - 1, 0) // 8
    j0 = 8 * n_far_octs
    n_octs = (last + 1 - j0) // 8
    n_quads = (last + 1 - j0) // 4
    lax.fori_loop(0, n_far_octs, far_oct_body, 0)
    lax.fori_loop(0, n_octs, oct_body, 0)
    lax.fori_loop(0, n_quads - 2 * n_octs, quad_body, 0)
    lax.fori_loop(0, (last + 1 - j0) // 2 - 2 * n_quads, pair_body, 0)

    def single_body(t, carry):
        scores(1, i + 1, 0, sb_sc, pb_sc)
        accumulate(sa_sc, pa_sc, last)
        sa_sc[...] = sb_sc[...]
        pa_sc[...] = pb_sc[...]
        return carry

    lax.fori_loop(0, (last + 1) % 2, single_body, 0)

    lv = lamv_ref[...]
    d1 = jnp.sum(lv[0:1] * lv[1:2], axis=1, keepdims=True)
    d2 = jnp.sum(lv[2:3] * lv[3:4], axis=1, keepdims=True)
    lam = jnp.exp(d1) - jnp.exp(d2) + LAMBDA_INIT

    acc = acc_sc[...]
    out = acc[:, :V_DIM] / acc[:, V_DIM:]
    diff = out[:tq] - lam * out[tq:]
    o_ref[...] = (_rms_norm(diff, gs_ref[...]) * (1.0 - LAMBDA_INIT)).astype(o_ref.dtype)


def _diff_attn(q, kt, v_aug, bias, lamv, g_subln, tq, tk):
    s = q.shape[1]
    nq = s // tq
    nd = bias.shape[1] - 2
    return pl.pallas_call(
        functools.partial(_attn_kernel, tq=tq, tk=tk, nd=nd),
        grid=(N_HEADS, nq),
        in_specs=[pl.BlockSpec((1, tq, V_DIM), lambda h, i: (h, i, 0)),
                  pl.BlockSpec((1, tq, V_DIM), lambda h, i: (h, jnp.minimum(i + 1, nq - 1), 0)),
                  pl.BlockSpec((1, s // tk, V_DIM, tk), lambda h, i: (h, 0, 0, 0)),
                  pl.BlockSpec((1, s, 2 * V_DIM), lambda h, i: (h, 0, 0)),
                  pl.BlockSpec((1, nd + 2, tq, tk), lambda h, i: (h, 0, 0, 0), pipeline_mode=pl.Buffered(1)),
                  pl.BlockSpec((4, HEAD_DIM), lambda h, i: (0, 0)),
                  pl.BlockSpec((1, V_DIM), lambda h, i: (0, 0))],
        out_specs=pl.BlockSpec((tq, V_DIM), lambda h, i: (i, h)),
        out_shape=jax.ShapeDtypeStruct((s, ATTN_WIDTH), BF16),
        scratch_shapes=[pltpu.VMEM((2, 2 * tq, V_DIM), BF16),
                        pltpu.VMEM((2 * tq, tk), F32),
                        pltpu.VMEM((2 * tq, tk), F32),
                        pltpu.VMEM((2 * tq, LANES), F32),
                        pltpu.VMEM((2 * tq, LANES), F32),
                        pltpu.VMEM((2 * tq, LANES), F32),
                        pltpu.VMEM((2 * tq, 2 * V_DIM), F32)],
        compiler_params=pltpu.CompilerParams(dimension_semantics=("arbitrary", "arbitrary"),
                                             vmem_limit_bytes=VMEM_LIMIT),
        name="diff_attn",
    )(q, q, kt, v_aug, bias, lamv, g_subln)


def _ffn_kernel(x_ref, lru_ref, attn_ref, mod_ref, wo_ref, g2_ref, wup_ref, cw_ref, cb_ref, wdn_ref, gf_ref,
                o_ref, tail_sc, *, tm):
    @pl.when(pl.program_id(0) == 0)
    def _():
        tail_sc[...] = jnp.zeros(tail_sc.shape, F32)

    gate1 = mod_ref[:, 2 * D_MODEL:3 * D_MODEL]
    shift2 = mod_ref[:, 3 * D_MODEL:4 * D_MODEL]
    scale2 = mod_ref[:, 4 * D_MODEL:5 * D_MODEL]
    gate2 = mod_ref[:, 5 * D_MODEL:6 * D_MODEL]

    mix = (jnp.dot(lru_ref[...], wo_ref[0:LRU_WIDTH, :], preferred_element_type=F32)
           + jnp.dot(attn_ref[...], wo_ref[LRU_WIDTH:, :], preferred_element_type=F32))
    x1 = x_ref[...] + gate1 * mix
    h2 = (_rms_norm(x1, g2_ref[...]) * (1.0 + scale2) + shift2).astype(BF16)

    ff = jnp.zeros((tm, D_MODEL), F32)
    for c in range(D_FF // FFN_CK):
        lo = c * FFN_CK
        a = jnp.dot(h2, wup_ref[:, lo:lo + FFN_CK].astype(BF16), preferred_element_type=F32)
        g = jnp.dot(h2, wup_ref[:, D_FF + lo:D_FF + lo + FFN_CK].astype(BF16), preferred_element_type=F32)
        prev8 = tail_sc[:, lo:lo + FFN_CK]
        ac = cb_ref[:, lo:lo + FFN_CK]
        for j in range(CONV_FFN):
            sh = CONV_FFN - 1 - j
            a_s = a if sh == 0 else _shift_rows(prev8, a, sh)
            ac = ac + a_s * cw_ref[j:j + 1, lo:lo + FFN_CK]
        tail_sc[:, lo:lo + FFN_CK] = a[tm - SUBLANES:]
        u = (_gelu_tanh(ac) * g).astype(BF16)
        ff = ff + jnp.dot(u, wdn_ref[lo:lo + FFN_CK, :].astype(BF16), preferred_element_type=F32)

    x2 = x1 + gate2 * ff
    o_ref[...] = _rms_norm(x2, gf_ref[...])


def _out_ffn(x2d, lru, attn, mod, w_out, g2, w_up, conv_w, conv_b, w_down, g_final):
    s = x2d.shape[0]
    tm = min(FFN_TM, s)
    row = lambda w: pl.BlockSpec((tm, w), lambda i: (i, 0))
    const = lambda shape: pl.BlockSpec(shape, lambda i: (0, 0), pipeline_mode=pl.Buffered(1))
    return pl.pallas_call(
        functools.partial(_ffn_kernel, tm=tm),
        grid=(s // tm,),
        in_specs=[row(D_MODEL), row(LRU_WIDTH), row(ATTN_WIDTH),
                  const((1, 6 * D_MODEL)),
                  const((D_MODEL, D_MODEL)), const((1, D_MODEL)),
                  const((D_MODEL, 2 * D_FF)),
                  const((CONV_FFN, D_FF)), const((1, D_FF)),
                  const((D_FF, D_MODEL)), const((1, D_MODEL))],
        out_specs=row(D_MODEL),
        out_shape=jax.ShapeDtypeStruct((s, D_MODEL), F32),
        scratch_shapes=[pltpu.VMEM((SUBLANES, D_FF), F32)],
        compiler_params=pltpu.CompilerParams(dimension_semantics=("arbitrary",),
                                             vmem_limit_bytes=FFN_VMEM_LIMIT),
        name="out_ffn",
    )(x2d, lru, attn, mod, w_out, g2, w_up, conv_w, conv_b, w_down, g_final)


def _block_diag(w):
    nb, bs, _ = w.shape
    eye = jnp.eye(nb, dtype=w.dtype)
    return (w[:, :, None, :] * eye[:, None, :, None]).reshape(nb * bs, nb * bs)


def kernel(x, c, w_ada, b_ada, g_norm1, w_in, conv_lru_w, conv_lru_b, lru_wa, lru_ba, lru_wx, lru_bx, lru_lambda,
           lam_q1, lam_k1, lam_q2, lam_k2, g_subln, w_out, g_norm2, w_up, conv_ffn_w, conv_ffn_b, w_down, rel_bias,
           g_final):
    b, s, d = x.shape
    assert b == 1 and d == D_MODEL and w_ada.shape[0] == 1
    x2d = x.reshape(s, d)
    row = lambda a: a.reshape(1, -1)

    mod = _adaln_mod(c.reshape(d, 1), w_ada[0], row(b_ada[0]))

    tq, tk = min(ATT_TQ, s), min(ATT_TK, s)
    w_gates = jnp.concatenate([_block_diag(lru_wa[0]), _block_diag(lru_wx[0])], axis=1).astype(BF16)
    b_gates = jnp.concatenate([lru_ba[0], lru_bx[0]]).reshape(1, -1)
    q, kt, v, lru = _in_proj_lru(x2d, mod, row(g_norm1[0]), w_in[0], conv_lru_w[0], row(conv_lru_b[0]),
                                 w_gates, b_gates, row(lru_lambda[0]), tk)

    bias = _bias_tiles(rel_bias.T.reshape(-1), tq, tk)
    lamv = jnp.stack([lam_q1[0], lam_k1[0], lam_q2[0], lam_k2[0]])
    attn = _diff_attn(q, kt, v, bias, lamv, row(g_subln[0]), tq, tk)

    out = _out_ffn(x2d, lru, attn, mod, w_out[0].astype(BF16), row(g_norm2[0]), w_up[0],
                   conv_ffn_w[0], row(conv_ffn_b[0]), w_down[0], row(g_final))
    return out.reshape(b, s, d)
```

```python
import functools
import math

import jax
import jax.numpy as jnp
from jax import lax
from jax.experimental import pallas as pl
from jax.experimental.pallas import tpu as pltpu

F32 = jnp.float32
BF16 = jnp.bfloat16

D_MODEL = 1024
LRU_WIDTH = 512
LRU_BLOCKS = 8
LRU_BLOCK = LRU_WIDTH // LRU_BLOCKS
CONV_LRU = 4
LRU_C = 8.0
N_HEADS = 4
HEAD_DIM = 64
V_DIM = 2 * HEAD_DIM
QK_WIDTH = N_HEADS * 2 * HEAD_DIM
ATTN_WIDTH = N_HEADS * V_DIM
D_IN = 2 * QK_WIDTH + ATTN_WIDTH + 2 * LRU_WIDTH
D_FF = 3 * D_MODEL
CONV_FFN = 3
NUM_BUCKETS = 32
MAX_EXACT = NUM_BUCKETS // 2
MAX_DISTANCE = 128
EPS = 1e-6
NEG_INF = -1e30
LAMBDA_INIT = 0.8 - 0.6 * math.exp(-0.3 * 0)
LOG2E = math.log2(math.e)

LANES = 128
SUBLANES = 8
VMEM_BYTES = 64 * 1024 * 1024
VMEM_LIMIT = VMEM_BYTES - 8 * 1024 * 1024
FFN_VMEM_LIMIT = VMEM_BYTES - 1024 * 1024

LRU_BLOCK_T = 32
LRU_PITCH = 40
LANE_COLS = LRU_WIDTH // LANES

MOD_TN = 1536
ATT_TQ = 512
ATT_TK = 512
FFN_TM = 512
FFN_CK = 1536


def _rms_norm(x, g):
    y = x * lax.rsqrt(jnp.mean(x * x, axis=-1, keepdims=True) + EPS)
    return y * g


def _gelu_tanh(x):
    cdf = 0.5 * (1.0 + jnp.tanh(math.sqrt(2.0 / math.pi) * (x + 0.044715 * (x * x * x))))
    return x * cdf


def _shift_rows(prev8, x, s):
    ext = jnp.concatenate([prev8, x], axis=0)
    return pltpu.roll(ext, s, 0)[SUBLANES:]


def _adaln_kernel(c_ref, w_ref, b_ref, o_ref):
    c = c_ref[...]
    cond = c * jax.nn.sigmoid(c)
    o_ref[...] = jnp.sum(cond * w_ref[...], axis=0, keepdims=True) + b_ref[...]


def _adaln_mod(c_col, w_ada, b_ada):
    d, n = w_ada.shape
    return pl.pallas_call(
        _adaln_kernel,
        grid=(n // MOD_TN,),
        in_specs=[pl.BlockSpec((d, 1), lambda j: (0, 0)),
                  pl.BlockSpec((d, MOD_TN), lambda j: (0, j)),
                  pl.BlockSpec((1, MOD_TN), lambda j: (0, j))],
        out_specs=pl.BlockSpec((1, MOD_TN), lambda j: (0, j)),
        out_shape=jax.ShapeDtypeStruct((1, n), F32),
        compiler_params=pltpu.CompilerParams(vmem_limit_bytes=VMEM_LIMIT),
        name="adaln_mod",
    )(c_col, w_ada, b_ada)


def _time_major_rows(ref, base):
    return [jnp.concatenate([ref[c, pl.ds(base + k, SUBLANES, stride=LRU_PITCH), :] for c in range(LANE_COLS)],
                            axis=1) for k in range(LRU_BLOCK_T)]


def _lru_conv_gates(xs, xprev, cw, cb, wg, bg):
    sub = lax.broadcasted_iota(jnp.int32, xs[0].shape, 0)
    back = {}
    for j in range(1, CONV_LRU):
        y = jnp.where(sub == SUBLANES - 1, xprev[CONV_LRU - 1 - j], xs[LRU_BLOCK_T - j])
        back[-j] = pltpu.roll(y, 1, 0)
    at = lambda k: xs[k] if k >= 0 else back[k]
    xcs = []
    for k in range(LRU_BLOCK_T):
        xc = cb
        for j in range(CONV_LRU):
            xc = xc + at(k - (CONV_LRU - 1 - j)) * cw[j:j + 1]
        xcs.append(xc)
    xc = jnp.concatenate(xcs, axis=0)
    gates = jnp.dot(xc.astype(BF16), wg, preferred_element_type=F32) + bg
    return xc, gates


def _lru_recur(xc, gates, gate, f_prev, softplus):
    sig = 0.5 * jnp.tanh(0.5 * gates) + 0.5
    r = sig[:, :LRU_WIDTH]
    ig = sig[:, LRU_WIDTH:]
    log_a = -LRU_C * r * softplus
    a = jnp.exp(log_a)
    th = jnp.tanh(log_a)
    u = -2.0 * th / (1.0 - th)
    b = jnp.where(u > 0.0, u * lax.rsqrt(u), 0.0) * (ig * xc)

    hs, prods = [], []
    for k in range(LRU_BLOCK_T):
        ak = a[k * SUBLANES:(k + 1) * SUBLANES]
        bk = b[k * SUBLANES:(k + 1) * SUBLANES]
        hs.append(bk if k == 0 else ak * hs[-1] + bk)
        prods.append(ak if k == 0 else ak * prods[-1])

    sub = lax.broadcasted_iota(jnp.int32, f_prev.shape, 0)
    start = pltpu.roll(f_prev, 1, 0)
    fa = prods[-1]
    fb = jnp.where(sub == 0, hs[-1] + fa * start, hs[-1])
    sh = 1
    while sh < SUBLANES:
        valid = sub >= sh
        fa_s = jnp.where(valid, pltpu.roll(fa, sh, 0), 1.0)
        fb_s = jnp.where(valid, pltpu.roll(fb, sh, 0), 0.0)
        fb = fa * fb_s + fb
        fa = fa * fa_s
        sh *= 2
    f = fb
    before = jnp.where(sub == 0, start, pltpu.roll(f, 1, 0))
    outs = [(hs[k] + prods[k] * before) * gate[k * SUBLANES:(k + 1) * SUBLANES] for k in range(LRU_BLOCK_T)]
    return outs, f


def _in_proj_lru_kernel(x_ref, mod_ref, g_ref, w_ref, cw_ref, cb_ref, wg_ref, bg_ref, lam_ref,
                        q_ref, k_ref, v_ref, lru_ref, w_sc, xr_sc, yg_sc, o_sc, tail_sc, f_sc, *, tm):
    i = pl.program_id(0)

    @pl.when(i == 0)
    def _():
        w_sc[...] = w_ref[...].astype(BF16)
        xr_sc[...] = jnp.zeros(xr_sc.shape, F32)
        yg_sc[...] = jnp.zeros(yg_sc.shape, F32)

    @pl.when(i <= 1)
    def _():
        tail_sc[...] = jnp.zeros(tail_sc.shape, F32)
        f_sc[...] = jnp.zeros(f_sc.shape, F32)

    tc = SUBLANES * LRU_BLOCK_T
    assert tm == 2 * tc
    chunk_rows = SUBLANES * LRU_PITCH
    z = -lam_ref[...]
    softplus = jnp.maximum(z, 0.0) + jnp.log1p(jnp.exp(-jnp.abs(z)))
    lru_w = (cw_ref[...], cb_ref[...], wg_ref[...], bg_ref[...])

    x = x_ref[...]
    shift1 = mod_ref[:, 0:D_MODEL]
    scale1 = mod_ref[:, D_MODEL:2 * D_MODEL]
    h = (_rms_norm(x, g_ref[...]) * (1.0 + scale1) + shift1).astype(BF16)
    ones = jnp.ones((x.shape[0], V_DIM), BF16)

    def proj(lo, width):
        return jnp.dot(h, w_sc[:, lo:lo + width], preferred_element_type=F32)

    def emit(outs, base):
        for k, out in enumerate(outs):
            for c in range(LANE_COLS):
                o_sc[c, pl.ds(base + k, SUBLANES, stride=LRU_PITCH), :] = out[:, c * LANES:(c + 1) * LANES]

    half = QK_WIDTH // 2
    pq_a = proj(0, half)
    xs0 = _time_major_rows(xr_sc, 0)
    xprev = [tail_sc[j] for j in range(CONV_LRU - 1)]
    xc0, gates0 = _lru_conv_gates(xs0, xprev, *lru_w)
    pq = jnp.concatenate([pq_a, proj(half, half)], axis=1)
    for hd in range(N_HEADS):
        q_ref[hd] = (pq[:, hd * V_DIM:(hd + 1) * V_DIM] * (HEAD_DIM ** -0.5 * LOG2E)).astype(BF16)

    pk_a = proj(QK_WIDTH, half)
    gate0 = _gelu_tanh(jnp.concatenate(_time_major_rows(yg_sc, 0), axis=0))
    outs0, f0 = _lru_recur(xc0, gates0, gate0, f_sc[...], softplus)
    emit(outs0, 0)
    pk = jnp.concatenate([pk_a, proj(QK_WIDTH + half, half)], axis=1)
    for hd in range(N_HEADS):
        k_ref[hd, 0] = pk[:, hd * V_DIM:(hd + 1) * V_DIM].T.astype(BF16)

    pv_a = proj(2 * QK_WIDTH, half)
    xs1 = _time_major_rows(xr_sc, chunk_rows)
    xc1, gates1 = _lru_conv_gates(xs1, xs0[LRU_BLOCK_T - (CONV_LRU - 1):], *lru_w)
    pv = jnp.concatenate([pv_a, proj(2 * QK_WIDTH + half, half)], axis=1)
    for hd in range(N_HEADS):
        v_ref[hd] = jnp.concatenate([pv[:, hd * V_DIM:(hd + 1) * V_DIM].astype(BF16), ones], axis=1)

    base = 2 * QK_WIDTH + ATTN_WIDTH
    pxr_a = proj(base, half)
    gate1 = _gelu_tanh(jnp.concatenate(_time_major_rows(yg_sc, chunk_rows), axis=0))
    outs1, f1 = _lru_recur(xc1, gates1, gate1, f0, softplus)
    emit(outs1, chunk_rows)
    pxr = jnp.concatenate([pxr_a, proj(base + half, half)], axis=1)
    f_sc[...] = f1
    for j in range(CONV_LRU - 1):
        tail_sc[j] = xs1[LRU_BLOCK_T - (CONV_LRU - 1) + j]

    pyg = proj(base + LRU_WIDTH, LRU_WIDTH)
    for n in range(tm // LRU_BLOCK_T):
        rows = slice(n * LRU_BLOCK_T, (n + 1) * LRU_BLOCK_T)
        pad = slice(n * LRU_PITCH, n * LRU_PITCH + LRU_BLOCK_T)
        for c in range(LANE_COLS):
            cols = slice(c * LANES, (c + 1) * LANES)
            lru_ref[rows, cols] = o_sc[c, pad, :].astype(lru_ref.dtype)
            xr_sc[c, pad, :] = pxr[rows, cols]
            yg_sc[c, pad, :] = pyg[rows, cols]


def _in_proj_lru(x2d, mod, g1, w_in, conv_w, conv_b, w_gates_bf16, b_gates, lam, tm):
    s = x2d.shape[0]
    n = s // tm
    assert tm == 2 * SUBLANES * LRU_BLOCK_T
    pad_rows = (tm // LRU_BLOCK_T) * LRU_PITCH
    cur = lambda i: jnp.minimum(i, n - 1)
    const = lambda shape: pl.BlockSpec(shape, lambda i: (0, 0))
    return pl.pallas_call(
        functools.partial(_in_proj_lru_kernel, tm=tm),
        grid=(n + 1,),
        in_specs=[pl.BlockSpec((tm, D_MODEL), lambda i: (cur(i), 0)),
                  const((1, 6 * D_MODEL)), const((1, D_MODEL)),
                  pl.BlockSpec((D_MODEL, D_IN), lambda i: (0, 0), pipeline_mode=pl.Buffered(1)),
                  const((CONV_LRU, LRU_WIDTH)), const((1, LRU_WIDTH)),
                  const((LRU_WIDTH, 2 * LRU_WIDTH)), const((1, 2 * LRU_WIDTH)),
                  const((1, LRU_WIDTH))],
        out_specs=[pl.BlockSpec((N_HEADS, tm, V_DIM), lambda i: (0, cur(i), 0)),
                   pl.BlockSpec((N_HEADS, 1, V_DIM, tm), lambda i: (0, cur(i), 0, 0)),
                   pl.BlockSpec((N_HEADS, tm, 2 * V_DIM), lambda i: (0, cur(i), 0)),
                   pl.BlockSpec((tm, LRU_WIDTH), lambda i: (jnp.maximum(i - 1, 0), 0))],
        out_shape=[jax.ShapeDtypeStruct((N_HEADS, s, V_DIM), BF16),
                   jax.ShapeDtypeStruct((N_HEADS, n, V_DIM, tm), BF16),
                   jax.ShapeDtypeStruct((N_HEADS, s, 2 * V_DIM), BF16),
                   jax.ShapeDtypeStruct((s, LRU_WIDTH), BF16)],
        scratch_shapes=[pltpu.VMEM((D_MODEL, D_IN), BF16),
                        pltpu.VMEM((LANE_COLS, pad_rows, LANES), F32),
                        pltpu.VMEM((LANE_COLS, pad_rows, LANES), F32),
                        pltpu.VMEM((LANE_COLS, pad_rows, LANES), F32),
                        pltpu.VMEM((CONV_LRU - 1, SUBLANES, LRU_WIDTH), F32),
                        pltpu.VMEM((SUBLANES, LRU_WIDTH), F32)],
        compiler_params=pltpu.CompilerParams(dimension_semantics=("arbitrary",),
                                             vmem_limit_bytes=VMEM_LIMIT),
        name="in_proj_lru",
    )(x2d, mod, g1, w_in, conv_w, conv_b, w_gates_bf16, b_gates, lam)


def _bias_tiles_kernel(table_ref, o_ref, *, tq, tk, nd):
    hd = pl.program_id(0)
    blk = MAX_DISTANCE
    qpos = lax.broadcasted_iota(jnp.int32, (blk, blk), 0)
    kpos = lax.broadcasted_iota(jnp.int32, (blk, blk), 1)
    far = table_ref[hd * NUM_BUCKETS + NUM_BUCKETS - 1]

    def band_block(offset):
        rel = offset * blk + qpos - kpos
        n = jnp.maximum(rel, 0)
        nf = jnp.maximum(n, 1).astype(F32)
        y = (jnp.log(nf / MAX_EXACT) / math.log(MAX_DISTANCE / MAX_EXACT) * (NUM_BUCKETS - MAX_EXACT))
        val = jnp.zeros((blk, blk), F32)
        for b in range(NUM_BUCKETS):
            hit = (n == b) if b < MAX_EXACT else ((n >= MAX_EXACT) & (y >= b - MAX_EXACT))
            val = jnp.where(hit, (table_ref[hd * NUM_BUCKETS + b] - far) * LOG2E, val)
        return jnp.where(rel >= 0, val, NEG_INF)

    blocks = {0: band_block(0), 1: band_block(1)}
    zeros = jnp.zeros((blk, blk), F32)
    neg = jnp.full((blk, blk), NEG_INF, F32)
    for dd in range(nd):
        for a in range(tq // blk):
            for b in range(tk // blk):
                off = ((dd + 1) * tk - tq) // blk + a - b
                o_ref[0, dd, a * blk:(a + 1) * blk, b * blk:(b + 1) * blk] = (
                    neg if off < 0 else blocks.get(off, zeros))
    o_ref[0, nd] = jnp.zeros((tq, tk), F32)
    o_ref[0, nd + 1] = jnp.full((tq, tk), NEG_INF, F32)


def _bias_tiles(table_flat, tq, tk):
    assert tq % tk == 0 and tk % MAX_DISTANCE == 0
    nd = tq // tk + 1
    return pl.pallas_call(
        functools.partial(_bias_tiles_kernel, tq=tq, tk=tk, nd=nd),
        grid=(N_HEADS,),
        in_specs=[pl.BlockSpec(memory_space=pltpu.SMEM)],
        out_specs=pl.BlockSpec((1, nd + 2, tq, tk), lambda h: (h, 0, 0, 0)),
        out_shape=jax.ShapeDtypeStruct((N_HEADS, nd + 2, tq, tk), F32),
        name="bias_tiles",
    )(table_flat)


def _attn_kernel(q_ref, qn_ref, k_ref, v_ref, bias_ref, lamv_ref, gs_ref, o_ref, qs_sc, sa_sc, sb_sc, pa_sc, pb_sc,
                 m_sc, acc_sc, *, tq, tk, nd):
    i = pl.program_id(1)

    def stack_maps(q):
        lane = lax.broadcasted_iota(jnp.int32, q.shape, 1)
        zero = jnp.zeros_like(q)
        return jnp.concatenate([jnp.where(lane < HEAD_DIM, q, zero), jnp.where(lane >= HEAD_DIM, q, zero)], axis=0)

    qs_sc[0] = stack_maps(q_ref[0])
    qs_sc[1] = stack_maps(qn_ref[0])

    def limits(qi):
        q_start = qi * tq
        n_far = jnp.maximum(q_start - (MAX_DISTANCE - 1), 0) // tk
        last = (q_start + tq - 1) // tk
        return q_start, n_far, last

    _, _, last = limits(i)

    def scores(slot, qi, j, s_ref, part_ref, far_only=False):
        q_start, n_far, last_q = limits(qi)
        jc = jnp.minimum(j, last_q)
        s = jnp.dot(qs_sc[slot], k_ref[0, jc], preferred_element_type=F32)
        if not far_only:
            dd = (q_start + tq - (jc + 1) * tk) // tk
            idx = jnp.where(j > last_q, nd + 1, jnp.where(j < n_far, nd, dd))
            s = (s.reshape(2, tq, tk) + bias_ref[0, idx][None]).reshape(2 * tq, tk)
        s_ref[...] = s
        part_ref[...] = functools.reduce(
            jnp.maximum, [s[:, c * LANES:(c + 1) * LANES] for c in range(tk // LANES)])

    def accumulate(s_ref, part_ref, j):
        jc = jnp.minimum(j, last)
        vb = v_ref[0, pl.ds(pl.multiple_of(jc * tk, tk), tk), :]
        m_prev = m_sc[...]
        m_new = jnp.maximum(m_prev, jnp.max(part_ref[...], axis=1, keepdims=True))
        alpha = jnp.exp2(m_prev - m_new)
        p = jnp.exp2(s_ref[...] - jnp.concatenate([m_new] * (tk // LANES), axis=1))
        acc_sc[...] = (jnp.concatenate([alpha, alpha], axis=1) * acc_sc[...]
                       + jnp.dot(p.astype(BF16), vb, preferred_element_type=F32))
        m_sc[...] = m_new

    @pl.when(i == 0)
    def _():
        scores(0, i, 0, sa_sc, pa_sc)

    m_sc[...] = jnp.full(m_sc.shape, NEG_INF, F32)
    acc_sc[...] = jnp.zeros(acc_sc.shape, F32)

    def two_blocks(j):
        scores(0, i, j + 1, sb_sc, pb_sc)
        accumulate(sa_sc, pa_sc, j)
        to_next = (j + 2 > last).astype(jnp.int32)
        scores(to_next, i + to_next, (1 - to_next) * (j + 2), sa_sc, pa_sc)
        accumulate(sb_sc, pb_sc, j + 1)

    def far_oct_body(t, carry):
        for u in range(4):
            j = 8 * t + 2 * u
            scores(0, i, j + 1, sb_sc, pb_sc, far_only=True)
            accumulate(sa_sc, pa_sc, j)
            scores(0, i, j + 2, sa_sc, pa_sc, far_only=True)
            accumulate(sb_sc, pb_sc, j + 1)
        return carry

    def oct_body(t, carry):
        for u in range(4):
            two_blocks(j0 + 8 * t + 2 * u)
        return carry

    def quad_body(t, carry):
        two_blocks(j0 + 8 * n_octs + 4 * t)
        two_blocks(j0 + 8 * n_octs + 4 * t + 2)
        return carry

    def pair_body(t, carry):
        two_blocks(j0 + 4 * n_quads + 2 * t)
        return carry

    _, n_far, _ = limits(i)
    n_far_octs = jnp.maximum(n_far - 1, 0) // 8
    j0 = 8 * n_far_octs
    n_octs = (last + 1 - j0) // 8
    n_quads = (last + 1 - j0) // 4
    lax.fori_loop(0, n_far_octs, far_oct_body, 0)
    lax.fori_loop(0, n_octs, oct_body, 0)
    lax.fori_loop(0, n_quads - 2 * n_octs, quad_body, 0)
    lax.fori_loop(0, (last + 1 - j0) // 2 - 2 * n_quads, pair_body, 0)

    def single_body(t, carry):
        scores(1, i + 1, 0, sb_sc, pb_sc)
        accumulate(sa_sc, pa_sc, last)
        sa_sc[...] = sb_sc[...]
        pa_sc[...] = pb_sc[...]
        return carry

    lax.fori_loop(0, (last + 1) % 2, single_body, 0)

    lv = lamv_ref[...]
    d1 = jnp.sum(lv[0:1] * lv[1:2], axis=1, keepdims=True)
    d2 = jnp.sum(lv[2:3] * lv[3:4], axis=1, keepdims=True)
    lam = jnp.exp(d1) - jnp.exp(d2) + LAMBDA_INIT

    acc = acc_sc[...]
    out = acc[:, :V_DIM] / acc[:, V_DIM:]
    diff = out[:tq] - lam * out[tq:]
    o_ref[...] = (_rms_norm(diff, gs_ref[...]) * (1.0 - LAMBDA_INIT)).astype(o_ref.dtype)


def _diff_attn(q, kt, v_aug, bias, lamv, g_subln, tq, tk):
    s = q.shape[1]
    nq = s // tq
    nd = bias.shape[1] - 2
    return pl.pallas_call(
        functools.partial(_attn_kernel, tq=tq, tk=tk, nd=nd),
        grid=(N_HEADS, nq),
        in_specs=[pl.BlockSpec((1, tq, V_DIM), lambda h, i: (h, i, 0)),
                  pl.BlockSpec((1, tq, V_DIM), lambda h, i: (h, jnp.minimum(i + 1, nq - 1), 0)),
                  pl.BlockSpec((1, s // tk, V_DIM, tk), lambda h, i: (h, 0, 0, 0)),
                  pl.BlockSpec((1, s, 2 * V_DIM), lambda h, i: (h, 0, 0)),
                  pl.BlockSpec((1, nd + 2, tq, tk), lambda h, i: (h, 0, 0, 0), pipeline_mode=pl.Buffered(1)),
                  pl.BlockSpec((4, HEAD_DIM), lambda h, i: (0, 0)),
                  pl.BlockSpec((1, V_DIM), lambda h, i: (0, 0))],
        out_specs=pl.BlockSpec((tq, V_DIM), lambda h, i: (i, h)),
        out_shape=jax.ShapeDtypeStruct((s, ATTN_WIDTH), BF16),
        scratch_shapes=[pltpu.VMEM((2, 2 * tq, V_DIM), BF16),
                        pltpu.VMEM((2 * tq, tk), F32),
                        pltpu.VMEM((2 * tq, tk), F32),
                        pltpu.VMEM((2 * tq, LANES), F32),
                        pltpu.VMEM((2 * tq, LANES), F32),
                        pltpu.VMEM((2 * tq, LANES), F32),
                        pltpu.VMEM((2 * tq, 2 * V_DIM), F32)],
        compiler_params=pltpu.CompilerParams(dimension_semantics=("arbitrary", "arbitrary"),
                                             vmem_limit_bytes=VMEM_LIMIT),
        name="diff_attn",
    )(q, q, kt, v_aug, bias, lamv, g_subln)


def _ffn_kernel(x_ref, lru_ref, attn_ref, mod_ref, wo_ref, g2_ref, wup_ref, cw_ref, cb_ref, wdn_ref, gf_ref,
                o_ref, tail_sc, *, tm):
    @pl.when(pl.program_id(0) == 0)
    def _():
        tail_sc[...] = jnp.zeros(tail_sc.shape, F32)

    gate1 = mod_ref[:, 2 * D_MODEL:3 * D_MODEL]
    shift2 = mod_ref[:, 3 * D_MODEL:4 * D_MODEL]
    scale2 = mod_ref[:, 4 * D_MODEL:5 * D_MODEL]
    gate2 = mod_ref[:, 5 * D_MODEL:6 * D_MODEL]

    mix = (jnp.dot(lru_ref[...], wo_ref[0:LRU_WIDTH, :], preferred_element_type=F32)
           + jnp.dot(attn_ref[...], wo_ref[LRU_WIDTH:, :], preferred_element_type=F32))
    x1 = x_ref[...] + gate1 * mix
    h2 = (_rms_norm(x1, g2_ref[...]) * (1.0 + scale2) + shift2).astype(BF16)

    ff = jnp.zeros((tm, D_MODEL), F32)
    for c in range(D_FF // FFN_CK):
        lo = c * FFN_CK
        a = jnp.dot(h2, wup_ref[:, lo:lo + FFN_CK].astype(BF16), preferred_element_type=F32)
        g = jnp.dot(h2, wup_ref[:, D_FF + lo:D_FF + lo + FFN_CK].astype(BF16), preferred_element_type=F32)
        prev8 = tail_sc[:, lo:lo + FFN_CK]
        ac = cb_ref[:, lo:lo + FFN_CK]
        for j in range(CONV_FFN):
            sh = CONV_FFN - 1 - j
            a_s = a if sh == 0 else _shift_rows(prev8, a, sh)
            ac = ac + a_s * cw_ref[j:j + 1, lo:lo + FFN_CK]
        tail_sc[:, lo:lo + FFN_CK] = a[tm - SUBLANES:]
        u = (_gelu_tanh(ac) * g).astype(BF16)
        ff = ff + jnp.dot(u, wdn_ref[lo:lo + FFN_CK, :].astype(BF16), preferred_element_type=F32)

    x2 = x1 + gate2 * ff
    o_ref[...] = _rms_norm(x2, gf_ref[...])


def _out_ffn(x2d, lru, attn, mod, w_out, g2, w_up, conv_w, conv_b, w_down, g_final):
    s = x2d.shape[0]
    tm = min(FFN_TM, s)
    row = lambda w: pl.BlockSpec((tm, w), lambda i: (i, 0))
    const = lambda shape: pl.BlockSpec(shape, lambda i: (0, 0), pipeline_mode=pl.Buffered(1))
    return pl.pallas_call(
        functools.partial(_ffn_kernel, tm=tm),
        grid=(s // tm,),
        in_specs=[row(D_MODEL), row(LRU_WIDTH), row(ATTN_WIDTH),
                  const((1, 6 * D_MODEL)),
                  const((D_MODEL, D_MODEL)), const((1, D_MODEL)),
                  const((D_MODEL, 2 * D_FF)),
                  const((CONV_FFN, D_FF)), const((1, D_FF)),
                  const((D_FF, D_MODEL)), const((1, D_MODEL))],
        out_specs=row(D_MODEL),
        out_shape=jax.ShapeDtypeStruct((s, D_MODEL), F32),
        scratch_shapes=[pltpu.VMEM((SUBLANES, D_FF), F32)],
        compiler_params=pltpu.CompilerParams(dimension_semantics=("arbitrary",),
                                             vmem_limit_bytes=FFN_VMEM_LIMIT),
        name="out_ffn",
    )(x2d, lru, attn, mod, w_out, g2, w_up, conv_w, conv_b, w_down, g_final)


def _block_diag(w):
    nb, bs, _ = w.shape
    eye = jnp.eye(nb, dtype=w.dtype)
    return (w[:, :, None, :] * eye[:, None, :, None]).reshape(nb * bs, nb * bs)


def kernel(x, c, w_ada, b_ada, g_norm1, w_in, conv_lru_w, conv_lru_b, lru_wa, lru_ba, lru_wx, lru_bx, lru_lambda,
           lam_q1, lam_k1, lam_q2, lam_k2, g_subln, w_out, g_norm2, w_up, conv_ffn_w, conv_ffn_b, w_down, rel_bias,
           g_final):
    b, s, d = x.shape
    assert b == 1 and d == D_MODEL and w_ada.shape[0] == 1
    x2d = x.reshape(s, d)
    row = lambda a: a.reshape(1, -1)

    mod = _adaln_mod(c.reshape(d, 1), w_ada[0], row(b_ada[0]))

    tq, tk = min(ATT_TQ, s), min(ATT_TK, s)
    w_gates = jnp.concatenate([_block_diag(lru_wa[0]), _block_diag(lru_wx[0])], axis=1).astype(BF16)
    b_gates = jnp.concatenate([lru_ba[0], lru_bx[0]]).reshape(1, -1)
    q, kt, v, lru = _in_proj_lru(x2d, mod, row(g_norm1[0]), w_in[0], conv_lru_w[0], row(conv_lru_b[0]),
                                 w_gates, b_gates, row(lru_lambda[0]), tk)

    bias = _bias_tiles(rel_bias.T.reshape(-1), tq, tk)
    lamv = jnp.stack([lam_q1[0], lam_k1[0], lam_q2[0], lam_k2[0]])
    attn = _diff_attn(q, kt, v, bias, lamv, row(g_subln[0]), tq, tk)

    out = _out_ffn(x2d, lru, attn, mod, w_out[0].astype(BF16), row(g_norm2[0]), w_up[0],
                   conv_ffn_w[0], row(conv_ffn_b[0]), w_down[0], row(g_final))
    return out.reshape(b, s, d)
```

```python
import functools
import math

import jax
import jax.numpy as jnp
from jax import lax
from jax.experimental import pallas as pl
from jax.experimental.pallas import tpu as pltpu

F32 = jnp.float32
BF16 = jnp.bfloat16

D_MODEL = 1024
LRU_WIDTH = 512
LRU_BLOCKS = 8
LRU_BLOCK = LRU_WIDTH // LRU_BLOCKS
CONV_LRU = 4
LRU_C = 8.0
N_HEADS = 4
HEAD_DIM = 64
V_DIM = 2 * HEAD_DIM
QK_WIDTH = N_HEADS * 2 * HEAD_DIM
ATTN_WIDTH = N_HEADS * V_DIM
D_IN = 2 * QK_WIDTH + ATTN_WIDTH + 2 * LRU_WIDTH
D_FF = 3 * D_MODEL
CONV_FFN = 3
NUM_BUCKETS = 32
MAX_EXACT = NUM_BUCKETS // 2
MAX_DISTANCE = 128
EPS = 1e-6
NEG_INF = -1e30
LAMBDA_INIT = 0.8 - 0.6 * math.exp(-0.3 * 0)
LOG2E = math.log2(math.e)

LANES = 128
SUBLANES = 8
VMEM_BYTES = 64 * 1024 * 1024
VMEM_LIMIT = VMEM_BYTES - 8 * 1024 * 1024
FFN_VMEM_LIMIT = VMEM_BYTES - 1024 * 1024

LRU_BLOCK_T = 32
LRU_PITCH = 40
LANE_COLS = LRU_WIDTH // LANES

MOD_TN = 1536
ATT_TQ = 512
ATT_TK = 512
FFN_TM = 512
FFN_CK = 1536


def _rms_norm(x, g):
    y = x * lax.rsqrt(jnp.mean(x * x, axis=-1, keepdims=True) + EPS)
    return y * g


def _gelu_tanh(x):
    cdf = 0.5 * (1.0 + jnp.tanh(math.sqrt(2.0 / math.pi) * (x + 0.044715 * (x * x * x))))
    return x * cdf


def _shift_rows(prev8, x, s):
    ext = jnp.concatenate([prev8, x], axis=0)
    return pltpu.roll(ext, s, 0)[SUBLANES:]


def _adaln_kernel(c_ref, w_ref, b_ref, o_ref):
    c = c_ref[...]
    cond = c * jax.nn.sigmoid(c)
    o_ref[...] = jnp.sum(cond * w_ref[...], axis=0, keepdims=True) + b_ref[...]


def _adaln_mod(c_col, w_ada, b_ada):
    d, n = w_ada.shape
    return pl.pallas_call(
        _adaln_kernel,
        grid=(n // MOD_TN,),
        in_specs=[pl.BlockSpec((d, 1), lambda j: (0, 0)),
                  pl.BlockSpec((d, MOD_TN), lambda j: (0, j)),
                  pl.BlockSpec((1, MOD_TN), lambda j: (0, j))],
        out_specs=pl.BlockSpec((1, MOD_TN), lambda j: (0, j)),
        out_shape=jax.ShapeDtypeStruct((1, n), F32),
        compiler_params=pltpu.CompilerParams(vmem_limit_bytes=VMEM_LIMIT),
        name="adaln_mod",
    )(c_col, w_ada, b_ada)


def _time_major_rows(ref, base):
    return [jnp.concatenate([ref[c, pl.ds(base + k, SUBLANES, stride=LRU_PITCH), :] for c in range(LANE_COLS)],
                            axis=1) for k in range(LRU_BLOCK_T)]


def _lru_conv_gates(xs, xprev, cw, cb, wg, bg):
    sub = lax.broadcasted_iota(jnp.int32, xs[0].shape, 0)
    back = {}
    for j in range(1, CONV_LRU):
        y = jnp.where(sub == SUBLANES - 1, xprev[CONV_LRU - 1 - j], xs[LRU_BLOCK_T - j])
        back[-j] = pltpu.roll(y, 1, 0)
    at = lambda k: xs[k] if k >= 0 else back[k]
    xcs = []
    for k in range(LRU_BLOCK_T):
        xc = cb
        for j in range(CONV_LRU):
            xc = xc + at(k - (CONV_LRU - 1 - j)) * cw[j:j + 1]
        xcs.append(xc)
    xc = jnp.concatenate(xcs, axis=0)
    gates = jnp.dot(xc.astype(BF16), wg, preferred_element_type=F32) + bg
    return xc, gates


def _lru_recur(xc, gates, gate, f_prev, softplus):
    sig = 0.5 * jnp.tanh(gates) + 0.5
    r = sig[:, :LRU_WIDTH]
    ig = sig[:, LRU_WIDTH:]
    log_a = -LRU_C * r * softplus
    a = jnp.exp(log_a)
    th = jnp.tanh(log_a)
    u = -2.0 * th / (1.0 - th)
    b = jnp.where(u > 0.0, u * lax.rsqrt(u), 0.0) * (ig * xc)

    hs, prods = [], []
    for k in range(LRU_BLOCK_T):
        ak = a[k * SUBLANES:(k + 1) * SUBLANES]
        bk = b[k * SUBLANES:(k + 1) * SUBLANES]
        hs.append(bk if k == 0 else ak * hs[-1] + bk)
        prods.append(ak if k == 0 else ak * prods[-1])

    sub = lax.broadcasted_iota(jnp.int32, f_prev.shape, 0)
    start = pltpu.roll(f_prev, 1, 0)
    fa = prods[-1]
    fb = jnp.where(sub == 0, hs[-1] + fa * start, hs[-1])
    sh = 1
    while sh < SUBLANES:
        valid = sub >= sh
        fa_s = jnp.where(valid, pltpu.roll(fa, sh, 0), 1.0)
        fb_s = jnp.where(valid, pltpu.roll(fb, sh, 0), 0.0)
        fb = fa * fb_s + fb
        fa = fa * fa_s
        sh *= 2
    f = fb
    before = jnp.where(sub == 0, start, pltpu.roll(f, 1, 0))
    outs = [(hs[k] + prods[k] * before) * gate[k * SUBLANES:(k + 1) * SUBLANES] for k in range(LRU_BLOCK_T)]
    return outs, f


def _in_proj_lru_kernel(x_ref, mod_ref, g_ref, w_ref, cw_ref, cb_ref, wg_ref, bg_ref, lam_ref,
                        q_ref, k_ref, v_ref, lru_ref, w_sc, xr_sc, yg_sc, o_sc, tail_sc, f_sc, *, tm):
    i = pl.program_id(0)

    @pl.when(i == 0)
    def _():
        w_sc[...] = w_ref[...].astype(BF16)
        xr_sc[...] = jnp.zeros(xr_sc.shape, F32)
        yg_sc[...] = jnp.zeros(yg_sc.shape, F32)

    @pl.when(i <= 1)
    def _():
        tail_sc[...] = jnp.zeros(tail_sc.shape, F32)
        f_sc[...] = jnp.zeros(f_sc.shape, F32)

    tc = SUBLANES * LRU_BLOCK_T
    assert tm == 2 * tc
    chunk_rows = SUBLANES * LRU_PITCH
    z = -lam_ref[...]
    softplus = jnp.maximum(z, 0.0) + jnp.log1p(jnp.exp(-jnp.abs(z)))
    lru_w = (cw_ref[...], cb_ref[...], wg_ref[...], bg_ref[...])

    x = x_ref[...]
    shift1 = mod_ref[:, 0:D_MODEL]
    scale1 = mod_ref[:, D_MODEL:2 * D_MODEL]
    h = (_rms_norm(x, g_ref[...]) * (1.0 + scale1) + shift1).astype(BF16)
    ones = jnp.ones((x.shape[0], V_DIM), BF16)

    def proj(lo, width):
        return jnp.dot(h, w_sc[:, lo:lo + width], preferred_element_type=F32)

    def emit(outs, base):
        for k, out in enumerate(outs):
            for c in range(LANE_COLS):
                o_sc[c, pl.ds(base + k, SUBLANES, stride=LRU_PITCH), :] = out[:, c * LANES:(c + 1) * LANES]

    half = QK_WIDTH // 2
    pq_a = proj(0, half)
    xs0 = _time_major_rows(xr_sc, 0)
    xprev = [tail_sc[j] for j in range(CONV_LRU - 1)]
    xc0, gates0 = _lru_conv_gates(xs0, xprev, *lru_w)
    pq = jnp.concatenate([pq_a, proj(half, half)], axis=1)
    for hd in range(N_HEADS):
        q_ref[hd] = (pq[:, hd * V_DIM:(hd + 1) * V_DIM] * (HEAD_DIM ** -0.5 * LOG2E)).astype(BF16)

    pk_a = proj(QK_WIDTH, half)
    gate0 = _gelu_tanh(jnp.concatenate(_time_major_rows(yg_sc, 0), axis=0))
    outs0, f0 = _lru_recur(xc0, gates0, gate0, f_sc[...], softplus)
    emit(outs0, 0)
    pk = jnp.concatenate([pk_a, proj(QK_WIDTH + half, half)], axis=1)
    for hd in range(N_HEADS):
        k_ref[hd, 0] = pk[:, hd * V_DIM:(hd + 1) * V_DIM].T.astype(BF16)

    pv_a = proj(2 * QK_WIDTH, half)
    xs1 = _time_major_rows(xr_sc, chunk_rows)
    xc1, gates1 = _lru_conv_gates(xs1, xs0[LRU_BLOCK_T - (CONV_LRU - 1):], *lru_w)
    pv = jnp.concatenate([pv_a, proj(2 * QK_WIDTH + half, half)], axis=1)
    for hd in range(N_HEADS):
        v_ref[hd] = jnp.concatenate([pv[:, hd * V_DIM:(hd + 1) * V_DIM].astype(BF16), ones], axis=1)

    base = 2 * QK_WIDTH + ATTN_WIDTH
    pxr_a = proj(base, half)
    gate1 = _gelu_tanh(jnp.concatenate(_time_major_rows(yg_sc, chunk_rows), axis=0))
    outs1, f1 = _lru_recur(xc1, gates1, gate1, f0, softplus)
    emit(outs1, chunk_rows)
    pxr = jnp.concatenate([pxr_a, proj(base + half, half)], axis=1)
    f_sc[...] = f1
    for j in range(CONV_LRU - 1):
        tail_sc[j] = xs1[LRU_BLOCK_T - (CONV_LRU - 1) + j]

    pyg = proj(base + LRU_WIDTH, LRU_WIDTH)
    for n in range(tm // LRU_BLOCK_T):
        rows = slice(n * LRU_BLOCK_T, (n + 1) * LRU_BLOCK_T)
        pad = slice(n * LRU_PITCH, n * LRU_PITCH + LRU_BLOCK_T)
        for c in range(LANE_COLS):
            cols = slice(c * LANES, (c + 1) * LANES)
            lru_ref[rows, cols] = o_sc[c, pad, :].astype(lru_ref.dtype)
            xr_sc[c, pad, :] = pxr[rows, cols]
            yg_sc[c, pad, :] = pyg[rows, cols]


def _in_proj_lru(x2d, mod, g1, w_in, conv_w, conv_b, w_gates_bf16, b_gates, lam, tm):
    s = x2d.shape[0]
    n = s // tm
    assert tm == 2 * SUBLANES * LRU_BLOCK_T
    pad_rows = (tm // LRU_BLOCK_T) * LRU_PITCH
    cur = lambda i: jnp.minimum(i, n - 1)
    const = lambda shape: pl.BlockSpec(shape, lambda i: (0, 0))
    return pl.pallas_call(
        functools.partial(_in_proj_lru_kernel, tm=tm),
        grid=(n + 1,),
        in_specs=[pl.BlockSpec((tm, D_MODEL), lambda i: (cur(i), 0)),
                  const((1, 6 * D_MODEL)), const((1, D_MODEL)),
                  pl.BlockSpec((D_MODEL, D_IN), lambda i: (0, 0), pipeline_mode=pl.Buffered(1)),
                  const((CONV_LRU, LRU_WIDTH)), const((1, LRU_WIDTH)),
                  const((LRU_WIDTH, 2 * LRU_WIDTH)), const((1, 2 * LRU_WIDTH)),
                  const((1, LRU_WIDTH))],
        out_specs=[pl.BlockSpec((N_HEADS, tm, V_DIM), lambda i: (0, cur(i), 0)),
                   pl.BlockSpec((N_HEADS, 1, V_DIM, tm), lambda i: (0, cur(i), 0, 0)),
                   pl.BlockSpec((N_HEADS, tm, 2 * V_DIM), lambda i: (0, cur(i), 0)),
                   pl.BlockSpec((tm, LRU_WIDTH), lambda i: (jnp.maximum(i - 1, 0), 0))],
        out_shape=[jax.ShapeDtypeStruct((N_HEADS, s, V_DIM), BF16),
                   jax.ShapeDtypeStruct((N_HEADS, n, V_DIM, tm), BF16),
                   jax.ShapeDtypeStruct((N_HEADS, s, 2 * V_DIM), BF16),
                   jax.ShapeDtypeStruct((s, LRU_WIDTH), BF16)],
        scratch_shapes=[pltpu.VMEM((D_MODEL, D_IN), BF16),
                        pltpu.VMEM((LANE_COLS, pad_rows, LANES), F32),
                        pltpu.VMEM((LANE_COLS, pad_rows, LANES), F32),
                        pltpu.VMEM((LANE_COLS, pad_rows, LANES), F32),
                        pltpu.VMEM((CONV_LRU - 1, SUBLANES, LRU_WIDTH), F32),
                        pltpu.VMEM((SUBLANES, LRU_WIDTH), F32)],
        compiler_params=pltpu.CompilerParams(dimension_semantics=("arbitrary",),
                                             vmem_limit_bytes=VMEM_LIMIT),
        name="in_proj_lru",
    )(x2d, mod, g1, w_in, conv_w, conv_b, w_gates_bf16, b_gates, lam)


def _bias_tiles_kernel(table_ref, o_ref, *, tq, tk, nd):
    hd = pl.program_id(0)
    blk = MAX_DISTANCE
    qpos = lax.broadcasted_iota(jnp.int32, (blk, blk), 0)
    kpos = lax.broadcasted_iota(jnp.int32, (blk, blk), 1)
    far = table_ref[hd * NUM_BUCKETS + NUM_BUCKETS - 1]

    def band_block(offset):
        rel = offset * blk + qpos - kpos
        n = jnp.maximum(rel, 0)
        nf = jnp.maximum(n, 1).astype(F32)
        y = (jnp.log(nf / MAX_EXACT) / math.log(MAX_DISTANCE / MAX_EXACT) * (NUM_BUCKETS - MAX_EXACT))
        val = jnp.zeros((blk, blk), F32)
        for b in range(NUM_BUCKETS):
            hit = (n == b) if b < MAX_EXACT else ((n >= MAX_EXACT) & (y >= b - MAX_EXACT))
            val = jnp.where(hit, (table_ref[hd * NUM_BUCKETS + b] - far) * LOG2E, val)
        return jnp.where(rel >= 0, val, NEG_INF)

    blocks = {0: band_block(0), 1: band_block(1)}
    zeros = jnp.zeros((blk, blk), F32)
    neg = jnp.full((blk, blk), NEG_INF, F32)
    for dd in range(nd):
        for a in range(tq // blk):
            for b in range(tk // blk):
                off = ((dd + 1) * tk - tq) // blk + a - b
                o_ref[0, dd, a * blk:(a + 1) * blk, b * blk:(b + 1) * blk] = (
                    neg if off < 0 else blocks.get(off, zeros))
    o_ref[0, nd] = jnp.zeros((tq, tk), F32)
    o_ref[0, nd + 1] = jnp.full((tq, tk), NEG_INF, F32)


def _bias_tiles(table_flat, tq, tk):
    assert tq % tk == 0 and tk % MAX_DISTANCE == 0
    nd = tq // tk + 1
    return pl.pallas_call(
        functools.partial(_bias_tiles_kernel, tq=tq, tk=tk, nd=nd),
        grid=(N_HEADS,),
        in_specs=[pl.BlockSpec(memory_space=pltpu.SMEM)],
        out_specs=pl.BlockSpec((1, nd + 2, tq, tk), lambda h: (h, 0, 0, 0)),
        out_shape=jax.ShapeDtypeStruct((N_HEADS, nd + 2, tq, tk), F32),
        name="bias_tiles",
    )(table_flat)


def _attn_kernel(q_ref, qn_ref, k_ref, v_ref, bias_ref, lamv_ref, gs_ref, o_ref, qs_sc, sa_sc, sb_sc, pa_sc, pb_sc,
                 m_sc, acc_sc, *, tq, tk, nd):
    i = pl.program_id(1)

    def stack_maps(q):
        lane = lax.broadcasted_iota(jnp.int32, q.shape, 1)
        zero = jnp.zeros_like(q)
        return jnp.concatenate([jnp.where(lane < HEAD_DIM, q, zero), jnp.where(lane >= HEAD_DIM, q, zero)], axis=0)

    qs_sc[0] = stack_maps(q_ref[0])
    qs_sc[1] = stack_maps(qn_ref[0])

    def limits(qi):
        q_start = qi * tq
        n_far = jnp.maximum(q_start - (MAX_DISTANCE - 1), 0) // tk
        last = (q_start + tq - 1) // tk
        return q_start, n_far, last

    _, _, last = limits(i)

    def scores(slot, qi, j, s_ref, part_ref, far_only=False):
        q_start, n_far, last_q = limits(qi)
        jc = jnp.minimum(j, last_q)
        s = jnp.dot(qs_sc[slot], k_ref[0, jc], preferred_element_type=F32)
        if not far_only:
            dd = (q_start + tq - (jc + 1) * tk) // tk
            idx = jnp.where(j > last_q, nd + 1, jnp.where(j < n_far, nd, dd))
            s = (s.reshape(2, tq, tk) + bias_ref[0, idx][None]).reshape(2 * tq, tk)
        s_ref[...] = s
        part_ref[...] = functools.reduce(
            jnp.maximum, [s[:, c * LANES:(c + 1) * LANES] for c in range(tk // LANES)])

    def accumulate(s_ref, part_ref, j):
        jc = jnp.minimum(j, last)
        vb = v_ref[0, pl.ds(pl.multiple_of(jc * tk, tk), tk), :]
        m_prev = m_sc[...]
        m_new = jnp.maximum(m_prev, jnp.max(part_ref[...], axis=1, keepdims=True))
        alpha = jnp.exp2(m_prev - m_new)
        p = jnp.exp2(s_ref[...] - jnp.concatenate([m_new] * (tk // LANES), axis=1))
        acc_sc[...] = (jnp.concatenate([alpha, alpha], axis=1) * acc_sc[...]
                       + jnp.dot(p.astype(BF16), vb, preferred_element_type=F32))
        m_sc[...] = m_new

    @pl.when(i == 0)
    def _():
        scores(0, i, 0, sa_sc, pa_sc)

    m_sc[...] = jnp.full(m_sc.shape, NEG_INF, F32)
    acc_sc[...] = jnp.zeros(acc_sc.shape, F32)

    def two_blocks(j):
        scores(0, i, j + 1, sb_sc, pb_sc)
        accumulate(sa_sc, pa_sc, j)
        to_next = (j + 2 > last).astype(jnp.int32)
        scores(to_next, i + to_next, (1 - to_next) * (j + 2), sa_sc, pa_sc)
        accumulate(sb_sc, pb_sc, j + 1)

    def far_oct_body(t, carry):
        for u in range(4):
            j = 8 * t + 2 * u
            scores(0, i, j + 1, sb_sc, pb_sc, far_only=True)
            accumulate(sa_sc, pa_sc, j)
            scores(0, i, j + 2, sa_sc, pa_sc, far_only=True)
            accumulate(sb_sc, pb_sc, j + 1)
        return carry

    def oct_body(t, carry):
        for u in range(4):
            two_blocks(j0 + 8 * t + 2 * u)
        return carry

    def quad_body(t, carry):
        two_blocks(j0 + 8 * n_octs + 4 * t)
        two_blocks(j0 + 8 * n_octs + 4 * t + 2)
        return carry

    def pair_body(t, carry):
        two_blocks(j0 + 4 * n_quads + 2 * t)
        return carry

    _, n_far, _ = limits(i)
    n_far_octs = jnp.maximum(n_far - 1, 0) // 8
    j0 = 8 * n_far_octs
    n_octs = (last + 1 - j0) // 8
    n_quads = (last + 1 - j0) // 4
    lax.fori_loop(0, n_far_octs, far_oct_body, 0)
    lax.fori_loop(0, n_octs, oct_body, 0)
    lax.fori_loop(0, n_quads - 2 * n_octs, quad_body, 0)
    lax.fori_loop(0, (last + 1 - j0) // 2 - 2 * n_quads, pair_body, 0)

    def single_body(t, carry):
        scores(1, i + 1, 0, sb_sc, pb_sc)
        accumulate(sa_sc, pa_sc, last)
        sa_sc[...] = sb_sc[...]
        pa_sc[...] = pb_sc[...]
        return carry

    lax.fori_loop(0, (last + 1) % 2, single_body, 0)

    lv = lamv_ref[...]
    d1 = jnp.sum(lv[0:1] * lv[1:2], axis=1, keepdims=True)
    d2 = jnp.sum(lv[2:3] * lv[3:4], axis=1, keepdims=True)
    lam = jnp.exp(d1) - jnp.exp(d2) + LAMBDA_INIT

    acc = acc_sc[...]
    out = acc[:, :V_DIM] / acc[:, V_DIM:]
    diff = out[:tq] - lam * out[tq:]
    o_ref[...] = (_rms_norm(diff, gs_ref[...]) * (1.0 - LAMBDA_INIT)).astype(o_ref.dtype)


def _diff_attn(q, kt, v_aug, bias, lamv, g_subln, tq, tk):
    s = q.shape[1]
    nq = s // tq
    nd = bias.shape[1] - 2
    return pl.pallas_call(
        functools.partial(_attn_kernel, tq=tq, tk=tk, nd=nd),
        grid=(N_HEADS, nq),
        in_specs=[pl.BlockSpec((1, tq, V_DIM), lambda h, i: (h, i, 0)),
                  pl.BlockSpec((1, tq, V_DIM), lambda h, i: (h, jnp.minimum(i + 1, nq - 1), 0)),
                  pl.BlockSpec((1, s // tk, V_DIM, tk), lambda h, i: (h, 0, 0, 0)),
                  pl.BlockSpec((1, s, 2 * V_DIM), lambda h, i: (h, 0, 0)),
                  pl.BlockSpec((1, nd + 2, tq, tk), lambda h, i: (h, 0, 0, 0), pipeline_mode=pl.Buffered(1)),
                  pl.BlockSpec((4, HEAD_DIM), lambda h, i: (0, 0)),
                  pl.BlockSpec((1, V_DIM), lambda h, i: (0, 0))],
        out_specs=pl.BlockSpec((tq, V_DIM), lambda h, i: (i, h)),
        out_shape=jax.ShapeDtypeStruct((s, ATTN_WIDTH), BF16),
        scratch_shapes=[pltpu.VMEM((2, 2 * tq, V_DIM), BF16),
                        pltpu.VMEM((2 * tq, tk), F32),
                        pltpu.VMEM((2 * tq, tk), F32),
                        pltpu.VMEM((2 * tq, LANES), F32),
                        pltpu.VMEM((2 * tq, LANES), F32),
                        pltpu.VMEM((2 * tq, LANES), F32),
                        pltpu.VMEM((2 * tq, 2 * V_DIM), F32)],
        compiler_params=pltpu.CompilerParams(dimension_semantics=("arbitrary", "arbitrary"),
                                             vmem_limit_bytes=VMEM_LIMIT),
        name="diff_attn",
    )(q, q, kt, v_aug, bias, lamv, g_subln)


def _ffn_kernel(x_ref, lru_ref, attn_ref, mod_ref, wo_ref, g2_ref, wup_ref, cw_ref, cb_ref, wdn_ref, gf_ref,
                o_ref, tail_sc, *, tm):
    @pl.when(pl.program_id(0) == 0)
    def _():
        tail_sc[...] = jnp.zeros(tail_sc.shape, F32)

    gate1 = mod_ref[:, 2 * D_MODEL:3 * D_MODEL]
    shift2 = mod_ref[:, 3 * D_MODEL:4 * D_MODEL]
    scale2 = mod_ref[:, 4 * D_MODEL:5 * D_MODEL]
    gate2 = mod_ref[:, 5 * D_MODEL:6 * D_MODEL]

    mix = (jnp.dot(lru_ref[...], wo_ref[0:LRU_WIDTH, :], preferred_element_type=F32)
           + jnp.dot(attn_ref[...], wo_ref[LRU_WIDTH:, :], preferred_element_type=F32))
    x1 = x_ref[...] + gate1 * mix
    h2 = (_rms_norm(x1, g2_ref[...]) * (1.0 + scale2) + shift2).astype(BF16)

    ff = jnp.zeros((tm, D_MODEL), F32)
    for c in range(D_FF // FFN_CK):
        lo = c * FFN_CK
        a = jnp.dot(h2, wup_ref[:, lo:lo + FFN_CK].astype(BF16), preferred_element_type=F32)
        g = jnp.dot(h2, wup_ref[:, D_FF + lo:D_FF + lo + FFN_CK].astype(BF16), preferred_element_type=F32)
        prev8 = tail_sc[:, lo:lo + FFN_CK]
        ac = cb_ref[:, lo:lo + FFN_CK]
        for j in range(CONV_FFN):
            sh = CONV_FFN - 1 - j
            a_s = a if sh == 0 else _shift_rows(prev8, a, sh)
            ac = ac + a_s * cw_ref[j:j + 1, lo:lo + FFN_CK]
        tail_sc[:, lo:lo + FFN_CK] = a[tm - SUBLANES:]
        u = (_gelu_tanh(ac) * g).astype(BF16)
        ff = ff + jnp.dot(u, wdn_ref[lo:lo + FFN_CK, :].astype(BF16), preferred_element_type=F32)

    x2 = x1 + gate2 * ff
    o_ref[...] = _rms_norm(x2, gf_ref[...])


def _out_ffn(x2d, lru, attn, mod, w_out, g2, w_up, conv_w, conv_b, w_down, g_final):
    s = x2d.shape[0]
    tm = min(FFN_TM, s)
    row = lambda w: pl.BlockSpec((tm, w), lambda i: (i, 0))
    const = lambda shape: pl.BlockSpec(shape, lambda i: (0, 0), pipeline_mode=pl.Buffered(1))
    return pl.pallas_call(
        functools.partial(_ffn_kernel, tm=tm),
        grid=(s // tm,),
        in_specs=[row(D_MODEL), row(LRU_WIDTH), row(ATTN_WIDTH),
                  const((1, 6 * D_MODEL)),
                  const((D_MODEL, D_MODEL)), const((1, D_MODEL)),
                  const((D_MODEL, 2 * D_FF)),
                  const((CONV_FFN, D_FF)), const((1, D_FF)),
                  const((D_FF, D_MODEL)), const((1, D_MODEL))],
        out_specs=row(D_MODEL),
        out_shape=jax.ShapeDtypeStruct((s, D_MODEL), F32),
        scratch_shapes=[pltpu.VMEM((SUBLANES, D_FF), F32)],
        compiler_params=pltpu.CompilerParams(dimension_semantics=("arbitrary",),
                                             vmem_limit_bytes=FFN_VMEM_LIMIT),
        name="out_ffn",
    )(x2d, lru, attn, mod, w_out, g2, w_up, conv_w, conv_b, w_down, g_final)


def _block_diag(w):
    nb, bs, _ = w.shape
    eye = jnp.eye(nb, dtype=w.dtype)
    return (w[:, :, None, :] * eye[:, None, :, None]).reshape(nb * bs, nb * bs)


def kernel(x, c, w_ada, b_ada, g_norm1, w_in, conv_lru_w, conv_lru_b, lru_wa, lru_ba, lru_wx, lru_bx, lru_lambda,
           lam_q1, lam_k1, lam_q2, lam_k2, g_subln, w_out, g_norm2, w_up, conv_ffn_w, conv_ffn_b, w_down, rel_bias,
           g_final):
    b, s, d = x.shape
    assert b == 1 and d == D_MODEL and w_ada.shape[0] == 1
    x2d = x.reshape(s, d)
    row = lambda a: a.reshape(1, -1)

    mod = _adaln_mod(c.reshape(d, 1), w_ada[0], row(b_ada[0]))

    tq, tk = min(ATT_TQ, s), min(ATT_TK, s)
    w_gates = (0.5 * jnp.concatenate([_block_diag(lru_wa[0]), _block_diag(lru_wx[0])], axis=1)).astype(BF16)
    b_gates = 0.5 * jnp.concatenate([lru_ba[0], lru_bx[0]]).reshape(1, -1)
    q, kt, v, lru = _in_proj_lru(x2d, mod, row(g_norm1[0]), w_in[0], conv_lru_w[0], row(conv_lru_b[0]),
                                 w_gates, b_gates, row(lru_lambda[0]), tk)

    bias = _bias_tiles(rel_bias.T.reshape(-1), tq, tk)
    lamv = jnp.stack([lam_q1[0], lam_k1[0], lam_q2[0], lam_k2[0]])
    attn = _diff_attn(q, kt, v, bias, lamv, row(g_subln[0]), tq, tk)

    out = _out_ffn(x2d, lru, attn, mod, w_out[0].astype(BF16), row(g_norm2[0]), w_up[0],
                   conv_ffn_w[0], row(conv_ffn_b[0]), w_down[0], row(g_final))
    return out.reshape(b, s, d)
```

```python
import functools
import math

import jax
import jax.numpy as jnp
from jax import lax
from jax.experimental import pallas as pl
from jax.experimental.pallas import tpu as pltpu

F32 = jnp.float32
BF16 = jnp.bfloat16

D_MODEL = 1024
LRU_WIDTH = 512
LRU_BLOCKS = 8
LRU_BLOCK = LRU_WIDTH // LRU_BLOCKS
CONV_LRU = 4
LRU_C = 8.0
N_HEADS = 4
HEAD_DIM = 64
V_DIM = 2 * HEAD_DIM
QK_WIDTH = N_HEADS * 2 * HEAD_DIM
ATTN_WIDTH = N_HEADS * V_DIM
D_IN = 2 * QK_WIDTH + ATTN_WIDTH + 2 * LRU_WIDTH
D_FF = 3 * D_MODEL
CONV_FFN = 3
NUM_BUCKETS = 32
MAX_EXACT = NUM_BUCKETS // 2
MAX_DISTANCE = 128
EPS = 1e-6
NEG_INF = -1e30
LAMBDA_INIT = 0.8 - 0.6 * math.exp(-0.3 * 0)
LOG2E = math.log2(math.e)

LANES = 128
SUBLANES = 8
VMEM_BYTES = 64 * 1024 * 1024
VMEM_LIMIT = VMEM_BYTES - 8 * 1024 * 1024
FFN_VMEM_LIMIT = VMEM_BYTES - 1024 * 1024

LRU_BLOCK_T = 32
LRU_PITCH = 40
LANE_COLS = LRU_WIDTH // LANES

MOD_TN = 1536
ATT_TQ = 512
ATT_TK = 512
FFN_TM = 512
FFN_CK = 1536


def _rms_norm(x, g):
    y = x * lax.rsqrt(jnp.mean(x * x, axis=-1, keepdims=True) + EPS)
    return y * g


def _gelu_tanh(x):
    cdf = 0.5 * (1.0 + jnp.tanh(math.sqrt(2.0 / math.pi) * (x + 0.044715 * (x * x * x))))
    return x * cdf


def _shift_rows(prev8, x, s):
    ext = jnp.concatenate([prev8, x], axis=0)
    return pltpu.roll(ext, s, 0)[SUBLANES:]


def _adaln_kernel(c_ref, w_ref, b_ref, o_ref):
    c = c_ref[...]
    cond = c * jax.nn.sigmoid(c)
    o_ref[...] = jnp.sum(cond * w_ref[...], axis=0, keepdims=True) + b_ref[...]


def _adaln_mod(c_col, w_ada, b_ada):
    d, n = w_ada.shape
    return pl.pallas_call(
        _adaln_kernel,
        grid=(n // MOD_TN,),
        in_specs=[pl.BlockSpec((d, 1), lambda j: (0, 0)),
                  pl.BlockSpec((d, MOD_TN), lambda j: (0, j)),
                  pl.BlockSpec((1, MOD_TN), lambda j: (0, j))],
        out_specs=pl.BlockSpec((1, MOD_TN), lambda j: (0, j)),
        out_shape=jax.ShapeDtypeStruct((1, n), F32),
        compiler_params=pltpu.CompilerParams(vmem_limit_bytes=VMEM_LIMIT),
        name="adaln_mod",
    )(c_col, w_ada, b_ada)


def _time_major_rows(ref, base):
    return [jnp.concatenate([ref[c, pl.ds(base + k, SUBLANES, stride=LRU_PITCH), :] for c in range(LANE_COLS)],
                            axis=1) for k in range(LRU_BLOCK_T)]


def _lru_conv_gates(xs, xprev, cw, cb, wg, bg):
    sub = lax.broadcasted_iota(jnp.int32, xs[0].shape, 0)
    back = {}
    for j in range(1, CONV_LRU):
        y = jnp.where(sub == SUBLANES - 1, xprev[CONV_LRU - 1 - j], xs[LRU_BLOCK_T - j])
        back[-j] = pltpu.roll(y, 1, 0)
    at = lambda k: xs[k] if k >= 0 else back[k]
    xcs = []
    for k in range(LRU_BLOCK_T):
        xc = cb
        for j in range(CONV_LRU):
            xc = xc + at(k - (CONV_LRU - 1 - j)) * cw[j:j + 1]
        xcs.append(xc)
    xc = jnp.concatenate(xcs, axis=0)
    gates = jnp.dot(xc.astype(BF16), wg, preferred_element_type=F32) + bg
    return xc, gates


def _lru_recur(xc, gates, gate, f_prev, softplus):
    t = jnp.tanh(gates)
    ig = 0.5 * t[:, LRU_WIDTH:] + 0.5
    half_rate = (-0.5 * LRU_C) * softplus
    log_a = half_rate * t[:, :LRU_WIDTH] + half_rate
    a = jnp.exp(log_a)
    th = jnp.tanh(log_a)
    u = -2.0 * th / (1.0 - th)
    b = jnp.where(u > 0.0, u * lax.rsqrt(u), 0.0) * (ig * xc)

    hs, prods = [], []
    for k in range(LRU_BLOCK_T):
        ak = a[k * SUBLANES:(k + 1) * SUBLANES]
        bk = b[k * SUBLANES:(k + 1) * SUBLANES]
        hs.append(bk if k == 0 else ak * hs[-1] + bk)
        prods.append(ak if k == 0 else ak * prods[-1])

    sub = lax.broadcasted_iota(jnp.int32, f_prev.shape, 0)
    start = pltpu.roll(f_prev, 1, 0)
    fa = prods[-1]
    fb = jnp.where(sub == 0, hs[-1] + fa * start, hs[-1])
    sh = 1
    while sh < SUBLANES:
        valid = sub >= sh
        fa_s = jnp.where(valid, pltpu.roll(fa, sh, 0), 1.0)
        fb_s = jnp.where(valid, pltpu.roll(fb, sh, 0), 0.0)
        fb = fa * fb_s + fb
        fa = fa * fa_s
        sh *= 2
    f = fb
    before = jnp.where(sub == 0, start, pltpu.roll(f, 1, 0))
    outs = [(hs[k] + prods[k] * before) * gate[k * SUBLANES:(k + 1) * SUBLANES] for k in range(LRU_BLOCK_T)]
    return outs, f


def _in_proj_lru_kernel(x_ref, mod_ref, g_ref, w_ref, cw_ref, cb_ref, wg_ref, bg_ref, lam_ref,
                        q_ref, k_ref, v_ref, lru_ref, w_sc, xr_sc, yg_sc, o_sc, tail_sc, f_sc, *, tm):
    i = pl.program_id(0)

    @pl.when(i == 0)
    def _():
        w_sc[...] = w_ref[...].astype(BF16)
        xr_sc[...] = jnp.zeros(xr_sc.shape, F32)
        yg_sc[...] = jnp.zeros(yg_sc.shape, F32)

    @pl.when(i <= 1)
    def _():
        tail_sc[...] = jnp.zeros(tail_sc.shape, F32)
        f_sc[...] = jnp.zeros(f_sc.shape, F32)

    tc = SUBLANES * LRU_BLOCK_T
    assert tm == 2 * tc
    chunk_rows = SUBLANES * LRU_PITCH
    z = -lam_ref[...]
    softplus = jnp.maximum(z, 0.0) + jnp.log1p(jnp.exp(-jnp.abs(z)))
    lru_w = (cw_ref[...], cb_ref[...], wg_ref[...], bg_ref[...])

    x = x_ref[...]
    shift1 = mod_ref[:, 0:D_MODEL]
    scale1 = mod_ref[:, D_MODEL:2 * D_MODEL]
    h = (_rms_norm(x, g_ref[...] * (1.0 + scale1)) + shift1).astype(BF16)
    ones = jnp.ones((x.shape[0], V_DIM), BF16)

    def proj(lo, width):
        return jnp.dot(h, w_sc[:, lo:lo + width], preferred_element_type=F32)

    def emit(outs, base):
        for k, out in enumerate(outs):
            for c in range(LANE_COLS):
                o_sc[c, pl.ds(base + k, SUBLANES, stride=LRU_PITCH), :] = out[:, c * LANES:(c + 1) * LANES]

    half = QK_WIDTH // 2
    pq_a = proj(0, half)
    xs0 = _time_major_rows(xr_sc, 0)
    xprev = [tail_sc[j] for j in range(CONV_LRU - 1)]
    xc0, gates0 = _lru_conv_gates(xs0, xprev, *lru_w)
    pq = jnp.concatenate([pq_a, proj(half, half)], axis=1)
    for hd in range(N_HEADS):
        q_ref[hd] = (pq[:, hd * V_DIM:(hd + 1) * V_DIM] * (HEAD_DIM ** -0.5 * LOG2E)).astype(BF16)

    pk_a = proj(QK_WIDTH, half)
    gate0 = _gelu_tanh(jnp.concatenate(_time_major_rows(yg_sc, 0), axis=0))
    outs0, f0 = _lru_recur(xc0, gates0, gate0, f_sc[...], softplus)
    emit(outs0, 0)
    pk = jnp.concatenate([pk_a, proj(QK_WIDTH + half, half)], axis=1)
    for hd in range(N_HEADS):
        k_ref[hd, 0] = pk[:, hd * V_DIM:(hd + 1) * V_DIM].T.astype(BF16)

    pv_a = proj(2 * QK_WIDTH, half)
    xs1 = _time_major_rows(xr_sc, chunk_rows)
    xc1, gates1 = _lru_conv_gates(xs1, xs0[LRU_BLOCK_T - (CONV_LRU - 1):], *lru_w)
    pv = jnp.concatenate([pv_a, proj(2 * QK_WIDTH + half, half)], axis=1)
    for hd in range(N_HEADS):
        v_ref[hd] = jnp.concatenate([pv[:, hd * V_DIM:(hd + 1) * V_DIM].astype(BF16), ones], axis=1)

    base = 2 * QK_WIDTH + ATTN_WIDTH
    pxr_a = proj(base, half)
    gate1 = _gelu_tanh(jnp.concatenate(_time_major_rows(yg_sc, chunk_rows), axis=0))
    outs1, f1 = _lru_recur(xc1, gates1, gate1, f0, softplus)
    emit(outs1, chunk_rows)
    pxr = jnp.concatenate([pxr_a, proj(base + half, half)], axis=1)
    f_sc[...] = f1
    for j in range(CONV_LRU - 1):
        tail_sc[j] = xs1[LRU_BLOCK_T - (CONV_LRU - 1) + j]

    pyg = proj(base + LRU_WIDTH, LRU_WIDTH)
    for n in range(tm // LRU_BLOCK_T):
        rows = slice(n * LRU_BLOCK_T, (n + 1) * LRU_BLOCK_T)
        pad = slice(n * LRU_PITCH, n * LRU_PITCH + LRU_BLOCK_T)
        for c in range(LANE_COLS):
            cols = slice(c * LANES, (c + 1) * LANES)
            lru_ref[rows, cols] = o_sc[c, pad, :].astype(lru_ref.dtype)
            xr_sc[c, pad, :] = pxr[rows, cols]
            yg_sc[c, pad, :] = pyg[rows, cols]


def _in_proj_lru(x2d, mod, g1, w_in, conv_w, conv_b, w_gates_bf16, b_gates, lam, tm):
    s = x2d.shape[0]
    n = s // tm
    assert tm == 2 * SUBLANES * LRU_BLOCK_T
    pad_rows = (tm // LRU_BLOCK_T) * LRU_PITCH
    cur = lambda i: jnp.minimum(i, n - 1)
    const = lambda shape: pl.BlockSpec(shape, lambda i: (0, 0))
    return pl.pallas_call(
        functools.partial(_in_proj_lru_kernel, tm=tm),
        grid=(n + 1,),
        in_specs=[pl.BlockSpec((tm, D_MODEL), lambda i: (cur(i), 0)),
                  const((1, 6 * D_MODEL)), const((1, D_MODEL)),
                  pl.BlockSpec((D_MODEL, D_IN), lambda i: (0, 0), pipeline_mode=pl.Buffered(1)),
                  const((CONV_LRU, LRU_WIDTH)), const((1, LRU_WIDTH)),
                  const((LRU_WIDTH, 2 * LRU_WIDTH)), const((1, 2 * LRU_WIDTH)),
                  const((1, LRU_WIDTH))],
        out_specs=[pl.BlockSpec((N_HEADS, tm, V_DIM), lambda i: (0, cur(i), 0)),
                   pl.BlockSpec((N_HEADS, 1, V_DIM, tm), lambda i: (0, cur(i), 0, 0)),
                   pl.BlockSpec((N_HEADS, tm, 2 * V_DIM), lambda i: (0, cur(i), 0)),
                   pl.BlockSpec((tm, LRU_WIDTH), lambda i: (jnp.maximum(i - 1, 0), 0))],
        out_shape=[jax.ShapeDtypeStruct((N_HEADS, s, V_DIM), BF16),
                   jax.ShapeDtypeStruct((N_HEADS, n, V_DIM, tm), BF16),
                   jax.ShapeDtypeStruct((N_HEADS, s, 2 * V_DIM), BF16),
                   jax.ShapeDtypeStruct((s, LRU_WIDTH), BF16)],
        scratch_shapes=[pltpu.VMEM((D_MODEL, D_IN), BF16),
                        pltpu.VMEM((LANE_COLS, pad_rows, LANES), F32),
                        pltpu.VMEM((LANE_COLS, pad_rows, LANES), F32),
                        pltpu.VMEM((LANE_COLS, pad_rows, LANES), F32),
                        pltpu.VMEM((CONV_LRU - 1, SUBLANES, LRU_WIDTH), F32),
                        pltpu.VMEM((SUBLANES, LRU_WIDTH), F32)],
        compiler_params=pltpu.CompilerParams(dimension_semantics=("arbitrary",),
                                             vmem_limit_bytes=VMEM_LIMIT),
        name="in_proj_lru",
    )(x2d, mod, g1, w_in, conv_w, conv_b, w_gates_bf16, b_gates, lam)


def _bias_tiles_kernel(table_ref, o_ref, *, tq, tk, nd):
    hd = pl.program_id(0)
    blk = MAX_DISTANCE
    qpos = lax.broadcasted_iota(jnp.int32, (blk, blk), 0)
    kpos = lax.broadcasted_iota(jnp.int32, (blk, blk), 1)
    far = table_ref[hd * NUM_BUCKETS + NUM_BUCKETS - 1]

    def band_block(offset):
        rel = offset * blk + qpos - kpos
        n = jnp.maximum(rel, 0)
        nf = jnp.maximum(n, 1).astype(F32)
        y = (jnp.log(nf / MAX_EXACT) / math.log(MAX_DISTANCE / MAX_EXACT) * (NUM_BUCKETS - MAX_EXACT))
        val = jnp.zeros((blk, blk), F32)
        for b in range(NUM_BUCKETS):
            hit = (n == b) if b < MAX_EXACT else ((n >= MAX_EXACT) & (y >= b - MAX_EXACT))
            val = jnp.where(hit, (table_ref[hd * NUM_BUCKETS + b] - far) * LOG2E, val)
        return jnp.where(rel >= 0, val, NEG_INF)

    blocks = {0: band_block(0), 1: band_block(1)}
    zeros = jnp.zeros((blk, blk), F32)
    neg = jnp.full((blk, blk), NEG_INF, F32)
    for dd in range(nd):
        for a in range(tq // blk):
            for b in range(tk // blk):
                off = ((dd + 1) * tk - tq) // blk + a - b
                o_ref[0, dd, a * blk:(a + 1) * blk, b * blk:(b + 1) * blk] = (
                    neg if off < 0 else blocks.get(off, zeros))
    o_ref[0, nd] = jnp.zeros((tq, tk), F32)
    o_ref[0, nd + 1] = jnp.full((tq, tk), NEG_INF, F32)


def _bias_tiles(table_flat, tq, tk):
    assert tq % tk == 0 and tk % MAX_DISTANCE == 0
    nd = tq // tk + 1
    return pl.pallas_call(
        functools.partial(_bias_tiles_kernel, tq=tq, tk=tk, nd=nd),
        grid=(N_HEADS,),
        in_specs=[pl.BlockSpec(memory_space=pltpu.SMEM)],
        out_specs=pl.BlockSpec((1, nd + 2, tq, tk), lambda h: (h, 0, 0, 0)),
        out_shape=jax.ShapeDtypeStruct((N_HEADS, nd + 2, tq, tk), F32),
        name="bias_tiles",
    )(table_flat)


def _attn_kernel(q_ref, qn_ref, k_ref, v_ref, bias_ref, lamv_ref, gs_ref, o_ref, qs_sc, sa_sc, sb_sc, pa_sc, pb_sc,
                 m_sc, acc_sc, *, tq, tk, nd):
    i = pl.program_id(1)

    def stack_maps(q):
        lane = lax.broadcasted_iota(jnp.int32, q.shape, 1)
        zero = jnp.zeros_like(q)
        return jnp.concatenate([jnp.where(lane < HEAD_DIM, q, zero), jnp.where(lane >= HEAD_DIM, q, zero)], axis=0)

    qs_sc[0] = stack_maps(q_ref[0])
    qs_sc[1] = stack_maps(qn_ref[0])

    def limits(qi):
        q_start = qi * tq
        n_far = jnp.maximum(q_start - (MAX_DISTANCE - 1), 0) // tk
        last = (q_start + tq - 1) // tk
        return q_start, n_far, last

    _, _, last = limits(i)

    def scores(slot, qi, j, s_ref, part_ref, far_only=False):
        q_start, n_far, last_q = limits(qi)
        jc = jnp.minimum(j, last_q)
        s = jnp.dot(qs_sc[slot], k_ref[0, jc], preferred_element_type=F32)
        if not far_only:
            dd = (q_start + tq - (jc + 1) * tk) // tk
            idx = jnp.where(j > last_q, nd + 1, jnp.where(j < n_far, nd, dd))
            s = (s.reshape(2, tq, tk) + bias_ref[0, idx][None]).reshape(2 * tq, tk)
        s_ref[...] = s
        part_ref[...] = functools.reduce(
            jnp.maximum, [s[:, c * LANES:(c + 1) * LANES] for c in range(tk // LANES)])

    def accumulate(s_ref, part_ref, j):
        jc = jnp.minimum(j, last)
        vb = v_ref[0, pl.ds(pl.multiple_of(jc * tk, tk), tk), :]
        m_prev = m_sc[...]
        m_new = jnp.maximum(m_prev, jnp.max(part_ref[...], axis=1, keepdims=True))
        alpha = jnp.exp2(m_prev - m_new)
        p = jnp.exp2(s_ref[...] - jnp.concatenate([m_new] * (tk // LANES), axis=1))
        acc_sc[...] = (jnp.concatenate([alpha, alpha], axis=1) * acc_sc[...]
                       + jnp.dot(p.astype(BF16), vb, preferred_element_type=F32))
        m_sc[...] = m_new

    @pl.when(i == 0)
    def _():
        scores(0, i, 0, sa_sc, pa_sc)

    m_sc[...] = jnp.full(m_sc.shape, NEG_INF, F32)
    acc_sc[...] = jnp.zeros(acc_sc.shape, F32)

    def two_blocks(j):
        scores(0, i, j + 1, sb_sc, pb_sc)
        accumulate(sa_sc, pa_sc, j)
        to_next = (j + 2 > last).astype(jnp.int32)
        scores(to_next, i + to_next, (1 - to_next) * (j + 2), sa_sc, pa_sc)
        accumulate(sb_sc, pb_sc, j + 1)

    def far_oct_body(t, carry):
        for u in range(4):
            j = 8 * t + 2 * u
            scores(0, i, j + 1, sb_sc, pb_sc, far_only=True)
            accumulate(sa_sc, pa_sc, j)
            scores(0, i, j + 2, sa_sc, pa_sc, far_only=True)
            accumulate(sb_sc, pb_sc, j + 1)
        return carry

    def oct_body(t, carry):
        for u in range(4):
            two_blocks(j0 + 8 * t + 2 * u)
        return carry

    def quad_body(t, carry):
        two_blocks(j0 + 8 * n_octs + 4 * t)
        two_blocks(j0 + 8 * n_octs + 4 * t + 2)
        return carry

    def pair_body(t, carry):
        two_blocks(j0 + 4 * n_quads + 2 * t)
        return carry

    _, n_far, _ = limits(i)
    n_far_octs = jnp.maximum(n_far - 1, 0) // 8
    j0 = 8 * n_far_octs
    n_octs = (last + 1 - j0) // 8
    n_quads = (last + 1 - j0) // 4
    lax.fori_loop(0, n_far_octs, far_oct_body, 0)
    lax.fori_loop(0, n_octs, oct_body, 0)
    lax.fori_loop(0, n_quads - 2 * n_octs, quad_body, 0)
    lax.fori_loop(0, (last + 1 - j0) // 2 - 2 * n_quads, pair_body, 0)

    def single_body(t, carry):
        scores(1, i + 1, 0, sb_sc, pb_sc)
        accumulate(sa_sc, pa_sc, last)
        sa_sc[...] = sb_sc[...]
        pa_sc[...] = pb_sc[...]
        return carry

    lax.fori_loop(0, (last + 1) % 2, single_body, 0)

    lv = lamv_ref[...]
    d1 = jnp.sum(lv[0:1] * lv[1:2], axis=1, keepdims=True)
    d2 = jnp.sum(lv[2:3] * lv[3:4], axis=1, keepdims=True)
    lam = jnp.exp(d1) - jnp.exp(d2) + LAMBDA_INIT

    acc = acc_sc[...]
    out = acc[:, :V_DIM] / acc[:, V_DIM:]
    diff = out[:tq] - lam * out[tq:]
    o_ref[...] = (_rms_norm(diff, gs_ref[...]) * (1.0 - LAMBDA_INIT)).astype(o_ref.dtype)


def _diff_attn(q, kt, v_aug, bias, lamv, g_subln, tq, tk):
    s = q.shape[1]
    nq = s // tq
    nd = bias.shape[1] - 2
    return pl.pallas_call(
        functools.partial(_attn_kernel, tq=tq, tk=tk, nd=nd),
        grid=(N_HEADS, nq),
        in_specs=[pl.BlockSpec((1, tq, V_DIM), lambda h, i: (h, i, 0)),
                  pl.BlockSpec((1, tq, V_DIM), lambda h, i: (h, jnp.minimum(i + 1, nq - 1), 0)),
                  pl.BlockSpec((1, s // tk, V_DIM, tk), lambda h, i: (h, 0, 0, 0)),
                  pl.BlockSpec((1, s, 2 * V_DIM), lambda h, i: (h, 0, 0)),
                  pl.BlockSpec((1, nd + 2, tq, tk), lambda h, i: (h, 0, 0, 0), pipeline_mode=pl.Buffered(1)),
                  pl.BlockSpec((4, HEAD_DIM), lambda h, i: (0, 0)),
                  pl.BlockSpec((1, V_DIM), lambda h, i: (0, 0))],
        out_specs=pl.BlockSpec((tq, V_DIM), lambda h, i: (i, h)),
        out_shape=jax.ShapeDtypeStruct((s, ATTN_WIDTH), BF16),
        scratch_shapes=[pltpu.VMEM((2, 2 * tq, V_DIM), BF16),
                        pltpu.VMEM((2 * tq, tk), F32),
                        pltpu.VMEM((2 * tq, tk), F32),
                        pltpu.VMEM((2 * tq, LANES), F32),
                        pltpu.VMEM((2 * tq, LANES), F32),
                        pltpu.VMEM((2 * tq, LANES), F32),
                        pltpu.VMEM((2 * tq, 2 * V_DIM), F32)],
        compiler_params=pltpu.CompilerParams(dimension_semantics=("arbitrary", "arbitrary"),
                                             vmem_limit_bytes=VMEM_LIMIT),
        name="diff_attn",
    )(q, q, kt, v_aug, bias, lamv, g_subln)


def _ffn_kernel(x_ref, lru_ref, attn_ref, mod_ref, wo_ref, g2_ref, wup_ref, cw_ref, cb_ref, wdn_ref, gf_ref,
                o_ref, tail_sc, *, tm):
    @pl.when(pl.program_id(0) == 0)
    def _():
        tail_sc[...] = jnp.zeros(tail_sc.shape, F32)

    gate1 = mod_ref[:, 2 * D_MODEL:3 * D_MODEL]
    shift2 = mod_ref[:, 3 * D_MODEL:4 * D_MODEL]
    scale2 = mod_ref[:, 4 * D_MODEL:5 * D_MODEL]
    gate2 = mod_ref[:, 5 * D_MODEL:6 * D_MODEL]

    mix = (jnp.dot(lru_ref[...], wo_ref[0:LRU_WIDTH, :], preferred_element_type=F32)
           + jnp.dot(attn_ref[...], wo_ref[LRU_WIDTH:, :], preferred_element_type=F32))
    x1 = x_ref[...] + gate1 * mix
    h2 = (_rms_norm(x1, g2_ref[...]) * (1.0 + scale2) + shift2).astype(BF16)

    ff = jnp.zeros((tm, D_MODEL), F32)
    for c in range(D_FF // FFN_CK):
        lo = c * FFN_CK
        a = jnp.dot(h2, wup_ref[:, lo:lo + FFN_CK].astype(BF16), preferred_element_type=F32)
        g = jnp.dot(h2, wup_ref[:, D_FF + lo:D_FF + lo + FFN_CK].astype(BF16), preferred_element_type=F32)
        prev8 = tail_sc[:, lo:lo + FFN_CK]
        ac = cb_ref[:, lo:lo + FFN_CK]
        for j in range(CONV_FFN):
            sh = CONV_FFN - 1 - j
            a_s = a if sh == 0 else _shift_rows(prev8, a, sh)
            ac = ac + a_s * cw_ref[j:j + 1, lo:lo + FFN_CK]
        tail_sc[:, lo:lo + FFN_CK] = a[tm - SUBLANES:]
        u = (_gelu_tanh(ac) * g).astype(BF16)
        ff = ff + jnp.dot(u, wdn_ref[lo:lo + FFN_CK, :].astype(BF16), preferred_element_type=F32)

    x2 = x1 + gate2 * ff
    o_ref[...] = _rms_norm(x2, gf_ref[...])


def _out_ffn(x2d, lru, attn, mod, w_out, g2, w_up, conv_w, conv_b, w_down, g_final):
    s = x2d.shape[0]
    tm = min(FFN_TM, s)
    row = lambda w: pl.BlockSpec((tm, w), lambda i: (i, 0))
    const = lambda shape: pl.BlockSpec(shape, lambda i: (0, 0), pipeline_mode=pl.Buffered(1))
    return pl.pallas_call(
        functools.partial(_ffn_kernel, tm=tm),
        grid=(s // tm,),
        in_specs=[row(D_MODEL), row(LRU_WIDTH), row(ATTN_WIDTH),
                  const((1, 6 * D_MODEL)),
                  const((D_MODEL, D_MODEL)), const((1, D_MODEL)),
                  const((D_MODEL, 2 * D_FF)),
                  const((CONV_FFN, D_FF)), const((1, D_FF)),
                  const((D_FF, D_MODEL)), const((1, D_MODEL))],
        out_specs=row(D_MODEL),
        out_shape=jax.ShapeDtypeStruct((s, D_MODEL), F32),
        scratch_shapes=[pltpu.VMEM((SUBLANES, D_FF), F32)],
        compiler_params=pltpu.CompilerParams(dimension_semantics=("arbitrary",),
                                             vmem_limit_bytes=FFN_VMEM_LIMIT),
        name="out_ffn",
    )(x2d, lru, attn, mod, w_out, g2, w_up, conv_w, conv_b, w_down, g_final)


def _block_diag(w):
    nb, bs, _ = w.shape
    eye = jnp.eye(nb, dtype=w.dtype)
    return (w[:, :, None, :] * eye[:, None, :, None]).reshape(nb * bs, nb * bs)


def kernel(x, c, w_ada, b_ada, g_norm1, w_in, conv_lru_w, conv_lru_b, lru_wa, lru_ba, lru_wx, lru_bx, lru_lambda,
           lam_q1, lam_k1, lam_q2, lam_k2, g_subln, w_out, g_norm2, w_up, conv_ffn_w, conv_ffn_b, w_down, rel_bias,
           g_final):
    b, s, d = x.shape
    assert b == 1 and d == D_MODEL and w_ada.shape[0] == 1
    x2d = x.reshape(s, d)
    row = lambda a: a.reshape(1, -1)

    mod = _adaln_mod(c.reshape(d, 1), w_ada[0], row(b_ada[0]))

    tq, tk = min(ATT_TQ, s), min(ATT_TK, s)
    w_gates = (0.5 * jnp.concatenate([_block_diag(lru_wa[0]), _block_diag(lru_wx[0])], axis=1)).astype(BF16)
    b_gates = 0.5 * jnp.concatenate([lru_ba[0], lru_bx[0]]).reshape(1, -1)
    q, kt, v, lru = _in_proj_lru(x2d, mod, row(g_norm1[0]), w_in[0], conv_lru_w[0], row(conv_lru_b[0]),
                                 w_gates, b_gates, row(lru_lambda[0]), tk)

    bias = _bias_tiles(rel_bias.T.reshape(-1), tq, tk)
    lamv = jnp.stack([lam_q1[0], lam_k1[0], lam_q2[0], lam_k2[0]])
    attn = _diff_attn(q, kt, v, bias, lamv, row(g_subln[0]), tq, tk)

    out = _out_ffn(x2d, lru, attn, mod, w_out[0].astype(BF16), row(g_norm2[0]), w_up[0],
                   conv_ffn_w[0], row(conv_ffn_b[0]), w_down[0], row(g_final))
    return out.reshape(b, s, d)
```

```python
import functools
import math

import jax
import jax.numpy as jnp
from jax import lax
from jax.experimental import pallas as pl
from jax.experimental.pallas import tpu as pltpu

F32 = jnp.float32
BF16 = jnp.bfloat16

D_MODEL = 1024
LRU_WIDTH = 512
LRU_BLOCKS = 8
LRU_BLOCK = LRU_WIDTH // LRU_BLOCKS
CONV_LRU = 4
LRU_C = 8.0
N_HEADS = 4
HEAD_DIM = 64
V_DIM = 2 * HEAD_DIM
QK_WIDTH = N_HEADS * 2 * HEAD_DIM
ATTN_WIDTH = N_HEADS * V_DIM
D_IN = 2 * QK_WIDTH + ATTN_WIDTH + 2 * LRU_WIDTH
D_FF = 3 * D_MODEL
CONV_FFN = 3
NUM_BUCKETS = 32
MAX_EXACT = NUM_BUCKETS // 2
MAX_DISTANCE = 128
EPS = 1e-6
NEG_INF = -1e30
LAMBDA_INIT = 0.8 - 0.6 * math.exp(-0.3 * 0)
LOG2E = math.log2(math.e)

LANES = 128
SUBLANES = 8
VMEM_BYTES = 64 * 1024 * 1024
VMEM_LIMIT = VMEM_BYTES - 8 * 1024 * 1024
FFN_VMEM_LIMIT = VMEM_BYTES - 1024 * 1024

LRU_BLOCK_T = 32
LRU_PITCH = 40
LANE_COLS = LRU_WIDTH // LANES

MOD_TN = 1536
ATT_TQ = 512
ATT_TK = 512
FFN_TM = 512
FFN_CK = 1536


def _rms_norm(x, g):
    y = x * lax.rsqrt(jnp.mean(x * x, axis=-1, keepdims=True) + EPS)
    return y * g


def _gelu_tanh(x):
    cdf = 0.5 * (1.0 + jnp.tanh(math.sqrt(2.0 / math.pi) * (x + 0.044715 * (x * x * x))))
    return x * cdf


def _shift_rows(prev8, x, s):
    ext = jnp.concatenate([prev8, x], axis=0)
    return pltpu.roll(ext, s, 0)[SUBLANES:]


def _adaln_kernel(c_ref, w_ref, b_ref, o_ref):
    c = c_ref[...]
    cond = c * jax.nn.sigmoid(c)
    o_ref[...] = jnp.sum(cond * w_ref[...], axis=0, keepdims=True) + b_ref[...]


def _adaln_mod(c_col, w_ada, b_ada):
    d, n = w_ada.shape
    return pl.pallas_call(
        _adaln_kernel,
        grid=(n // MOD_TN,),
        in_specs=[pl.BlockSpec((d, 1), lambda j: (0, 0)),
                  pl.BlockSpec((d, MOD_TN), lambda j: (0, j)),
                  pl.BlockSpec((1, MOD_TN), lambda j: (0, j))],
        out_specs=pl.BlockSpec((1, MOD_TN), lambda j: (0, j)),
        out_shape=jax.ShapeDtypeStruct((1, n), F32),
        compiler_params=pltpu.CompilerParams(vmem_limit_bytes=VMEM_LIMIT),
        name="adaln_mod",
    )(c_col, w_ada, b_ada)


def _time_major_rows(ref, base):
    return [jnp.concatenate([ref[c, pl.ds(base + k, SUBLANES, stride=LRU_PITCH), :] for c in range(LANE_COLS)],
                            axis=1) for k in range(LRU_BLOCK_T)]


def _lru_conv_gates(xs, xprev, cw, cb, wg, bg):
    sub = lax.broadcasted_iota(jnp.int32, xs[0].shape, 0)
    back = {}
    for j in range(1, CONV_LRU):
        y = jnp.where(sub == SUBLANES - 1, xprev[CONV_LRU - 1 - j], xs[LRU_BLOCK_T - j])
        back[-j] = pltpu.roll(y, 1, 0)
    at = lambda k: xs[k] if k >= 0 else back[k]
    xcs = []
    for k in range(LRU_BLOCK_T):
        xc = cb
        for j in range(CONV_LRU):
            xc = xc + at(k - (CONV_LRU - 1 - j)) * cw[j:j + 1]
        xcs.append(xc)
    xc = jnp.concatenate(xcs, axis=0)
    gates = jnp.dot(xc.astype(BF16), wg, preferred_element_type=F32) + bg
    return xc, gates


def _lru_recur(xc, gates, gate, f_prev, softplus):
    t = jnp.tanh(gates)
    ig = 0.5 * t[:, LRU_WIDTH:] + 0.5
    half_rate = (-0.5 * LRU_C) * softplus
    log_a = half_rate * t[:, :LRU_WIDTH] + half_rate
    a = jnp.exp(log_a)
    th = jnp.tanh(log_a)
    u = -2.0 * th / (1.0 - th)
    b = jnp.where(u > 0.0, u * lax.rsqrt(u), 0.0) * (ig * xc)

    hs, prods = [], []
    for k in range(LRU_BLOCK_T):
        ak = a[k * SUBLANES:(k + 1) * SUBLANES]
        bk = b[k * SUBLANES:(k + 1) * SUBLANES]
        hs.append(bk if k == 0 else ak * hs[-1] + bk)
        prods.append(ak if k == 0 else ak * prods[-1])

    sub = lax.broadcasted_iota(jnp.int32, f_prev.shape, 0)
    start = pltpu.roll(f_prev, 1, 0)
    fa = prods[-1]
    fb = jnp.where(sub == 0, hs[-1] + fa * start, hs[-1])
    sh = 1
    while sh < SUBLANES:
        valid = sub >= sh
        fa_s = jnp.where(valid, pltpu.roll(fa, sh, 0), 1.0)
        fb_s = jnp.where(valid, pltpu.roll(fb, sh, 0), 0.0)
        fb = fa * fb_s + fb
        fa = fa * fa_s
        sh *= 2
    f = fb
    before = jnp.where(sub == 0, start, pltpu.roll(f, 1, 0))
    outs = [(hs[k] + prods[k] * before) * gate[k * SUBLANES:(k + 1) * SUBLANES] for k in range(LRU_BLOCK_T)]
    return outs, f


def _in_proj_lru_kernel(x_ref, mod_ref, g_ref, w_ref, cw_ref, cb_ref, wg_ref, bg_ref, lam_ref,
                        q_ref, k_ref, v_ref, lru_ref, w_sc, xr_sc, yg_sc, o_sc, tail_sc, f_sc, *, tm):
    i = pl.program_id(0)

    @pl.when(i == 0)
    def _():
        w_sc[...] = w_ref[...].astype(BF16)
        xr_sc[...] = jnp.zeros(xr_sc.shape, F32)
        yg_sc[...] = jnp.zeros(yg_sc.shape, F32)

    @pl.when(i <= 1)
    def _():
        tail_sc[...] = jnp.zeros(tail_sc.shape, F32)
        f_sc[...] = jnp.zeros(f_sc.shape, F32)

    tc = SUBLANES * LRU_BLOCK_T
    assert tm == 2 * tc
    chunk_rows = SUBLANES * LRU_PITCH
    z = -lam_ref[...]
    softplus = jnp.maximum(z, 0.0) + jnp.log1p(jnp.exp(-jnp.abs(z)))
    lru_w = (cw_ref[...], cb_ref[...], wg_ref[...], bg_ref[...])

    x = x_ref[...]
    shift1 = mod_ref[:, 0:D_MODEL]
    scale1 = mod_ref[:, D_MODEL:2 * D_MODEL]
    h = (_rms_norm(x, g_ref[...] * (1.0 + scale1)) + shift1).astype(BF16)
    ones = jnp.ones((x.shape[0], V_DIM), BF16)

    def proj(lo, width):
        return jnp.dot(h, w_sc[:, lo:lo + width], preferred_element_type=F32)

    def emit(outs, base):
        for k, out in enumerate(outs):
            for c in range(LANE_COLS):
                o_sc[c, pl.ds(base + k, SUBLANES, stride=LRU_PITCH), :] = out[:, c * LANES:(c + 1) * LANES]

    half = QK_WIDTH // 2
    pq_a = proj(0, half)
    xs0 = _time_major_rows(xr_sc, 0)
    xprev = [tail_sc[j] for j in range(CONV_LRU - 1)]
    xc0, gates0 = _lru_conv_gates(xs0, xprev, *lru_w)
    pq = jnp.concatenate([pq_a, proj(half, half)], axis=1)
    for hd in range(N_HEADS):
        q_ref[hd] = (pq[:, hd * V_DIM:(hd + 1) * V_DIM] * (HEAD_DIM ** -0.5 * LOG2E)).astype(BF16)

    pk_a = proj(QK_WIDTH, half)
    gate0 = _gelu_tanh(jnp.concatenate(_time_major_rows(yg_sc, 0), axis=0))
    outs0, f0 = _lru_recur(xc0, gates0, gate0, f_sc[...], softplus)
    emit(outs0, 0)
    pk = jnp.concatenate([pk_a, proj(QK_WIDTH + half, half)], axis=1)
    for hd in range(N_HEADS):
        k_ref[hd, 0] = pk[:, hd * V_DIM:(hd + 1) * V_DIM].T.astype(BF16)

    pv_a = proj(2 * QK_WIDTH, half)
    xs1 = _time_major_rows(xr_sc, chunk_rows)
    xc1, gates1 = _lru_conv_gates(xs1, xs0[LRU_BLOCK_T - (CONV_LRU - 1):], *lru_w)
    pv = jnp.concatenate([pv_a, proj(2 * QK_WIDTH + half, half)], axis=1)
    for hd in range(N_HEADS):
        v_ref[hd] = jnp.concatenate([pv[:, hd * V_DIM:(hd + 1) * V_DIM].astype(BF16), ones], axis=1)

    base = 2 * QK_WIDTH + ATTN_WIDTH
    pxr_a = proj(base, half)
    gate1 = _gelu_tanh(jnp.concatenate(_time_major_rows(yg_sc, chunk_rows), axis=0))
    outs1, f1 = _lru_recur(xc1, gates1, gate1, f0, softplus)
    emit(outs1, chunk_rows)
    pxr = jnp.concatenate([pxr_a, proj(base + half, half)], axis=1)
    f_sc[...] = f1
    for j in range(CONV_LRU - 1):
        tail_sc[j] = xs1[LRU_BLOCK_T - (CONV_LRU - 1) + j]

    pyg = proj(base + LRU_WIDTH, LRU_WIDTH)
    for n in range(tm // LRU_BLOCK_T):
        rows = slice(n * LRU_BLOCK_T, (n + 1) * LRU_BLOCK_T)
        pad = slice(n * LRU_PITCH, n * LRU_PITCH + LRU_BLOCK_T)
        for c in range(LANE_COLS):
            cols = slice(c * LANES, (c + 1) * LANES)
            lru_ref[rows, cols] = o_sc[c, pad, :].astype(lru_ref.dtype)
            xr_sc[c, pad, :] = pxr[rows, cols]
            yg_sc[c, pad, :] = pyg[rows, cols]


def _in_proj_lru(x2d, mod, g1, w_in, conv_w, conv_b, w_gates_bf16, b_gates, lam, tm):
    s = x2d.shape[0]
    n = s // tm
    assert tm == 2 * SUBLANES * LRU_BLOCK_T
    pad_rows = (tm // LRU_BLOCK_T) * LRU_PITCH
    cur = lambda i: jnp.minimum(i, n - 1)
    const = lambda shape: pl.BlockSpec(shape, lambda i: (0, 0))
    return pl.pallas_call(
        functools.partial(_in_proj_lru_kernel, tm=tm),
        grid=(n + 1,),
        in_specs=[pl.BlockSpec((tm, D_MODEL), lambda i: (cur(i), 0)),
                  const((1, 6 * D_MODEL)), const((1, D_MODEL)),
                  pl.BlockSpec((D_MODEL, D_IN), lambda i: (0, 0), pipeline_mode=pl.Buffered(1)),
                  const((CONV_LRU, LRU_WIDTH)), const((1, LRU_WIDTH)),
                  const((LRU_WIDTH, 2 * LRU_WIDTH)), const((1, 2 * LRU_WIDTH)),
                  const((1, LRU_WIDTH))],
        out_specs=[pl.BlockSpec((N_HEADS, tm, V_DIM), lambda i: (0, cur(i), 0)),
                   pl.BlockSpec((N_HEADS, 1, V_DIM, tm), lambda i: (0, cur(i), 0, 0)),
                   pl.BlockSpec((N_HEADS, tm, 2 * V_DIM), lambda i: (0, cur(i), 0)),
                   pl.BlockSpec((tm, LRU_WIDTH), lambda i: (jnp.maximum(i - 1, 0), 0))],
        out_shape=[jax.ShapeDtypeStruct((N_HEADS, s, V_DIM), BF16),
                   jax.ShapeDtypeStruct((N_HEADS, n, V_DIM, tm), BF16),
                   jax.ShapeDtypeStruct((N_HEADS, s, 2 * V_DIM), BF16),
                   jax.ShapeDtypeStruct((s, LRU_WIDTH), BF16)],
        scratch_shapes=[pltpu.VMEM((D_MODEL, D_IN), BF16),
                        pltpu.VMEM((LANE_COLS, pad_rows, LANES), F32),
                        pltpu.VMEM((LANE_COLS, pad_rows, LANES), F32),
                        pltpu.VMEM((LANE_COLS, pad_rows, LANES), F32),
                        pltpu.VMEM((CONV_LRU - 1, SUBLANES, LRU_WIDTH), F32),
                        pltpu.VMEM((SUBLANES, LRU_WIDTH), F32)],
        compiler_params=pltpu.CompilerParams(dimension_semantics=("arbitrary",),
                                             vmem_limit_bytes=VMEM_LIMIT),
        name="in_proj_lru",
    )(x2d, mod, g1, w_in, conv_w, conv_b, w_gates_bf16, b_gates, lam)


def _write_bias_tiles(table_ref, hd, o_ref, tq, tk, nd):
    blk = MAX_DISTANCE
    qpos = lax.broadcasted_iota(jnp.int32, (blk, blk), 0)
    kpos = lax.broadcasted_iota(jnp.int32, (blk, blk), 1)
    far = table_ref[hd * NUM_BUCKETS + NUM_BUCKETS - 1]

    def band_block(offset):
        rel = offset * blk + qpos - kpos
        n = jnp.maximum(rel, 0)
        nf = jnp.maximum(n, 1).astype(F32)
        y = (jnp.log(nf / MAX_EXACT) / math.log(MAX_DISTANCE / MAX_EXACT) * (NUM_BUCKETS - MAX_EXACT))
        val = jnp.zeros((blk, blk), F32)
        for b in range(NUM_BUCKETS):
            hit = (n == b) if b < MAX_EXACT else ((n >= MAX_EXACT) & (y >= b - MAX_EXACT))
            val = jnp.where(hit, (table_ref[hd * NUM_BUCKETS + b] - far) * LOG2E, val)
        return jnp.where(rel >= 0, val, NEG_INF)

    blocks = {0: band_block(0), 1: band_block(1)}
    zeros = jnp.zeros((blk, blk), F32)
    neg = jnp.full((blk, blk), NEG_INF, F32)
    for dd in range(nd):
        for a in range(tq // blk):
            for b in range(tk // blk):
                off = ((dd + 1) * tk - tq) // blk + a - b
                o_ref[dd, a * blk:(a + 1) * blk, b * blk:(b + 1) * blk] = (
                    neg if off < 0 else blocks.get(off, zeros))
    o_ref[nd] = jnp.zeros((tq, tk), F32)
    o_ref[nd + 1] = jnp.full((tq, tk), NEG_INF, F32)


def _attn_kernel(q_ref, qn_ref, k_ref, v_ref, table_ref, lamv_ref, gs_ref, o_ref, qs_sc, sa_sc, sb_sc, pa_sc, pb_sc,
                 m_sc, acc_sc, bias_ref, *, tq, tk, nd):
    i = pl.program_id(1)

    @pl.when(i == 0)
    def _():
        _write_bias_tiles(table_ref, pl.program_id(0), bias_ref, tq, tk, nd)

    def stack_maps(q):
        lane = lax.broadcasted_iota(jnp.int32, q.shape, 1)
        zero = jnp.zeros_like(q)
        return jnp.concatenate([jnp.where(lane < HEAD_DIM, q, zero), jnp.where(lane >= HEAD_DIM, q, zero)], axis=0)

    qs_sc[0] = stack_maps(q_ref[0])
    qs_sc[1] = stack_maps(qn_ref[0])

    def limits(qi):
        q_start = qi * tq
        n_far = jnp.maximum(q_start - (MAX_DISTANCE - 1), 0) // tk
        last = (q_start + tq - 1) // tk
        return q_start, n_far, last

    _, _, last = limits(i)

    def scores(slot, qi, j, s_ref, part_ref, far_only=False):
        q_start, n_far, last_q = limits(qi)
        jc = jnp.minimum(j, last_q)
        s = jnp.dot(qs_sc[slot], k_ref[0, jc], preferred_element_type=F32)
        if not far_only:
            dd = (q_start + tq - (jc + 1) * tk) // tk
            idx = jnp.where(j > last_q, nd + 1, jnp.where(j < n_far, nd, dd))
            s = (s.reshape(2, tq, tk) + bias_ref[idx][None]).reshape(2 * tq, tk)
        s_ref[...] = s
        part_ref[...] = functools.reduce(
            jnp.maximum, [s[:, c * LANES:(c + 1) * LANES] for c in range(tk // LANES)])

    def accumulate(s_ref, part_ref, j):
        jc = jnp.minimum(j, last)
        vb = v_ref[0, pl.ds(pl.multiple_of(jc * tk, tk), tk), :]
        m_prev = m_sc[...]
        m_new = jnp.maximum(m_prev, jnp.max(part_ref[...], axis=1, keepdims=True))
        alpha = jnp.exp2(m_prev - m_new)
        p = jnp.exp2(s_ref[...] - jnp.concatenate([m_new] * (tk // LANES), axis=1))
        acc_sc[...] = (jnp.concatenate([alpha, alpha], axis=1) * acc_sc[...]
                       + jnp.dot(p.astype(BF16), vb, preferred_element_type=F32))
        m_sc[...] = m_new

    @pl.when(i == 0)
    def _():
        scores(0, i, 0, sa_sc, pa_sc)

    m_sc[...] = jnp.full(m_sc.shape, NEG_INF, F32)
    acc_sc[...] = jnp.zeros(acc_sc.shape, F32)

    def two_blocks(j):
        scores(0, i, j + 1, sb_sc, pb_sc)
        accumulate(sa_sc, pa_sc, j)
        to_next = (j + 2 > last).astype(jnp.int32)
        scores(to_next, i + to_next, (1 - to_next) * (j + 2), sa_sc, pa_sc)
        accumulate(sb_sc, pb_sc, j + 1)

    def far_oct_body(t, carry):
        for u in range(4):
            j = 8 * t + 2 * u
            scores(0, i, j + 1, sb_sc, pb_sc, far_only=True)
            accumulate(sa_sc, pa_sc, j)
            scores(0, i, j + 2, sa_sc, pa_sc, far_only=True)
            accumulate(sb_sc, pb_sc, j + 1)
        return carry

    def oct_body(t, carry):
        for u in range(4):
            two_blocks(j0 + 8 * t + 2 * u)
        return carry

    def quad_body(t, carry):
        two_blocks(j0 + 8 * n_octs + 4 * t)
        two_blocks(j0 + 8 * n_octs + 4 * t + 2)
        return carry

    def pair_body(t, carry):
        two_blocks(j0 + 4 * n_quads + 2 * t)
        return carry

    _, n_far, _ = limits(i)
    n_far_octs = jnp.maximum(n_far - 1, 0) // 8
    j0 = 8 * n_far_octs
    n_octs = (last + 1 - j0) // 8
    n_quads = (last + 1 - j0) // 4
    lax.fori_loop(0, n_far_octs, far_oct_body, 0)
    lax.fori_loop(0, n_octs, oct_body, 0)
    lax.fori_loop(0, n_quads - 2 * n_octs, quad_body, 0)
    lax.fori_loop(0, (last + 1 - j0) // 2 - 2 * n_quads, pair_body, 0)

    def single_body(t, carry):
        scores(1, i + 1, 0, sb_sc, pb_sc)
        accumulate(sa_sc, pa_sc, last)
        sa_sc[...] = sb_sc[...]
        pa_sc[...] = pb_sc[...]
        return carry

    lax.fori_loop(0, (last + 1) % 2, single_body, 0)

    lv = lamv_ref[...]
    d1 = jnp.sum(lv[0:1] * lv[1:2], axis=1, keepdims=True)
    d2 = jnp.sum(lv[2:3] * lv[3:4], axis=1, keepdims=True)
    lam = jnp.exp(d1) - jnp.exp(d2) + LAMBDA_INIT

    acc = acc_sc[...]
    out = acc[:, :V_DIM] / acc[:, V_DIM:]
    diff = out[:tq] - lam * out[tq:]
    o_ref[...] = (_rms_norm(diff, gs_ref[...]) * (1.0 - LAMBDA_INIT)).astype(o_ref.dtype)


def _diff_attn(q, kt, v_aug, table_flat, lamv, g_subln, tq, tk):
    assert tq % tk == 0 and tk % MAX_DISTANCE == 0
    s = q.shape[1]
    nq = s // tq
    nd = tq // tk + 1
    return pl.pallas_call(
        functools.partial(_attn_kernel, tq=tq, tk=tk, nd=nd),
        grid=(N_HEADS, nq),
        in_specs=[pl.BlockSpec((1, tq, V_DIM), lambda h, i: (h, i, 0)),
                  pl.BlockSpec((1, tq, V_DIM), lambda h, i: (h, jnp.minimum(i + 1, nq - 1), 0)),
                  pl.BlockSpec((1, s // tk, V_DIM, tk), lambda h, i: (h, 0, 0, 0)),
                  pl.BlockSpec((1, s, 2 * V_DIM), lambda h, i: (h, 0, 0)),
                  pl.BlockSpec(memory_space=pltpu.SMEM),
                  pl.BlockSpec((4, HEAD_DIM), lambda h, i: (0, 0)),
                  pl.BlockSpec((1, V_DIM), lambda h, i: (0, 0))],
        out_specs=pl.BlockSpec((tq, V_DIM), lambda h, i: (i, h)),
        out_shape=jax.ShapeDtypeStruct((s, ATTN_WIDTH), BF16),
        scratch_shapes=[pltpu.VMEM((2, 2 * tq, V_DIM), BF16),
                        pltpu.VMEM((2 * tq, tk), F32),
                        pltpu.VMEM((2 * tq, tk), F32),
                        pltpu.VMEM((2 * tq, LANES), F32),
                        pltpu.VMEM((2 * tq, LANES), F32),
                        pltpu.VMEM((2 * tq, LANES), F32),
                        pltpu.VMEM((2 * tq, 2 * V_DIM), F32),
                        pltpu.VMEM((nd + 2, tq, tk), F32)],
        compiler_params=pltpu.CompilerParams(dimension_semantics=("arbitrary", "arbitrary"),
                                             vmem_limit_bytes=VMEM_LIMIT),
        name="diff_attn",
    )(q, q, kt, v_aug, table_flat, lamv, g_subln)


def _ffn_kernel(x_ref, lru_ref, attn_ref, mod_ref, wo_ref, g2_ref, wup_ref, cw_ref, cb_ref, wdn_ref, gf_ref,
                o_ref, tail_sc, *, tm):
    @pl.when(pl.program_id(0) == 0)
    def _():
        tail_sc[...] = jnp.zeros(tail_sc.shape, F32)

    gate1 = mod_ref[:, 2 * D_MODEL:3 * D_MODEL]
    shift2 = mod_ref[:, 3 * D_MODEL:4 * D_MODEL]
    scale2 = mod_ref[:, 4 * D_MODEL:5 * D_MODEL]
    gate2 = mod_ref[:, 5 * D_MODEL:6 * D_MODEL]

    mix = (jnp.dot(lru_ref[...], wo_ref[0:LRU_WIDTH, :], preferred_element_type=F32)
           + jnp.dot(attn_ref[...], wo_ref[LRU_WIDTH:, :], preferred_element_type=F32))
    x1 = x_ref[...] + gate1 * mix
    h2 = (_rms_norm(x1, g2_ref[...]) * (1.0 + scale2) + shift2).astype(BF16)

    ff = jnp.zeros((tm, D_MODEL), F32)
    for c in range(D_FF // FFN_CK):
        lo = c * FFN_CK
        a = jnp.dot(h2, wup_ref[:, lo:lo + FFN_CK].astype(BF16), preferred_element_type=F32)
        g = jnp.dot(h2, wup_ref[:, D_FF + lo:D_FF + lo + FFN_CK].astype(BF16), preferred_element_type=F32)
        prev8 = tail_sc[:, lo:lo + FFN_CK]
        ac = cb_ref[:, lo:lo + FFN_CK]
        for j in range(CONV_FFN):
            sh = CONV_FFN - 1 - j
            a_s = a if sh == 0 else _shift_rows(prev8, a, sh)
            ac = ac + a_s * cw_ref[j:j + 1, lo:lo + FFN_CK]
        tail_sc[:, lo:lo + FFN_CK] = a[tm - SUBLANES:]
        u = (_gelu_tanh(ac) * g).astype(BF16)
        ff = ff + jnp.dot(u, wdn_ref[lo:lo + FFN_CK, :].astype(BF16), preferred_element_type=F32)

    x2 = x1 + gate2 * ff
    o_ref[...] = _rms_norm(x2, gf_ref[...])


def _out_ffn(x2d, lru, attn, mod, w_out, g2, w_up, conv_w, conv_b, w_down, g_final):
    s = x2d.shape[0]
    tm = min(FFN_TM, s)
    row = lambda w: pl.BlockSpec((tm, w), lambda i: (i, 0))
    const = lambda shape: pl.BlockSpec(shape, lambda i: (0, 0), pipeline_mode=pl.Buffered(1))
    return pl.pallas_call(
        functools.partial(_ffn_kernel, tm=tm),
        grid=(s // tm,),
        in_specs=[row(D_MODEL), row(LRU_WIDTH), row(ATTN_WIDTH),
                  const((1, 6 * D_MODEL)),
                  const((D_MODEL, D_MODEL)), const((1, D_MODEL)),
                  const((D_MODEL, 2 * D_FF)),
                  const((CONV_FFN, D_FF)), const((1, D_FF)),
                  const((D_FF, D_MODEL)), const((1, D_MODEL))],
        out_specs=row(D_MODEL),
        out_shape=jax.ShapeDtypeStruct((s, D_MODEL), F32),
        scratch_shapes=[pltpu.VMEM((SUBLANES, D_FF), F32)],
        compiler_params=pltpu.CompilerParams(dimension_semantics=("arbitrary",),
                                             vmem_limit_bytes=FFN_VMEM_LIMIT),
        name="out_ffn",
    )(x2d, lru, attn, mod, w_out, g2, w_up, conv_w, conv_b, w_down, g_final)


def _block_diag(w):
    nb, bs, _ = w.shape
    eye = jnp.eye(nb, dtype=w.dtype)
    return (w[:, :, None, :] * eye[:, None, :, None]).reshape(nb * bs, nb * bs)


def kernel(x, c, w_ada, b_ada, g_norm1, w_in, conv_lru_w, conv_lru_b, lru_wa, lru_ba, lru_wx, lru_bx, lru_lambda,
           lam_q1, lam_k1, lam_q2, lam_k2, g_subln, w_out, g_norm2, w_up, conv_ffn_w, conv_ffn_b, w_down, rel_bias,
           g_final):
    b, s, d = x.shape
    assert b == 1 and d == D_MODEL and w_ada.shape[0] == 1
    x2d = x.reshape(s, d)
    row = lambda a: a.reshape(1, -1)

    mod = _adaln_mod(c.reshape(d, 1), w_ada[0], row(b_ada[0]))

    tq, tk = min(ATT_TQ, s), min(ATT_TK, s)
    w_gates = (0.5 * jnp.concatenate([_block_diag(lru_wa[0]), _block_diag(lru_wx[0])], axis=1)).astype(BF16)
    b_gates = 0.5 * jnp.concatenate([lru_ba[0], lru_bx[0]]).reshape(1, -1)
    q, kt, v, lru = _in_proj_lru(x2d, mod, row(g_norm1[0]), w_in[0], conv_lru_w[0], row(conv_lru_b[0]),
                                 w_gates, b_gates, row(lru_lambda[0]), tk)

    lamv = jnp.stack([lam_q1[0], lam_k1[0], lam_q2[0], lam_k2[0]])
    attn = _diff_attn(q, kt, v, rel_bias.T.reshape(-1), lamv, row(g_subln[0]), tq, tk)

    out = _out_ffn(x2d, lru, attn, mod, w_out[0].astype(BF16), row(g_norm2[0]), w_up[0],
                   conv_ffn_w[0], row(conv_ffn_b[0]), w_down[0], row(g_final))
    return out.reshape(b, s, d)
```

```python
import functools
import math

import jax
import jax.numpy as jnp
from jax import lax
from jax.experimental import pallas as pl
from jax.experimental.pallas import tpu as pltpu

F32 = jnp.float32
BF16 = jnp.bfloat16

D_MODEL = 1024
LRU_WIDTH = 512
LRU_BLOCKS = 8
LRU_BLOCK = LRU_WIDTH // LRU_BLOCKS
CONV_LRU = 4
LRU_C = 8.0
N_HEADS = 4
HEAD_DIM = 64
V_DIM = 2 * HEAD_DIM
QK_WIDTH = N_HEADS * 2 * HEAD_DIM
ATTN_WIDTH = N_HEADS * V_DIM
D_IN = 2 * QK_WIDTH + ATTN_WIDTH + 2 * LRU_WIDTH
D_FF = 3 * D_MODEL
CONV_FFN = 3
NUM_BUCKETS = 32
MAX_EXACT = NUM_BUCKETS // 2
MAX_DISTANCE = 128
EPS = 1e-6
NEG_INF = -1e30
LAMBDA_INIT = 0.8 - 0.6 * math.exp(-0.3 * 0)
LOG2E = math.log2(math.e)

LANES = 128
SUBLANES = 8
VMEM_BYTES = 64 * 1024 * 1024
VMEM_LIMIT = VMEM_BYTES - 8 * 1024 * 1024
FFN_VMEM_LIMIT = VMEM_BYTES - 1024 * 1024

LRU_BLOCK_T = 32
LRU_PITCH = 40
LANE_COLS = LRU_WIDTH // LANES

MOD_TN = 1536
ATT_TQ = 512
ATT_TK = 512
FFN_TM = 512
FFN_CK = 1536


def _rms_norm(x, g):
    y = x * lax.rsqrt(jnp.mean(x * x, axis=-1, keepdims=True) + EPS)
    return y * g


def _gelu_tanh(x):
    cdf = 0.5 * (1.0 + jnp.tanh(math.sqrt(2.0 / math.pi) * (x + 0.044715 * (x * x * x))))
    return x * cdf


def _shift_rows(prev8, x, s):
    ext = jnp.concatenate([prev8, x], axis=0)
    return pltpu.roll(ext, s, 0)[SUBLANES:]


def _adaln_kernel(c_ref, w_ref, b_ref, o_ref):
    c = c_ref[...]
    cond = c * jax.nn.sigmoid(c)
    o_ref[...] = jnp.sum(cond * w_ref[...], axis=0, keepdims=True) + b_ref[...]


def _adaln_mod(c_col, w_ada, b_ada):
    d, n = w_ada.shape
    return pl.pallas_call(
        _adaln_kernel,
        grid=(n // MOD_TN,),
        in_specs=[pl.BlockSpec((d, 1), lambda j: (0, 0)),
                  pl.BlockSpec((d, MOD_TN), lambda j: (0, j)),
                  pl.BlockSpec((1, MOD_TN), lambda j: (0, j))],
        out_specs=pl.BlockSpec((1, MOD_TN), lambda j: (0, j)),
        out_shape=jax.ShapeDtypeStruct((1, n), F32),
        compiler_params=pltpu.CompilerParams(vmem_limit_bytes=VMEM_LIMIT),
        name="adaln_mod",
    )(c_col, w_ada, b_ada)


def _time_major_rows(ref, base):
    return [jnp.concatenate([ref[c, pl.ds(base + k, SUBLANES, stride=LRU_PITCH), :] for c in range(LANE_COLS)],
                            axis=1) for k in range(LRU_BLOCK_T)]


def _lru_conv_gates(xs, xprev, cw, cb, wg, bg):
    sub = lax.broadcasted_iota(jnp.int32, xs[0].shape, 0)
    back = {}
    for j in range(1, CONV_LRU):
        y = jnp.where(sub == SUBLANES - 1, xprev[CONV_LRU - 1 - j], xs[LRU_BLOCK_T - j])
        back[-j] = pltpu.roll(y, 1, 0)
    at = lambda k: xs[k] if k >= 0 else back[k]
    xcs = []
    for k in range(LRU_BLOCK_T):
        xc = cb
        for j in range(CONV_LRU):
            xc = xc + at(k - (CONV_LRU - 1 - j)) * cw[j:j + 1]
        xcs.append(xc)
    xc = jnp.concatenate(xcs, axis=0)
    gates = jnp.dot(xc.astype(BF16), wg, preferred_element_type=F32) + bg
    return xc, gates


def _lru_recur(xc, gates, gate, f_prev, softplus):
    t = jnp.tanh(gates)
    ig = 0.5 * t[:, LRU_WIDTH:] + 0.5
    half_rate = (-0.5 * LRU_C) * softplus
    log_a = half_rate * t[:, :LRU_WIDTH] + half_rate
    a = jnp.exp(log_a)
    th = jnp.tanh(log_a)
    u = -2.0 * th / (1.0 - th)
    b = jnp.where(u > 0.0, u * lax.rsqrt(u), 0.0) * (ig * xc)

    hs, prods = [], []
    for k in range(LRU_BLOCK_T):
        ak = a[k * SUBLANES:(k + 1) * SUBLANES]
        bk = b[k * SUBLANES:(k + 1) * SUBLANES]
        hs.append(bk if k == 0 else ak * hs[-1] + bk)
        prods.append(ak if k == 0 else ak * prods[-1])

    sub = lax.broadcasted_iota(jnp.int32, f_prev.shape, 0)
    start = pltpu.roll(f_prev, 1, 0)
    fa = prods[-1]
    fb = jnp.where(sub == 0, hs[-1] + fa * start, hs[-1])
    sh = 1
    while sh < SUBLANES:
        valid = sub >= sh
        fa_s = jnp.where(valid, pltpu.roll(fa, sh, 0), 1.0)
        fb_s = jnp.where(valid, pltpu.roll(fb, sh, 0), 0.0)
        fb = fa * fb_s + fb
        fa = fa * fa_s
        sh *= 2
    f = fb
    before = jnp.where(sub == 0, start, pltpu.roll(f, 1, 0))
    outs = [(hs[k] + prods[k] * before) * gate[k * SUBLANES:(k + 1) * SUBLANES] for k in range(LRU_BLOCK_T)]
    return outs, f


def _in_proj_lru_kernel(x_ref, mod_ref, g_ref, w_ref, cw_ref, cb_ref, wg_ref, bg_ref, lam_ref,
                        q_ref, k_ref, v_ref, lru_ref, w_sc, xr_sc, yg_sc, o_sc, tail_sc, f_sc, *, tm):
    i = pl.program_id(0)

    @pl.when(i == 0)
    def _():
        w_sc[...] = w_ref[...].astype(BF16)
        xr_sc[...] = jnp.zeros(xr_sc.shape, F32)
        yg_sc[...] = jnp.zeros(yg_sc.shape, F32)

    @pl.when(i <= 1)
    def _():
        tail_sc[...] = jnp.zeros(tail_sc.shape, F32)
        f_sc[...] = jnp.zeros(f_sc.shape, F32)

    tc = SUBLANES * LRU_BLOCK_T
    assert tm == 2 * tc
    chunk_rows = SUBLANES * LRU_PITCH
    z = -lam_ref[...]
    softplus = jnp.maximum(z, 0.0) + jnp.log1p(jnp.exp(-jnp.abs(z)))
    lru_w = (cw_ref[...], cb_ref[...], wg_ref[...], bg_ref[...])

    x = x_ref[...]
    shift1 = mod_ref[:, 0:D_MODEL]
    scale1 = mod_ref[:, D_MODEL:2 * D_MODEL]
    h = (_rms_norm(x, g_ref[...] * (1.0 + scale1)) + shift1).astype(BF16)
    ones = jnp.ones((x.shape[0], V_DIM), BF16)

    def proj(lo, width):
        return jnp.dot(h, w_sc[:, lo:lo + width], preferred_element_type=F32)

    def emit(outs, base):
        for k, out in enumerate(outs):
            for c in range(LANE_COLS):
                o_sc[c, pl.ds(base + k, SUBLANES, stride=LRU_PITCH), :] = out[:, c * LANES:(c + 1) * LANES]

    half = QK_WIDTH // 2
    pq_a = proj(0, half)
    xs0 = _time_major_rows(xr_sc, 0)
    xprev = [tail_sc[j] for j in range(CONV_LRU - 1)]
    xc0, gates0 = _lru_conv_gates(xs0, xprev, *lru_w)
    pq = jnp.concatenate([pq_a, proj(half, half)], axis=1)
    for hd in range(N_HEADS):
        q_ref[hd] = (pq[:, hd * V_DIM:(hd + 1) * V_DIM] * (HEAD_DIM ** -0.5 * LOG2E)).astype(BF16)

    pk_a = proj(QK_WIDTH, half)
    gate0 = _gelu_tanh(jnp.concatenate(_time_major_rows(yg_sc, 0), axis=0))
    outs0, f0 = _lru_recur(xc0, gates0, gate0, f_sc[...], softplus)
    emit(outs0, 0)
    pk = jnp.concatenate([pk_a, proj(QK_WIDTH + half, half)], axis=1)
    for hd in range(N_HEADS):
        k_ref[hd, 0] = pk[:, hd * V_DIM:(hd + 1) * V_DIM].T.astype(BF16)

    pv_a = proj(2 * QK_WIDTH, half)
    xs1 = _time_major_rows(xr_sc, chunk_rows)
    xc1, gates1 = _lru_conv_gates(xs1, xs0[LRU_BLOCK_T - (CONV_LRU - 1):], *lru_w)
    pv = jnp.concatenate([pv_a, proj(2 * QK_WIDTH + half, half)], axis=1)
    for hd in range(N_HEADS):
        v_ref[hd] = jnp.concatenate([pv[:, hd * V_DIM:(hd + 1) * V_DIM].astype(BF16), ones], axis=1)

    base = 2 * QK_WIDTH + ATTN_WIDTH
    pxr_a = proj(base, half)
    gate1 = _gelu_tanh(jnp.concatenate(_time_major_rows(yg_sc, chunk_rows), axis=0))
    outs1, f1 = _lru_recur(xc1, gates1, gate1, f0, softplus)
    emit(outs1, chunk_rows)
    pxr = jnp.concatenate([pxr_a, proj(base + half, half)], axis=1)
    f_sc[...] = f1
    for j in range(CONV_LRU - 1):
        tail_sc[j] = xs1[LRU_BLOCK_T - (CONV_LRU - 1) + j]

    pyg = proj(base + LRU_WIDTH, LRU_WIDTH)
    for n in range(tm // LRU_BLOCK_T):
        rows = slice(n * LRU_BLOCK_T, (n + 1) * LRU_BLOCK_T)
        pad = slice(n * LRU_PITCH, n * LRU_PITCH + LRU_BLOCK_T)
        for c in range(LANE_COLS):
            cols = slice(c * LANES, (c + 1) * LANES)
            lru_ref[rows, cols] = o_sc[c, pad, :].astype(lru_ref.dtype)
            xr_sc[c, pad, :] = pxr[rows, cols]
            yg_sc[c, pad, :] = pyg[rows, cols]


def _in_proj_lru(x2d, mod, g1, w_in, conv_w, conv_b, w_gates_bf16, b_gates, lam, tm):
    s = x2d.shape[0]
    n = s // tm
    assert tm == 2 * SUBLANES * LRU_BLOCK_T
    pad_rows = (tm // LRU_BLOCK_T) * LRU_PITCH
    cur = lambda i: jnp.minimum(i, n - 1)
    const = lambda shape: pl.BlockSpec(shape, lambda i: (0, 0))
    return pl.pallas_call(
        functools.partial(_in_proj_lru_kernel, tm=tm),
        grid=(n + 1,),
        in_specs=[pl.BlockSpec((tm, D_MODEL), lambda i: (cur(i), 0)),
                  const((1, 6 * D_MODEL)), const((1, D_MODEL)),
                  pl.BlockSpec((D_MODEL, D_IN), lambda i: (0, 0), pipeline_mode=pl.Buffered(1)),
                  const((CONV_LRU, LRU_WIDTH)), const((1, LRU_WIDTH)),
                  const((LRU_WIDTH, 2 * LRU_WIDTH)), const((1, 2 * LRU_WIDTH)),
                  const((1, LRU_WIDTH))],
        out_specs=[pl.BlockSpec((N_HEADS, tm, V_DIM), lambda i: (0, cur(i), 0)),
                   pl.BlockSpec((N_HEADS, 1, V_DIM, tm), lambda i: (0, cur(i), 0, 0)),
                   pl.BlockSpec((N_HEADS, tm, 2 * V_DIM), lambda i: (0, cur(i), 0)),
                   pl.BlockSpec((tm, LRU_WIDTH), lambda i: (jnp.maximum(i - 1, 0), 0))],
        out_shape=[jax.ShapeDtypeStruct((N_HEADS, s, V_DIM), BF16),
                   jax.ShapeDtypeStruct((N_HEADS, n, V_DIM, tm), BF16),
                   jax.ShapeDtypeStruct((N_HEADS, s, 2 * V_DIM), BF16),
                   jax.ShapeDtypeStruct((s, LRU_WIDTH), BF16)],
        scratch_shapes=[pltpu.VMEM((D_MODEL, D_IN), BF16),
                        pltpu.VMEM((LANE_COLS, pad_rows, LANES), F32),
                        pltpu.VMEM((LANE_COLS, pad_rows, LANES), F32),
                        pltpu.VMEM((LANE_COLS, pad_rows, LANES), F32),
                        pltpu.VMEM((CONV_LRU - 1, SUBLANES, LRU_WIDTH), F32),
                        pltpu.VMEM((SUBLANES, LRU_WIDTH), F32)],
        compiler_params=pltpu.CompilerParams(dimension_semantics=("arbitrary",),
                                             vmem_limit_bytes=VMEM_LIMIT),
        name="in_proj_lru",
    )(x2d, mod, g1, w_in, conv_w, conv_b, w_gates_bf16, b_gates, lam)


def _write_bias_tiles(table_ref, hd, o_ref, tq, tk, nd):
    blk = MAX_DISTANCE
    qpos = lax.broadcasted_iota(jnp.int32, (blk, blk), 0)
    kpos = lax.broadcasted_iota(jnp.int32, (blk, blk), 1)
    far = table_ref[hd * NUM_BUCKETS + NUM_BUCKETS - 1]

    def band_block(offset):
        rel = offset * blk + qpos - kpos
        n = jnp.maximum(rel, 0)
        nf = jnp.maximum(n, 1).astype(F32)
        y = (jnp.log(nf / MAX_EXACT) / math.log(MAX_DISTANCE / MAX_EXACT) * (NUM_BUCKETS - MAX_EXACT))
        val = jnp.zeros((blk, blk), F32)
        for b in range(NUM_BUCKETS):
            hit = (n == b) if b < MAX_EXACT else ((n >= MAX_EXACT) & (y >= b - MAX_EXACT))
            val = jnp.where(hit, (table_ref[hd * NUM_BUCKETS + b] - far) * LOG2E, val)
        return jnp.where(rel >= 0, val, NEG_INF)

    blocks = {0: band_block(0), 1: band_block(1)}
    zeros = jnp.zeros((blk, blk), F32)
    neg = jnp.full((blk, blk), NEG_INF, F32)
    for dd in range(nd):
        for a in range(tq // blk):
            for b in range(tk // blk):
                off = ((dd + 1) * tk - tq) // blk + a - b
                o_ref[dd, a * blk:(a + 1) * blk, b * blk:(b + 1) * blk] = (
                    neg if off < 0 else blocks.get(off, zeros))
    o_ref[nd] = jnp.zeros((tq, tk), F32)
    o_ref[nd + 1] = jnp.full((tq, tk), NEG_INF, F32)


def _attn_kernel(q_ref, qn_ref, k_ref, v_ref, table_ref, lamv_ref, gs_ref, o_ref, qs_sc, sa_sc, sb_sc, pa_sc, pb_sc,
                 m_sc, acc_sc, bias_ref, *, tq, tk, nd):
    i = pl.program_id(1)

    @pl.when(i == 0)
    def _():
        _write_bias_tiles(table_ref, pl.program_id(0), bias_ref, tq, tk, nd)

    def stack_maps(q):
        lane = lax.broadcasted_iota(jnp.int32, q.shape, 1)
        zero = jnp.zeros_like(q)
        return jnp.concatenate([jnp.where(lane < HEAD_DIM, q, zero), jnp.where(lane >= HEAD_DIM, q, zero)], axis=0)

    qs_sc[0] = stack_maps(q_ref[0])
    qs_sc[1] = stack_maps(qn_ref[0])

    def limits(qi):
        q_start = qi * tq
        n_far = jnp.maximum(q_start - (MAX_DISTANCE - 1), 0) // tk
        last = (q_start + tq - 1) // tk
        return q_start, n_far, last

    _, _, last = limits(i)

    def scores(slot, qi, j, s_ref, part_ref, far_only=False):
        q_start, n_far, last_q = limits(qi)
        jc = jnp.minimum(j, last_q)
        s = jnp.dot(qs_sc[slot], k_ref[0, jc], preferred_element_type=F32)
        if not far_only:
            dd = (q_start + tq - (jc + 1) * tk) // tk
            idx = jnp.where(j > last_q, nd + 1, jnp.where(j < n_far, nd, dd))
            s = (s.reshape(2, tq, tk) + bias_ref[idx][None]).reshape(2 * tq, tk)
        s_ref[...] = s
        part_ref[...] = functools.reduce(
            jnp.maximum, [s[:, c * LANES:(c + 1) * LANES] for c in range(tk // LANES)])

    def accumulate(s_ref, part_ref, j):
        jc = jnp.minimum(j, last)
        vb = v_ref[0, pl.ds(pl.multiple_of(jc * tk, tk), tk), :]
        m_prev = m_sc[...]
        m_new = jnp.maximum(m_prev, jnp.max(part_ref[...], axis=1, keepdims=True))
        alpha = jnp.exp2(m_prev - m_new)
        p = jnp.exp2(s_ref[...] - jnp.concatenate([m_new] * (tk // LANES), axis=1))
        acc_sc[...] = (jnp.concatenate([alpha, alpha], axis=1) * acc_sc[...]
                       + jnp.dot(p.astype(BF16), vb, preferred_element_type=F32))
        m_sc[...] = m_new

    @pl.when(i == 0)
    def _():
        scores(0, i, 0, sa_sc, pa_sc)

    m_sc[...] = jnp.full(m_sc.shape, NEG_INF, F32)
    acc_sc[...] = jnp.zeros(acc_sc.shape, F32)

    def two_blocks(j):
        scores(0, i, j + 1, sb_sc, pb_sc)
        accumulate(sa_sc, pa_sc, j)
        to_next = (j + 2 > last).astype(jnp.int32)
        scores(to_next, i + to_next, (1 - to_next) * (j + 2), sa_sc, pa_sc)
        accumulate(sb_sc, pb_sc, j + 1)

    def far_oct_body(t, carry):
        for u in range(4):
            j = 8 * t + 2 * u
            scores(0, i, j + 1, sb_sc, pb_sc, far_only=True)
            accumulate(sa_sc, pa_sc, j)
            scores(0, i, j + 2, sa_sc, pa_sc, far_only=True)
            accumulate(sb_sc, pb_sc, j + 1)
        return carry

    def oct_body(t, carry):
        for u in range(4):
            two_blocks(j0 + 8 * t + 2 * u)
        return carry

    def quad_body(t, carry):
        two_blocks(j0 + 8 * n_octs + 4 * t)
        two_blocks(j0 + 8 * n_octs + 4 * t + 2)
        return carry

    def pair_body(t, carry):
        two_blocks(j0 + 4 * n_quads + 2 * t)
        return carry

    _, n_far, _ = limits(i)
    n_far_octs = jnp.maximum(n_far - 1, 0) // 8
    j0 = 8 * n_far_octs
    n_octs = (last + 1 - j0) // 8
    n_quads = (last + 1 - j0) // 4
    lax.fori_loop(0, n_far_octs, far_oct_body, 0)
    lax.fori_loop(0, n_octs, oct_body, 0)
    lax.fori_loop(0, n_quads - 2 * n_octs, quad_body, 0)
    lax.fori_loop(0, (last + 1 - j0) // 2 - 2 * n_quads, pair_body, 0)

    def single_body(t, carry):
        scores(1, i + 1, 0, sb_sc, pb_sc)
        accumulate(sa_sc, pa_sc, last)
        sa_sc[...] = sb_sc[...]
        pa_sc[...] = pb_sc[...]
        return carry

    lax.fori_loop(0, (last + 1) % 2, single_body, 0)

    lv = lamv_ref[...]
    d1 = jnp.sum(lv[0:1] * lv[1:2], axis=1, keepdims=True)
    d2 = jnp.sum(lv[2:3] * lv[3:4], axis=1, keepdims=True)
    lam = jnp.exp(d1) - jnp.exp(d2) + LAMBDA_INIT

    acc = acc_sc[...]
    out = acc[:, :V_DIM] / acc[:, V_DIM:]
    diff = out[:tq] - lam * out[tq:]
    o_ref[...] = (_rms_norm(diff, gs_ref[...]) * (1.0 - LAMBDA_INIT)).astype(o_ref.dtype)


def _diff_attn(q, kt, v_aug, table_flat, lamv, g_subln, tq, tk):
    assert tq % tk == 0 and tk % MAX_DISTANCE == 0
    s = q.shape[1]
    nq = s // tq
    nd = tq // tk + 1
    return pl.pallas_call(
        functools.partial(_attn_kernel, tq=tq, tk=tk, nd=nd),
        grid=(N_HEADS, nq),
        in_specs=[pl.BlockSpec((1, tq, V_DIM), lambda h, i: (h, i, 0)),
                  pl.BlockSpec((1, tq, V_DIM), lambda h, i: (h, jnp.minimum(i + 1, nq - 1), 0)),
                  pl.BlockSpec((1, s // tk, V_DIM, tk), lambda h, i: (h, 0, 0, 0)),
                  pl.BlockSpec((1, s, 2 * V_DIM), lambda h, i: (h, 0, 0)),
                  pl.BlockSpec(memory_space=pltpu.SMEM),
                  pl.BlockSpec((4, HEAD_DIM), lambda h, i: (0, 0)),
                  pl.BlockSpec((1, V_DIM), lambda h, i: (0, 0))],
        out_specs=pl.BlockSpec((tq, V_DIM), lambda h, i: (i, h)),
        out_shape=jax.ShapeDtypeStruct((s, ATTN_WIDTH), BF16),
        scratch_shapes=[pltpu.VMEM((2, 2 * tq, V_DIM), BF16),
                        pltpu.VMEM((2 * tq, tk), F32),
                        pltpu.VMEM((2 * tq, tk), F32),
                        pltpu.VMEM((2 * tq, LANES), F32),
                        pltpu.VMEM((2 * tq, LANES), F32),
                        pltpu.VMEM((2 * tq, LANES), F32),
                        pltpu.VMEM((2 * tq, 2 * V_DIM), F32),
                        pltpu.VMEM((nd + 2, tq, tk), F32)],
        compiler_params=pltpu.CompilerParams(dimension_semantics=("arbitrary", "arbitrary"),
                                             vmem_limit_bytes=VMEM_LIMIT),
        name="diff_attn",
    )(q, q, kt, v_aug, table_flat, lamv, g_subln)


def _ffn_kernel(x_ref, lru_ref, attn_ref, mod_ref, wo_ref, g2_ref, wup_ref, cw_ref, cb_ref, wdn_ref, gf_ref,
                o_ref, tail_sc, *, tm):
    @pl.when(pl.program_id(0) == 0)
    def _():
        tail_sc[...] = jnp.zeros(tail_sc.shape, F32)

    gate1 = mod_ref[:, 2 * D_MODEL:3 * D_MODEL]
    shift2 = mod_ref[:, 3 * D_MODEL:4 * D_MODEL]
    scale2 = mod_ref[:, 4 * D_MODEL:5 * D_MODEL]
    gate2 = mod_ref[:, 5 * D_MODEL:6 * D_MODEL]

    mix = (jnp.dot(lru_ref[...], wo_ref[0:LRU_WIDTH, :].astype(BF16), preferred_element_type=F32)
           + jnp.dot(attn_ref[...], wo_ref[LRU_WIDTH:, :].astype(BF16), preferred_element_type=F32))
    x1 = x_ref[...] + gate1 * mix
    h2 = (_rms_norm(x1, g2_ref[...]) * (1.0 + scale2) + shift2).astype(BF16)

    ff = jnp.zeros((tm, D_MODEL), F32)
    for c in range(D_FF // FFN_CK):
        lo = c * FFN_CK
        a = jnp.dot(h2, wup_ref[:, lo:lo + FFN_CK].astype(BF16), preferred_element_type=F32)
        g = jnp.dot(h2, wup_ref[:, D_FF + lo:D_FF + lo + FFN_CK].astype(BF16), preferred_element_type=F32)
        prev8 = tail_sc[:, lo:lo + FFN_CK]
        ac = cb_ref[:, lo:lo + FFN_CK]
        for j in range(CONV_FFN):
            sh = CONV_FFN - 1 - j
            a_s = a if sh == 0 else _shift_rows(prev8, a, sh)
            ac = ac + a_s * cw_ref[j:j + 1, lo:lo + FFN_CK]
        tail_sc[:, lo:lo + FFN_CK] = a[tm - SUBLANES:]
        u = (_gelu_tanh(ac) * g).astype(BF16)
        ff = ff + jnp.dot(u, wdn_ref[lo:lo + FFN_CK, :].astype(BF16), preferred_element_type=F32)

    x2 = x1 + gate2 * ff
    o_ref[...] = _rms_norm(x2, gf_ref[...])


def _out_ffn(x2d, lru, attn, mod, w_out, g2, w_up, conv_w, conv_b, w_down, g_final):
    s = x2d.shape[0]
    tm = min(FFN_TM, s)
    row = lambda w: pl.BlockSpec((tm, w), lambda i: (i, 0))
    const = lambda shape: pl.BlockSpec(shape, lambda i: (0, 0), pipeline_mode=pl.Buffered(1))
    return pl.pallas_call(
        functools.partial(_ffn_kernel, tm=tm),
        grid=(s // tm,),
        in_specs=[row(D_MODEL), row(LRU_WIDTH), row(ATTN_WIDTH),
                  const((1, 6 * D_MODEL)),
                  const((D_MODEL, D_MODEL)), const((1, D_MODEL)),
                  const((D_MODEL, 2 * D_FF)),
                  const((CONV_FFN, D_FF)), const((1, D_FF)),
                  const((D_FF, D_MODEL)), const((1, D_MODEL))],
        out_specs=row(D_MODEL),
        out_shape=jax.ShapeDtypeStruct((s, D_MODEL), F32),
        scratch_shapes=[pltpu.VMEM((SUBLANES, D_FF), F32)],
        compiler_params=pltpu.CompilerParams(dimension_semantics=("arbitrary",),
                                             vmem_limit_bytes=FFN_VMEM_LIMIT),
        name="out_ffn",
    )(x2d, lru, attn, mod, w_out, g2, w_up, conv_w, conv_b, w_down, g_final)


def _block_diag(w):
    nb, bs, _ = w.shape
    eye = jnp.eye(nb, dtype=w.dtype)
    return (w[:, :, None, :] * eye[:, None, :, None]).reshape(nb * bs, nb * bs)


def kernel(x, c, w_ada, b_ada, g_norm1, w_in, conv_lru_w, conv_lru_b, lru_wa, lru_ba, lru_wx, lru_bx, lru_lambda,
           lam_q1, lam_k1, lam_q2, lam_k2, g_subln, w_out, g_norm2, w_up, conv_ffn_w, conv_ffn_b, w_down, rel_bias,
           g_final):
    b, s, d = x.shape
    assert b == 1 and d == D_MODEL and w_ada.shape[0] == 1
    x2d = x.reshape(s, d)
    row = lambda a: a.reshape(1, -1)

    mod = _adaln_mod(c.reshape(d, 1), w_ada[0], row(b_ada[0]))

    tq, tk = min(ATT_TQ, s), min(ATT_TK, s)
    w_gates = (0.5 * jnp.concatenate([_block_diag(lru_wa[0]), _block_diag(lru_wx[0])], axis=1)).astype(BF16)
    b_gates = 0.5 * jnp.concatenate([lru_ba[0], lru_bx[0]]).reshape(1, -1)
    q, kt, v, lru = _in_proj_lru(x2d, mod, row(g_norm1[0]), w_in[0], conv_lru_w[0], row(conv_lru_b[0]),
                                 w_gates, b_gates, row(lru_lambda[0]), tk)

    lamv = jnp.stack([lam_q1[0], lam_k1[0], lam_q2[0], lam_k2[0]])
    attn = _diff_attn(q, kt, v, rel_bias.T.reshape(-1), lamv, row(g_subln[0]), tq, tk)

    out = _out_ffn(x2d, lru, attn, mod, w_out[0], row(g_norm2[0]), w_up[0],
                   conv_ffn_w[0], row(conv_ffn_b[0]), w_down[0], row(g_final))
    return out.reshape(b, s, d)
```
